```python
import math
import jax
import jax.numpy as jnp
from jax import lax
import numpy as np

D_MODEL = 1024
BATCH = 4
SEQ = 8192
DEPTH = 2

GRID_W = 64
CTX_LEN = 256
CHUNK = 128
A_GROUPS = 8
A_WIDTH = 1024
A_GROUP_DIM = A_WIDTH // A_GROUPS
B_HEADS = 8
B_QK_DIM = 64
B_V_DIM = 2 * B_QK_DIM
ROPE_BASE = 10000.0
Q_BLOCK = 128
N_EXPERTS = 16
N_GROUPS = 4
EXPERTS_PER_GROUP = N_EXPERTS // N_GROUPS
TOP_K = 2
D_EXPERT = 512
N_MOD = 6
LN_EPS = 1e-5
RMS_EPS = 1e-5
DEEPNORM_ALPHA = (2.0 * DEPTH) ** 0.25
DEEPNORM_BETA = (8.0 * DEPTH) ** -0.25
U_END = A_WIDTH
V_END = 2 * A_WIDTH
Q_END = V_END + B_HEADS * 2 * B_QK_DIM
K_END = Q_END + B_HEADS * 2 * B_QK_DIM
VB_END = K_END + B_HEADS * B_V_DIM
GA_END = VB_END + D_MODEL
IN_COLS = GA_END + D_MODEL

kernel_name = 'hybrid_gmlp_diffattn_grouped_moe_dit'


def _layernorm(x, g=None, b=None):
    xf = x.astype(jnp.float32)
    mu = jnp.mean(xf, axis=-1, keepdims=True)
    var = jnp.mean(jnp.square(xf - mu), axis=-1, keepdims=True)
    y = (xf - mu) * lax.rsqrt(var + LN_EPS)
    if g is not None:
        y = y * g.astype(jnp.float32) + b.astype(jnp.float32)
    return y.astype(x.dtype)


def _rmsnorm(x, g):
    xf = x.astype(jnp.float32)
    y = xf * lax.rsqrt(jnp.mean(jnp.square(xf), axis=-1, keepdims=True) + RMS_EPS)
    return (y * g.astype(jnp.float32)).astype(x.dtype)


def _modulation(cond, w_mod, b_mod):
    m = jax.nn.silu(cond) @ w_mod + b_mod
    return jnp.split(m[..., None, :], N_MOD, axis=-1)


def _modulate(x, shift, scale):
    return _layernorm(x) * (1.0 + scale) + shift


def _deepnorm(x, sub, g, b):
    return _layernorm(DEEPNORM_ALPHA * x + sub, g, b)


def _axial_rope(n_tokens):
    rows = n_tokens // GRID_W
    row = jnp.repeat(jnp.arange(rows, dtype=jnp.float32), GRID_W)
    col = jnp.tile(jnp.arange(GRID_W, dtype=jnp.float32), rows)
    n_freq = B_QK_DIM // 4
    inv_freq = ROPE_BASE ** (-jnp.arange(n_freq, dtype=jnp.float32) / n_freq)
    ang = jnp.stack([row, col], axis=-1)[..., None] * inv_freq
    return jnp.cos(ang), jnp.sin(ang)


def _apply_axial_rope(x, cos, sin):
    shp = x.shape
    xr = x.reshape(shp[:-1] + (2, B_QK_DIM // 2))
    x1, x2 = xr[..., :B_QK_DIM // 4], xr[..., B_QK_DIM // 4:]
    c = cos[None, :, None, None].astype(x.dtype)
    s = sin[None, :, None, None].astype(x.dtype)
    out = jnp.concatenate([x1 * c - x2 * s, x2 * c + x1 * s], axis=-1)
    return out.reshape(shp)


def _project(h, w_in):
    bsz, n, _ = h.shape
    u, v, q, k, vb, ga, gb = jnp.split(h @ w_in, [U_END, V_END, Q_END, K_END, VB_END, GA_END], axis=-1)
    q = q.reshape(bsz, n, B_HEADS, 2, B_QK_DIM)
    k = k.reshape(bsz, n, B_HEADS, 2, B_QK_DIM)
    vb = vb.reshape(bsz, n, B_HEADS, B_V_DIM)
    return jax.nn.gelu(u), jax.nn.gelu(v), q, k, vb, ga, gb


def _project_kv(h, w_in):
    bsz, n, _ = h.shape
    k, vb = jnp.split(h @ w_in[:, Q_END:VB_END], [K_END - Q_END], axis=-1)
    return k.reshape(bsz, n, B_HEADS, 2, B_QK_DIM), vb.reshape(bsz, n, B_HEADS, B_V_DIM)


def _chunk_mlp(u, v, sgu_g, sgu_b, w_s, b_s):
    bsz, n, _ = v.shape
    vn = _layernorm(v, sgu_g, sgu_b).reshape(bsz, n // CHUNK, CHUNK, A_GROUPS, A_GROUP_DIM)
    y = jnp.einsum('gpq,bcqgd->bcpgd', w_s, vn) + b_s.T[None, None, :, :, None]
    return u * y.reshape(bsz, n, A_WIDTH)


def _diff_attend(q, k, v, lam):
    s = jnp.einsum('bqhmd,bkhmd->bhmqk', q, k).astype(jnp.float32) * (B_QK_DIM ** -0.5)
    p = jax.nn.softmax(s, axis=-1)
    a = p[:, :, 0] - lam * p[:, :, 1]
    return jnp.einsum('bhqk,bkhe->bqhe', a.astype(v.dtype), v)


def _diff_heads_out(o, subln_g, lam_init):
    bsz, n = o.shape[0], o.shape[1]
    return (_rmsnorm(o, subln_g) * (1.0 - lam_init)).reshape(bsz, n, B_HEADS * B_V_DIM)


def _merge(ya, yb, ga, gb, w_pa, w_pb, w_o):
    return (jax.nn.sigmoid(ga) * (ya @ w_pa) + jax.nn.sigmoid(gb) * (yb @ w_pb)) @ w_o


def _route(h, w_router, b_router):
    logits = (h @ w_router).astype(jnp.float32) + b_router.astype(jnp.float32)
    probs = jax.nn.softmax(logits, axis=-1)
    grouped = probs.reshape(probs.shape[:-1] + (N_GROUPS, EXPERTS_PER_GROUP))
    group_score = jnp.sum(lax.top_k(grouped, TOP_K)[0], axis=-1)
    best = jnp.argmax(group_score, axis=-1)
    in_group = jnp.arange(N_GROUPS) == best[..., None]
    masked = jnp.where(in_group[..., None], grouped, -1.0).reshape(probs.shape)
    top_p, top_i = lax.top_k(masked, TOP_K)
    top_p = top_p / jnp.sum(top_p, axis=-1, keepdims=True)
    return jnp.sum(jax.nn.one_hot(top_i, N_EXPERTS, dtype=jnp.float32) * top_p[..., None], axis=-2)


def _moe(h, gates, w1, w3, w2):
    out = jnp.zeros_like(h)
    for e in range(N_EXPERTS):
        hid = jax.nn.silu(h @ w1[e]) * (h @ w3[e])
        out = out + gates[..., e:e + 1].astype(h.dtype) * (hid @ w2[e])
    return out


def setup_inputs(seed: int = 0) -> dict:
    key = jax.random.key(seed)
    ks = jax.random.split(key, 26)
    f32 = jnp.float32
    L, D = DEPTH, D_MODEL

    def nrm(k, shape, scale):
        return jax.random.normal(k, shape, f32) * scale

    return {
        'x': nrm(ks[0], (BATCH, SEQ, D), 1.0),
        'c': nrm(ks[1], (BATCH, D), 1.0),
        'ctx': nrm(ks[2], (BATCH, CTX_LEN, D), 1.0),
        'c_ctx': nrm(ks[3], (D,), 1.0),
        'w_mod': nrm(ks[4], (L, D, N_MOD * D), 0.5 * D ** -0.5),
        'b_mod': nrm(ks[5], (L, N_MOD * D), 0.02),
        'w_in': nrm(ks[6], (L, D, IN_COLS), D ** -0.5),
        'sgu_g': 1.0 + nrm(ks[7], (L, A_WIDTH), 0.02),
        'sgu_b': nrm(ks[8], (L, A_WIDTH), 0.02),
        'w_s': nrm(ks[9], (L, A_GROUPS, CHUNK, CHUNK), CHUNK ** -0.5),
        'b_s': 1.0 + nrm(ks[10], (L, A_GROUPS, CHUNK), 0.02),
        'lambda_q': nrm(ks[11], (L, 2, B_QK_DIM), 0.1),
        'lambda_k': nrm(ks[12], (L, 2, B_QK_DIM), 0.1),
        'subln_g': 1.0 + nrm(ks[13], (L, B_V_DIM), 0.02),
        'w_pa': nrm(ks[14], (L, A_WIDTH, D), DEEPNORM_BETA * A_WIDTH ** -0.5),
        'w_pb': nrm(ks[15], (L, B_HEADS * B_V_DIM, D), DEEPNORM_BETA * (B_HEADS * B_V_DIM) ** -0.5),
        'w_o': nrm(ks[16], (L, D, D), DEEPNORM_BETA * D ** -0.5),
        'ln1_g': 1.0 + nrm(ks[17], (L, D), 0.02),
        'ln1_b': nrm(ks[18], (L, D), 0.02),
        'w_router': nrm(ks[19], (D, N_EXPERTS), D ** -0.5),
        'b_router': nrm(ks[20], (N_EXPERTS,), 0.01),
        'w1': nrm(ks[21], (L, N_EXPERTS, D, D_EXPERT), D ** -0.5),
        'w3': nrm(ks[22], (L, N_EXPERTS, D, D_EXPERT), D ** -0.5),
        'w2': nrm(ks[23], (L, N_EXPERTS, D_EXPERT, D), DEEPNORM_BETA * D_EXPERT ** -0.5),
        'ln2_g': 1.0 + nrm(ks[24], (L, D), 0.02),
        'ln2_b': nrm(ks[25], (L, D), 0.02),
    }


def reference(x, c, ctx, c_ctx, w_mod, b_mod, w_in, sgu_g, sgu_b, w_s, b_s, lambda_q, lambda_k, subln_g,
              w_pa, w_pb, w_o, ln1_g, ln1_b, w_router, b_router, w1, w3, w2, ln2_g, ln2_b):
    bsz, n_lat, _ = x.shape
    n_blocks = n_lat // Q_BLOCK
    cos, sin = _axial_rope(n_lat)
    for l in range(DEPTH):
        last = l == DEPTH - 1
        lam_init = 0.8 - 0.6 * math.exp(-0.3 * l)
        lq = lambda_q[l].astype(jnp.float32)
        lk = lambda_k[l].astype(jnp.float32)
        lam = jnp.exp(jnp.sum(lq[0] * lk[0])) - jnp.exp(jnp.sum(lq[1] * lk[1])) + lam_init
        sh1, sc1, gt1, sh2, sc2, gt2 = _modulation(c, w_mod[l], b_mod[l])
        csh1, csc1, cgt1, csh2, csc2, cgt2 = _modulation(c_ctx, w_mod[l], b_mod[l])

        h = _modulate(x, sh1, sc1)
        hc = _modulate(ctx, csh1, csc1)
        u, v, q, k, vb, ga, gb = _project(h, w_in[l])
        if last:
            ck, cvb = _project_kv(hc, w_in[l])
        else:
            cu, cv, cq, ck, cvb, cga, cgb = _project(hc, w_in[l])
        q = _apply_axial_rope(q, cos, sin)
        k = _apply_axial_rope(k, cos, sin)
        k_all = jnp.concatenate([ck, k], axis=1)
        v_all = jnp.concatenate([cvb, vb], axis=1)
        q_blocks = q.reshape(bsz, n_blocks, Q_BLOCK, B_HEADS, 2, B_QK_DIM).swapaxes(0, 1)
        o = lax.map(lambda qb: _diff_attend(qb, k_all, v_all, lam), q_blocks)
        o = o.swapaxes(0, 1).reshape(bsz, n_lat, B_HEADS, B_V_DIM)
        ya = _chunk_mlp(u, v, sgu_g[l], sgu_b[l], w_s[l], b_s[l])
        yb = _diff_heads_out(o, subln_g[l], lam_init)
        mix = _merge(ya, yb, ga, gb, w_pa[l], w_pb[l], w_o[l])
        x = _deepnorm(x, gt1 * mix, ln1_g[l], ln1_b[l])
        if not last:
            cya = _chunk_mlp(cu, cv, sgu_g[l], sgu_b[l], w_s[l], b_s[l])
            cyb = _diff_heads_out(_diff_attend(cq, ck, cvb, lam), subln_g[l], lam_init)
            cmix = _merge(cya, cyb, cga, cgb, w_pa[l], w_pb[l], w_o[l])
            ctx = _deepnorm(ctx, cgt1 * cmix, ln1_g[l], ln1_b[l])

        h = _modulate(x, sh2, sc2)
        y = _moe(h, _route(h, w_router, b_router), w1[l], w3[l], w2[l])
        x = _deepnorm(x, gt2 * y, ln2_g[l], ln2_b[l])
        if not last:
            hc = _modulate(ctx, csh2, csc2)
            cy = _moe(hc, _route(hc, w_router, b_router), w1[l], w3[l], w2[l])
            ctx = _deepnorm(ctx, cgt2 * cy, ln2_g[l], ln2_b[l])
    return x
```

```python
import functools
import math

import jax
import jax.numpy as jnp
from jax import lax
from jax.experimental import pallas as pl
from jax.experimental.pallas import tpu as pltpu

F32 = jnp.float32
BF16 = jnp.bfloat16

GRID_W = 64
CHUNK = 128
A_GROUPS = 8
A_WIDTH = 1024
HEADS = 8
QK_DIM = 64
V_DIM = 2 * QK_DIM
HEAD_COLS = HEADS * V_DIM
ROPE_BASE = 10000.0
N_EXPERTS = 16
N_GROUPS = 4
EXPERTS_PER_GROUP = N_EXPERTS // N_GROUPS
N_MOD = 6
GATE_LANES = 128
LN_EPS = 1e-5
RMS_EPS = 1e-5
NEG_BIG = -1e30

VMEM_LIMIT_BYTES = 56 * 1024 * 1024

TOKEN_TILE = 256
ATTN_Q_TILE = 512
ATTN_KV_CHUNK = 256
MOE_TOKEN_TILE = 1024


def _params(semantics):
    return pltpu.CompilerParams(dimension_semantics=semantics, vmem_limit_bytes=VMEM_LIMIT_BYTES)


def _resident(shape):
    nd = len(shape)
    return pl.BlockSpec(shape, lambda *_: (0,) * nd, pipeline_mode=pl.Buffered(1))


def _ln_rows(x):
    mu = jnp.mean(x, axis=-1, keepdims=True)
    xc = x - mu
    var = jnp.mean(xc * xc, axis=-1, keepdims=True)
    return xc * lax.rsqrt(var + LN_EPS)


def _mod_kernel(cond_ref, w_ref, b_ref, o_ref):
    c = cond_ref[...]
    s = c * jax.nn.sigmoid(c)
    o_ref[...] = jnp.dot(s, w_ref[...], preferred_element_type=F32, precision=lax.Precision.HIGHEST) + b_ref[...]


def _modulation(cond, w_mod, b_mod):
    rows, d = cond.shape
    return pl.pallas_call(
        _mod_kernel,
        grid=(N_MOD,),
        in_specs=[
            pl.BlockSpec((rows, d), lambda j: (0, 0)),
            pl.BlockSpec((d, d), lambda j: (0, j)),
            pl.BlockSpec((1, d), lambda j: (0, j)),
        ],
        out_specs=pl.BlockSpec((rows, d), lambda j: (0, j)),
        out_shape=jax.ShapeDtypeStruct((rows, N_MOD * d), F32),
        compiler_params=_params(("arbitrary",)),
        name="modulation",
    )(cond, w_mod, b_mod.reshape(1, -1))


def _rope(x, cos, sin_signed):
    n = x.shape[-1]
    reps = n // cos.shape[-1]
    c = jnp.tile(cos, (1, reps))
    s = jnp.tile(sin_signed, (1, reps))
    lane = lax.broadcasted_iota(jnp.int32, x.shape, 1)
    first = (lane & 31) < 16
    partner = jnp.where(first, pltpu.roll(x, n - 16, 1), pltpu.roll(x, 16, 1))
    return x * c + partner * s


def _inproj_kernel(x_ref, mod_ref, w_ref, sg_ref, sb_ref, ws_ref, bs_ref, cos_ref, sin_ref,
                   ya_ref, qt_ref, k_ref, vt_ref, ga_ref, gb_ref, *, d_model, use_rope, q_scale):
    d = d_model
    tm = x_ref.shape[1]
    shift = mod_ref[0, :, 0:d]
    scale = mod_ref[0, :, d:2 * d]
    h = (_ln_rows(x_ref[0]) * (1.0 + scale) + shift).astype(BF16)

    def proj(c0, width):
        return jnp.dot(h, w_ref[:, c0:c0 + width], preferred_element_type=F32)

    c_u, c_v = 0, A_WIDTH
    c_q = 2 * A_WIDTH
    c_k = c_q + HEAD_COLS
    c_vb = c_k + HEAD_COLS
    c_ga = c_vb + HEAD_COLS
    c_gb = c_ga + d

    u = jax.nn.gelu(proj(c_u, A_WIDTH))
    v = jax.nn.gelu(proj(c_v, A_WIDTH))
    vn = (_ln_rows(v) * sg_ref[...] + sb_ref[...]).astype(BF16)
    gd = A_WIDTH // A_GROUPS
    for pair in range(tm // (2 * CHUNK)):
        r0 = pair * 2 * CHUNK
        r1 = r0 + CHUNK
        for g in range(A_GROUPS):
            cols = slice(g * gd, (g + 1) * gd)
            rhs = jnp.concatenate([vn[r0:r0 + CHUNK, cols], vn[r1:r1 + CHUNK, cols]], axis=1)
            y = jnp.dot(ws_ref[g], rhs, preferred_element_type=F32)
            bias = bs_ref[g]
            ya_ref[0, r0:r0 + CHUNK, cols] = (u[r0:r0 + CHUNK, cols] * (y[:, :gd] + bias)).astype(BF16)
            ya_ref[0, r1:r1 + CHUNK, cols] = (u[r1:r1 + CHUNK, cols] * (y[:, gd:] + bias)).astype(BF16)

    q = proj(c_q, HEAD_COLS)
    k = proj(c_k, HEAD_COLS)
    if use_rope:
        q = _rope(q, cos_ref[...], sin_ref[...])
        k = _rope(k, cos_ref[...], sin_ref[...])
    q = q * q_scale
    k_ref[0] = k.astype(BF16)
    vb = proj(c_vb, HEAD_COLS)
    for hd in range(HEADS):
        cols = slice(hd * V_DIM, (hd + 1) * V_DIM)
        qt_ref[0, hd] = q[:, cols].T.astype(BF16)
        vt_ref[0, hd] = vb[:, cols].T.astype(BF16)

    ga_ref[0] = jax.nn.sigmoid(proj(c_ga, d)).astype(BF16)
    gb_ref[0] = jax.nn.sigmoid(proj(c_gb, d)).astype(BF16)


def _inproj(x, mod, w_in, sgu_g, sgu_b, w_s, b_s_b, cos_t, sin_t, *, use_rope):
    bsz, n, d = x.shape
    tm = min(TOKEN_TILE, n)
    nt = n // tm
    in_cols = w_in.shape[1]
    q_scale = (QK_DIM ** -0.5) * math.log2(math.e)
    tok = lambda b, i: (b, i, 0)
    tr = lambda b, i: (b, 0, 0, i)
    rope_map = (lambda b, i: (i, 0)) if use_rope else (lambda b, i: (0, 0))
    return pl.pallas_call(
        functools.partial(_inproj_kernel, d_model=d, use_rope=use_rope, q_scale=q_scale),
        grid=(bsz, nt),
        in_specs=[
            pl.BlockSpec((1, tm, d), tok),
            pl.BlockSpec((1, 1, N_MOD * d), lambda b, i: (b, 0, 0)),
            _resident((d, in_cols)),
            _resident((1, A_WIDTH)),
            _resident((1, A_WIDTH)),
            _resident((A_GROUPS, CHUNK, CHUNK)),
            _resident((A_GROUPS, CHUNK, CHUNK)),
            pl.BlockSpec((tm, 2 * QK_DIM), rope_map),
            pl.BlockSpec((tm, 2 * QK_DIM), rope_map),
        ],
        out_specs=[
            pl.BlockSpec((1, tm, A_WIDTH), tok),
            pl.BlockSpec((1, HEADS, V_DIM, tm), tr),
            pl.BlockSpec((1, tm, HEAD_COLS), tok),
            pl.BlockSpec((1, HEADS, V_DIM, tm), tr),
            pl.BlockSpec((1, tm, d), tok),
            pl.BlockSpec((1, tm, d), tok),
        ],
        out_shape=[
            jax.ShapeDtypeStruct((bsz, n, A_WIDTH), BF16),
            jax.ShapeDtypeStruct((bsz, HEADS, V_DIM, n), BF16),
            jax.ShapeDtypeStruct((bsz, n, HEAD_COLS), BF16),
            jax.ShapeDtypeStruct((bsz, HEADS, V_DIM, n), BF16),
            jax.ShapeDtypeStruct((bsz, n, d), BF16),
            jax.ShapeDtypeStruct((bsz, n, d), BF16),
        ],
        compiler_params=_params(("parallel", "parallel")),
        name="inproj_rope" if use_rope else "inproj_ctx",
    )(x, mod, w_in, sgu_g, sgu_b, w_s, b_s_b, cos_t, sin_t)


def _attn_kernel(*refs, n_lat_chunks, tk, lam_init):
    if n_lat_chunks:
        (qt_ref, kl_ref, vtl_ref, kc_ref, vtc_ref, lq_ref, lk_ref, sg_ref,
         o_ref, rhs_ref, m_ref, l_ref, acc_ref) = refs
    else:
        (qt_ref, kc_ref, vtc_ref, lq_ref, lk_ref, sg_ref,
         o_ref, rhs_ref, m_ref, l_ref, acc_ref) = refs
    tq = qt_ref.shape[-1]

    qt = qt_ref[0, 0]
    row = lax.broadcasted_iota(jnp.int32, qt.shape, 0)
    zero = jnp.zeros_like(qt)
    rhs_ref[:, 0:tq] = jnp.where(row < QK_DIM, qt, zero)
    rhs_ref[:, tq:2 * tq] = jnp.where(row >= QK_DIM, qt, zero)
    m_ref[...] = jnp.full(m_ref.shape, NEG_BIG, F32)
    l_ref[...] = jnp.zeros(l_ref.shape, F32)
    acc_ref[...] = jnp.zeros(acc_ref.shape, F32)

    def step(kc, vtc):
        s = jnp.dot(kc, rhs_ref[...], preferred_element_type=F32)
        m_old = m_ref[...]
        m_new = jnp.maximum(m_old, jnp.max(s, axis=0, keepdims=True))
        alpha = jnp.exp2(m_old - m_new)
        p = jnp.exp2(s - m_new)
        l_ref[...] = alpha * l_ref[...] + jnp.sum(p, axis=0, keepdims=True)
        acc_ref[...] = alpha * acc_ref[...] + jnp.dot(vtc, p.astype(BF16), preferred_element_type=F32)
        m_ref[...] = m_new

    if n_lat_chunks:
        def body(i, carry):
            off = pl.multiple_of(i * tk, tk)
            step(kl_ref[0, pl.ds(off, tk), :], vtl_ref[0, 0, :, pl.ds(off, tk)])
            return carry
        lax.fori_loop(0, n_lat_chunks, body, 0)
    step(kc_ref[0], vtc_ref[0, 0])

    lq = lq_ref[...]
    lk = lk_ref[...]
    lam = (jnp.exp(jnp.sum(lq[0:1] * lk[0:1], keepdims=True))
           - jnp.exp(jnp.sum(lq[1:2] * lk[1:2], keepdims=True)) + lam_init)
    acc = acc_ref[...]
    l = l_ref[...]
    o = acc[:, 0:tq] / l[:, 0:tq] - lam * (acc[:, tq:2 * tq] / l[:, tq:2 * tq])
    ms = jnp.mean(o * o, axis=0, keepdims=True)
    on = o * lax.rsqrt(ms + RMS_EPS)
    o_ref[0] = (on.T * sg_ref[...] * (1.0 - lam_init)).astype(BF16)


def _attention(qt, k_lat, vt_lat, k_ctx, vt_ctx, lq, lk, subln_g, *, lam_init):
    bsz, _, _, nq = qt.shape
    nc = k_ctx.shape[1]
    tq = min(ATTN_Q_TILE, nq)
    has_lat = k_lat is not None
    tk = ATTN_KV_CHUNK
    n_lat_chunks = (k_lat.shape[1] // tk) if has_lat else 0
    in_specs = [pl.BlockSpec((1, 1, V_DIM, tq), lambda b, h, i: (b, h, 0, i))]
    args = [qt]
    if has_lat:
        ns = k_lat.shape[1]
        in_specs += [pl.BlockSpec((1, ns, V_DIM), lambda b, h, i: (b, 0, h)),
                     pl.BlockSpec((1, 1, V_DIM, ns), lambda b, h, i: (b, h, 0, 0))]
        args += [k_lat, vt_lat]
    in_specs += [pl.BlockSpec((1, nc, V_DIM), lambda b, h, i: (b, 0, h)),
                 pl.BlockSpec((1, 1, V_DIM, nc), lambda b, h, i: (b, h, 0, 0)),
                 pl.BlockSpec((2, QK_DIM), lambda b, h, i: (0, 0)),
                 pl.BlockSpec((2, QK_DIM), lambda b, h, i: (0, 0)),
                 pl.BlockSpec((1, V_DIM), lambda b, h, i: (0, 0))]
    args += [k_ctx, vt_ctx, lq, lk, subln_g]
    return pl.pallas_call(
        functools.partial(_attn_kernel, n_lat_chunks=n_lat_chunks, tk=tk, lam_init=lam_init),
        grid=(bsz, HEADS, nq // tq),
        in_specs=in_specs,
        out_specs=pl.BlockSpec((1, tq, V_DIM), lambda b, h, i: (b, i, h)),
        out_shape=jax.ShapeDtypeStruct((bsz, nq, HEAD_COLS), BF16),
        scratch_shapes=[
            pltpu.VMEM((V_DIM, 2 * tq), BF16),
            pltpu.VMEM((1, 2 * tq), F32),
            pltpu.VMEM((1, 2 * tq), F32),
            pltpu.VMEM((V_DIM, 2 * tq), F32),
        ],
        compiler_params=_params(("parallel", "parallel", "arbitrary")),
        name="diff_attn_latent" if has_lat else "diff_attn_ctx",
    )(*args)


def _route_rows(logits_t):
    mx = jnp.max(logits_t, axis=0, keepdims=True)
    ex = jnp.exp(logits_t - mx)
    probs = ex / jnp.sum(ex, axis=0, keepdims=True)
    p = [probs[e:e + 1, :] for e in range(N_EXPERTS)]
    scores = []
    for g in range(N_GROUPS):
        a, b, c, d = p[4 * g:4 * g + 4]
        hi1, lo1 = jnp.maximum(a, b), jnp.minimum(a, b)
        hi2, lo2 = jnp.maximum(c, d), jnp.minimum(c, d)
        top1 = jnp.maximum(hi1, hi2)
        top2 = jnp.maximum(jnp.minimum(hi1, hi2), jnp.maximum(lo1, lo2))
        scores.append(top1 + top2)
    best = jnp.zeros_like(scores[0], dtype=jnp.int32)
    best_score = scores[0]
    for g in range(1, N_GROUPS):
        better = scores[g] > best_score
        best = jnp.where(better, g, best)
        best_score = jnp.where(better, scores[g], best_score)
    sel = []
    for e in range(N_EXPERTS):
        g = e // EXPERTS_PER_GROUP
        rank = jnp.zeros_like(best)
        for j in range(g * EXPERTS_PER_GROUP, (g + 1) * EXPERTS_PER_GROUP):
            if j == e:
                continue
            ahead = (p[j] >= p[e]) if j < e else (p[j] > p[e])
            rank = rank + ahead.astype(jnp.int32)
        sel.append((best == g) & (rank < 2))
    kept = [jnp.where(sel[e], p[e], 0.0) for e in range(N_EXPERTS)]
    denom = kept[0]
    for e in range(1, N_EXPERTS):
        denom = denom + kept[e]
    return jnp.concatenate([kp / denom for kp in kept], axis=0)


def _merge_kernel(x_ref, ya_ref, yb_ref, ga_ref, gb_ref, mod_ref, wpa_ref, wpb_ref, wo_ref,
                  g1_ref, b1_ref, wr_ref, br_ref, x1_ref, h2_ref, gates_ref, *, d_model, alpha):
    d = d_model
    tm = x_ref.shape[1]
    gate1 = mod_ref[0, :, 2 * d:3 * d]
    shift2 = mod_ref[0, :, 3 * d:4 * d]
    scale2 = mod_ref[0, :, 4 * d:5 * d]
    a = jnp.dot(ya_ref[0], wpa_ref[...], preferred_element_type=F32)
    b = jnp.dot(yb_ref[0], wpb_ref[...], preferred_element_type=F32)
    merged = (ga_ref[0].astype(F32) * a + gb_ref[0].astype(F32) * b).astype(BF16)
    mix = jnp.dot(merged, wo_ref[...], preferred_element_type=F32)
    x1 = _ln_rows(alpha * x_ref[0] + gate1 * mix) * g1_ref[...] + b1_ref[...]
    x1_ref[0] = x1
    h2 = _ln_rows(x1) * (1.0 + scale2) + shift2
    h2_ref[0] = h2.astype(BF16)
    logits_t = lax.dot_general(wr_ref[...], h2, (((1,), (1,)), ((), ())),
                               preferred_element_type=F32, precision=lax.Precision.HIGHEST)
    logits_t = logits_t + jnp.tile(br_ref[...], (1, tm // br_ref.shape[1]))
    gates_t = _route_rows(logits_t)
    pad = jnp.zeros((GATE_LANES - N_EXPERTS, tm), F32)
    gates_ref[0] = jnp.concatenate([gates_t, pad], axis=0).T


def _merge(x, ya, yb, ga, gb, mod, w_pa, w_pb, w_o, g1, b1, w_rt, b_r_b, *, alpha):
    bsz, n, d = x.shape
    tm = min(TOKEN_TILE, n)
    nt = n // tm
    tok = lambda b, i: (b, i, 0)
    return pl.pallas_call(
        functools.partial(_merge_kernel, d_model=d, alpha=alpha),
        grid=(bsz, nt),
        in_specs=[
            pl.BlockSpec((1, tm, d), tok),
            pl.BlockSpec((1, tm, A_WIDTH), tok),
            pl.BlockSpec((1, tm, HEAD_COLS), tok),
            pl.BlockSpec((1, tm, d), tok),
            pl.BlockSpec((1, tm, d), tok),
            pl.BlockSpec((1, 1, N_MOD * d), lambda b, i: (b, 0, 0)),
            _resident((A_WIDTH, d)),
            _resident((HEAD_COLS, d)),
            _resident((d, d)),
            _resident((1, d)),
            _resident((1, d)),
            _resident((N_EXPERTS, d)),
            _resident((N_EXPERTS, 128)),
        ],
        out_specs=[
            pl.BlockSpec((1, tm, d), tok),
            pl.BlockSpec((1, tm, d), tok),
            pl.BlockSpec((1, tm, GATE_LANES), tok),
        ],
        out_shape=[
            jax.ShapeDtypeStruct((bsz, n, d), F32),
            jax.ShapeDtypeStruct((bsz, n, d), BF16),
            jax.ShapeDtypeStruct((bsz, n, GATE_LANES), F32),
        ],
        compiler_params=_params(("parallel", "parallel")),
        name="merge_route",
    )(x, ya, yb, ga, gb, mod, w_pa, w_pb, w_o, g1, b1, w_rt, b_r_b)


def _moe_kernel(h_ref, gates_ref, x1_ref, mod_ref, w1_ref, w3_ref, w2_ref, g2_ref, b2_ref,
                o_ref, acc_ref, *, d_model, alpha):
    d = d_model
    e = pl.program_id(2)

    @pl.when(e == 0)
    def _():
        acc_ref[...] = jnp.zeros(acc_ref.shape, F32)

    h = h_ref[0]
    a = jnp.dot(h, w1_ref[0], preferred_element_type=F32)
    b = jnp.dot(h, w3_ref[0], preferred_element_type=F32)
    hid = (a * jax.nn.sigmoid(a) * b).astype(BF16)
    y = jnp.dot(hid, w2_ref[0], preferred_element_type=F32)
    gates = gates_ref[0]
    pick = lax.broadcasted_iota(jnp.int32, gates.shape, 1) == e
    gate_col = jnp.sum(jnp.where(pick, gates, 0.0), axis=1, keepdims=True)
    acc_ref[...] += gate_col * y

    @pl.when(e == N_EXPERTS - 1)
    def _():
        gate2 = mod_ref[0, :, 5 * d:6 * d]
        o_ref[0] = _ln_rows(alpha * x1_ref[0] + gate2 * acc_ref[...]) * g2_ref[...] + b2_ref[...]


def _moe(h2, gates_t, x1, mod, w1, w3, w2, g2, b2, *, alpha):
    bsz, n, d = x1.shape
    de = w1.shape[-1]
    tm = min(MOE_TOKEN_TILE, n)
    nt = n // tm
    tok = lambda b, i, e: (b, i, 0)
    return pl.pallas_call(
        functools.partial(_moe_kernel, d_model=d, alpha=alpha),
        grid=(bsz, nt, N_EXPERTS),
        in_specs=[
            pl.BlockSpec((1, tm, d), tok),
            pl.BlockSpec((1, tm, GATE_LANES), tok),
            pl.BlockSpec((1, tm, d), tok),
            pl.BlockSpec((1, 1, N_MOD * d), lambda b, i, e: (b, 0, 0)),
            pl.BlockSpec((1, d, de), lambda b, i, e: (e, 0, 0)),
            pl.BlockSpec((1, d, de), lambda b, i, e: (e, 0, 0)),
            pl.BlockSpec((1, de, d), lambda b, i, e: (e, 0, 0)),
            pl.BlockSpec((1, d), lambda b, i, e: (0, 0)),
            pl.BlockSpec((1, d), lambda b, i, e: (0, 0)),
        ],
        out_specs=pl.BlockSpec((1, tm, d), tok),
        out_shape=jax.ShapeDtypeStruct((bsz, n, d), F32),
        scratch_shapes=[pltpu.VMEM((tm, d), F32)],
        compiler_params=_params(("parallel", "parallel", "arbitrary")),
        name="moe_dense",
    )(h2, gates_t, x1, mod, w1, w3, w2, g2, b2)


def _rope_tables(n_tokens):
    rows = n_tokens // GRID_W
    row = jnp.repeat(jnp.arange(rows, dtype=F32), GRID_W)
    col = jnp.tile(jnp.arange(GRID_W, dtype=F32), rows)
    n_freq = QK_DIM // 4
    inv_freq = ROPE_BASE ** (-jnp.arange(n_freq, dtype=F32) / n_freq)
    ang_r = row[:, None] * inv_freq
    ang_c = col[:, None] * inv_freq
    cos64 = jnp.concatenate([jnp.cos(ang_r), jnp.cos(ang_r), jnp.cos(ang_c), jnp.cos(ang_c)], axis=-1)
    sin64 = jnp.concatenate([-jnp.sin(ang_r), jnp.sin(ang_r), -jnp.sin(ang_c), jnp.sin(ang_c)], axis=-1)
    return jnp.tile(cos64, (1, 2)), jnp.tile(sin64, (1, 2))


def kernel(x, c, ctx, c_ctx, w_mod, b_mod, w_in, sgu_g, sgu_b, w_s, b_s, lambda_q, lambda_k, subln_g,
           w_pa, w_pb, w_o, ln1_g, ln1_b, w_router, b_router, w1, w3, w2, ln2_g, ln2_b):
    bsz, n_lat, d = x.shape
    n_ctx = ctx.shape[1]
    depth = w_mod.shape[0]
    alpha = (2.0 * depth) ** 0.25
    cos_t, sin_t = _rope_tables(n_lat)

    cond = jnp.zeros((8, d), F32).at[0:bsz].set(c).at[bsz].set(c_ctx)
    w_rt = w_router.T
    b_r_b = jnp.broadcast_to(b_router[:, None], (N_EXPERTS, 128))
    row = lambda v: v.reshape(1, -1)

    for l in range(depth):
        last = l == depth - 1
        lam_init = 0.8 - 0.6 * math.exp(-0.3 * l)
        mod = _modulation(cond, w_mod[l], b_mod[l])
        mod_lat = mod[0:bsz, None, :]
        mod_ctx = jnp.broadcast_to(mod[bsz][None, None, :], (bsz, 1, N_MOD * d))
        w_in_l = w_in[l].astype(BF16)
        w_s_l = w_s[l].astype(BF16)
        b_s_b = jnp.broadcast_to(b_s[l][:, :, None], (A_GROUPS, CHUNK, CHUNK))
        proj_args = (w_in_l, row(sgu_g[l]), row(sgu_b[l]), w_s_l, b_s_b, cos_t, sin_t)
        attn_args = (lambda_q[l], lambda_k[l], row(subln_g[l]))
        merge_w = (w_pa[l].astype(BF16), w_pb[l].astype(BF16), w_o[l].astype(BF16),
                   row(ln1_g[l]), row(ln1_b[l]), w_rt, b_r_b)
        moe_w = (w1[l].astype(BF16), w3[l].astype(BF16), w2[l].astype(BF16), row(ln2_g[l]), row(ln2_b[l]))

        ya, qt, k, vt, ga, gb = _inproj(x, mod_lat, *proj_args, use_rope=True)
        cya, cqt, ck, cvt, cga, cgb = _inproj(ctx, mod_ctx, *proj_args, use_rope=False)
        yb = _attention(qt, k, vt, ck, cvt, *attn_args, lam_init=lam_init)
        x1, h2, gates = _merge(x, ya, yb, ga, gb, mod_lat, *merge_w, alpha=alpha)
        x = _moe(h2, gates, x1, mod_lat, *moe_w, alpha=alpha)
        if not last:
            cyb = _attention(cqt, None, None, ck, cvt, *attn_args, lam_init=lam_init)
            c1, ch2, cgates = _merge(ctx, cya, cyb, cga, cgb, mod_ctx, *merge_w, alpha=alpha)
            flat = lambda t: t.reshape(1, bsz * n_ctx, t.shape[-1])
            ctx = _moe(flat(ch2), flat(cgates), flat(c1), mod_ctx[0:1], *moe_w, alpha=alpha)
            ctx = ctx.reshape(bsz, n_ctx, d)
    return x
```

```python
import functools
import math

import jax
import jax.numpy as jnp
from jax import lax
from jax.experimental import pallas as pl
from jax.experimental.pallas import tpu as pltpu

F32 = jnp.float32
BF16 = jnp.bfloat16

GRID_W = 64
CHUNK = 128
A_GROUPS = 8
A_WIDTH = 1024
HEADS = 8
QK_DIM = 64
V_DIM = 2 * QK_DIM
HEAD_COLS = HEADS * V_DIM
ROPE_BASE = 10000.0
N_EXPERTS = 16
N_GROUPS = 4
EXPERTS_PER_GROUP = N_EXPERTS // N_GROUPS
N_MOD = 6
GATE_LANES = 128
LN_EPS = 1e-5
RMS_EPS = 1e-5
NEG_BIG = -1e30

VMEM_LIMIT_BYTES = 56 * 1024 * 1024

TOKEN_TILE = 256
ATTN_Q_TILE = 512
ATTN_KV_CHUNK = 512
SUM_ROWS = 16
MOE_TOKEN_TILE = 1024


def _params(semantics):
    return pltpu.CompilerParams(dimension_semantics=semantics, vmem_limit_bytes=VMEM_LIMIT_BYTES)


def _resident(shape):
    nd = len(shape)
    return pl.BlockSpec(shape, lambda *_: (0,) * nd, pipeline_mode=pl.Buffered(1))


def _ln_rows(x):
    mu = jnp.mean(x, axis=-1, keepdims=True)
    xc = x - mu
    var = jnp.mean(xc * xc, axis=-1, keepdims=True)
    return xc * lax.rsqrt(var + LN_EPS)


def _mod_kernel(cond_ref, w_ref, b_ref, o_ref):
    c = cond_ref[...]
    s = c * jax.nn.sigmoid(c)
    o_ref[...] = jnp.dot(s, w_ref[...], preferred_element_type=F32, precision=lax.Precision.HIGHEST) + b_ref[...]


def _modulation(cond, w_mod, b_mod):
    rows, d = cond.shape
    return pl.pallas_call(
        _mod_kernel,
        grid=(N_MOD,),
        in_specs=[
            pl.BlockSpec((rows, d), lambda j: (0, 0)),
            pl.BlockSpec((d, d), lambda j: (0, j)),
            pl.BlockSpec((1, d), lambda j: (0, j)),
        ],
        out_specs=pl.BlockSpec((rows, d), lambda j: (0, j)),
        out_shape=jax.ShapeDtypeStruct((rows, N_MOD * d), F32),
        compiler_params=_params(("arbitrary",)),
        name="modulation",
    )(cond, w_mod, b_mod.reshape(1, -1))


def _rope(x, cos, sin_signed):
    n = x.shape[-1]
    reps = n // cos.shape[-1]
    c = jnp.tile(cos, (1, reps))
    s = jnp.tile(sin_signed, (1, reps))
    lane = lax.broadcasted_iota(jnp.int32, x.shape, 1)
    first = (lane & 31) < 16
    partner = jnp.where(first, pltpu.roll(x, n - 16, 1), pltpu.roll(x, 16, 1))
    return x * c + partner * s


def _inproj_kernel(x_ref, mod_ref, w_ref, sg_ref, sb_ref, ws_ref, bs_ref, cos_ref, sin_ref,
                   ya_ref, qt_ref, k_ref, vt_ref, ga_ref, gb_ref, *, d_model, use_rope, q_scale):
    d = d_model
    tm = x_ref.shape[1]
    shift = mod_ref[0, :, 0:d]
    scale = mod_ref[0, :, d:2 * d]
    h = (_ln_rows(x_ref[0]) * (1.0 + scale) + shift).astype(BF16)

    def proj(c0, width):
        return jnp.dot(h, w_ref[:, c0:c0 + width], preferred_element_type=F32)

    c_u, c_v = 0, A_WIDTH
    c_q = 2 * A_WIDTH
    c_k = c_q + HEAD_COLS
    c_vb = c_k + HEAD_COLS
    c_ga = c_vb + HEAD_COLS
    c_gb = c_ga + d

    u = jax.nn.gelu(proj(c_u, A_WIDTH))
    v = jax.nn.gelu(proj(c_v, A_WIDTH))
    vn = (_ln_rows(v) * sg_ref[...] + sb_ref[...]).astype(BF16)
    gd = A_WIDTH // A_GROUPS
    for pair in range(tm // (2 * CHUNK)):
        r0 = pair * 2 * CHUNK
        r1 = r0 + CHUNK
        for g in range(A_GROUPS):
            cols = slice(g * gd, (g + 1) * gd)
            rhs = jnp.concatenate([vn[r0:r0 + CHUNK, cols], vn[r1:r1 + CHUNK, cols]], axis=1)
            y = jnp.dot(ws_ref[g], rhs, preferred_element_type=F32)
            bias = bs_ref[g]
            ya_ref[0, r0:r0 + CHUNK, cols] = (u[r0:r0 + CHUNK, cols] * (y[:, :gd] + bias)).astype(BF16)
            ya_ref[0, r1:r1 + CHUNK, cols] = (u[r1:r1 + CHUNK, cols] * (y[:, gd:] + bias)).astype(BF16)

    q = proj(c_q, HEAD_COLS)
    k = proj(c_k, HEAD_COLS)
    if use_rope:
        q = _rope(q, cos_ref[...], sin_ref[...])
        k = _rope(k, cos_ref[...], sin_ref[...])
    q = q * q_scale
    k_ref[0] = k.astype(BF16)
    vb = proj(c_vb, HEAD_COLS)
    for hd in range(HEADS):
        cols = slice(hd * V_DIM, (hd + 1) * V_DIM)
        qt_ref[0, hd] = q[:, cols].T.astype(BF16)
        vt_ref[0, hd] = vb[:, cols].T.astype(BF16)

    ga_ref[0] = jax.nn.sigmoid(proj(c_ga, d)).astype(BF16)
    gb_ref[0] = jax.nn.sigmoid(proj(c_gb, d)).astype(BF16)


def _inproj(x, mod, w_in, sgu_g, sgu_b, w_s, b_s_b, cos_t, sin_t, *, use_rope):
    bsz, n, d = x.shape
    tm = min(TOKEN_TILE, n)
    nt = n // tm
    in_cols = w_in.shape[1]
    q_scale = (QK_DIM ** -0.5) * math.log2(math.e)
    tok = lambda b, i: (b, i, 0)
    tr = lambda b, i: (b, 0, 0, i)
    rope_map = (lambda b, i: (i, 0)) if use_rope else (lambda b, i: (0, 0))
    return pl.pallas_call(
        functools.partial(_inproj_kernel, d_model=d, use_rope=use_rope, q_scale=q_scale),
        grid=(bsz, nt),
        in_specs=[
            pl.BlockSpec((1, tm, d), tok),
            pl.BlockSpec((1, 1, N_MOD * d), lambda b, i: (b, 0, 0)),
            _resident((d, in_cols)),
            _resident((1, A_WIDTH)),
            _resident((1, A_WIDTH)),
            _resident((A_GROUPS, CHUNK, CHUNK)),
            _resident((A_GROUPS, CHUNK, CHUNK)),
            pl.BlockSpec((tm, 2 * QK_DIM), rope_map),
            pl.BlockSpec((tm, 2 * QK_DIM), rope_map),
        ],
        out_specs=[
            pl.BlockSpec((1, tm, A_WIDTH), tok),
            pl.BlockSpec((1, HEADS, V_DIM, tm), tr),
            pl.BlockSpec((1, tm, HEAD_COLS), tok),
            pl.BlockSpec((1, HEADS, V_DIM, tm), tr),
            pl.BlockSpec((1, tm, d), tok),
            pl.BlockSpec((1, tm, d), tok),
        ],
        out_shape=[
            jax.ShapeDtypeStruct((bsz, n, A_WIDTH), BF16),
            jax.ShapeDtypeStruct((bsz, HEADS, V_DIM, n), BF16),
            jax.ShapeDtypeStruct((bsz, n, HEAD_COLS), BF16),
            jax.ShapeDtypeStruct((bsz, HEADS, V_DIM, n), BF16),
            jax.ShapeDtypeStruct((bsz, n, d), BF16),
            jax.ShapeDtypeStruct((bsz, n, d), BF16),
        ],
        compiler_params=_params(("parallel", "parallel")),
        name="inproj_rope" if use_rope else "inproj_ctx",
    )(x, mod, w_in, sgu_g, sgu_b, w_s, b_s_b, cos_t, sin_t)


def _attn_kernel(*refs, n_lat_chunks, tk, lam_init):
    if n_lat_chunks:
        (qt_ref, kl_ref, vtl_ref, kc_ref, vtc_ref, lq_ref, lk_ref, sg_ref,
         o_ref, rhs_ref, s_ref, cm_ref, m_ref, acc_ref) = refs
    else:
        (qt_ref, kc_ref, vtc_ref, lq_ref, lk_ref, sg_ref,
         o_ref, rhs_ref, s_ref, cm_ref, m_ref, acc_ref) = refs
    tq = qt_ref.shape[-1]
    nc = kc_ref.shape[1]

    qt = qt_ref[0, 0]
    row = lax.broadcasted_iota(jnp.int32, qt.shape, 0)
    zero = jnp.zeros_like(qt)
    rhs_ref[:, 0:tq] = jnp.where(row < QK_DIM, qt, zero)
    rhs_ref[:, tq:2 * tq] = jnp.where(row >= QK_DIM, qt, zero)
    m_ref[...] = jnp.full(m_ref.shape, NEG_BIG, F32)
    acc_ref[...] = jnp.zeros(acc_ref.shape, F32)

    def stage_a(kc, slot, rows):
        s = jnp.dot(kc, rhs_ref[...], preferred_element_type=F32)
        s_ref[slot, 0:rows, :] = s
        cm_ref[slot] = jnp.max(s, axis=0, keepdims=True)

    def stage_b(vtc, slot, rows):
        m_old = m_ref[...]
        m_new = jnp.maximum(m_old, cm_ref[slot])
        alpha = jnp.exp2(m_old - m_new)
        p = jnp.exp2(s_ref[slot, 0:rows, :] - m_new)
        vt_ones = jnp.concatenate([vtc, jnp.ones((SUM_ROWS, rows), BF16)], axis=0)
        acc_ref[...] = alpha * acc_ref[...] + jnp.dot(vt_ones, p.astype(BF16), preferred_element_type=F32)
        m_ref[...] = m_new

    if n_lat_chunks:
        def lat_k(c):
            return kl_ref[0, pl.ds(pl.multiple_of(c * tk, tk), tk), :]

        def lat_vt(c):
            return vtl_ref[0, 0, :, pl.ds(pl.multiple_of(c * tk, tk), tk)]

        stage_a(lat_k(0), 0, tk)

        def body(j, carry):
            c = 2 * j
            stage_a(lat_k(c + 1), 1, tk)
            stage_b(lat_vt(c), 0, tk)
            stage_a(lat_k(c + 2), 0, tk)
            stage_b(lat_vt(c + 1), 1, tk)
            return carry
        lax.fori_loop(0, n_lat_chunks // 2 - 1, body, 0)
        c = n_lat_chunks - 2
        stage_a(lat_k(c + 1), 1, tk)
        stage_b(lat_vt(c), 0, tk)
        stage_a(kc_ref[0], 0, nc)
        stage_b(lat_vt(c + 1), 1, tk)
        stage_b(vtc_ref[0, 0], 0, nc)
    else:
        stage_a(kc_ref[0], 0, nc)
        stage_b(vtc_ref[0, 0], 0, nc)

    lq = lq_ref[...]
    lk = lk_ref[...]
    lam = (jnp.exp(jnp.sum(lq[0:1] * lk[0:1], keepdims=True))
           - jnp.exp(jnp.sum(lq[1:2] * lk[1:2], keepdims=True)) + lam_init)
    acc = acc_ref[0:V_DIM, :]
    l = acc_ref[V_DIM:V_DIM + 1, :]
    o = acc[:, 0:tq] / l[:, 0:tq] - lam * (acc[:, tq:2 * tq] / l[:, tq:2 * tq])
    ms = jnp.mean(o * o, axis=0, keepdims=True)
    on = o * lax.rsqrt(ms + RMS_EPS)
    o_ref[0] = (on.T * sg_ref[...] * (1.0 - lam_init)).astype(BF16)


def _attention(qt, k_lat, vt_lat, k_ctx, vt_ctx, lq, lk, subln_g, *, lam_init):
    bsz, _, _, nq = qt.shape
    nc = k_ctx.shape[1]
    tq = min(ATTN_Q_TILE, nq)
    has_lat = k_lat is not None
    tk = ATTN_KV_CHUNK
    n_lat_chunks = (k_lat.shape[1] // tk) if has_lat else 0
    assert n_lat_chunks % 2 == 0
    s_rows = tk if has_lat else nc
    in_specs = [pl.BlockSpec((1, 1, V_DIM, tq), lambda b, h, i: (b, h, 0, i))]
    args = [qt]
    if has_lat:
        ns = k_lat.shape[1]
        in_specs += [pl.BlockSpec((1, ns, V_DIM), lambda b, h, i: (b, 0, h)),
                     pl.BlockSpec((1, 1, V_DIM, ns), lambda b, h, i: (b, h, 0, 0))]
        args += [k_lat, vt_lat]
    in_specs += [pl.BlockSpec((1, nc, V_DIM), lambda b, h, i: (b, 0, h)),
                 pl.BlockSpec((1, 1, V_DIM, nc), lambda b, h, i: (b, h, 0, 0)),
                 pl.BlockSpec((2, QK_DIM), lambda b, h, i: (0, 0)),
                 pl.BlockSpec((2, QK_DIM), lambda b, h, i: (0, 0)),
                 pl.BlockSpec((1, V_DIM), lambda b, h, i: (0, 0))]
    args += [k_ctx, vt_ctx, lq, lk, subln_g]
    return pl.pallas_call(
        functools.partial(_attn_kernel, n_lat_chunks=n_lat_chunks, tk=tk, lam_init=lam_init),
        grid=(bsz, HEADS, nq // tq),
        in_specs=in_specs,
        out_specs=pl.BlockSpec((1, tq, V_DIM), lambda b, h, i: (b, i, h)),
        out_shape=jax.ShapeDtypeStruct((bsz, nq, HEAD_COLS), BF16),
        scratch_shapes=[
            pltpu.VMEM((V_DIM, 2 * tq), BF16),
            pltpu.VMEM((2, s_rows, 2 * tq), F32),
            pltpu.VMEM((2, 1, 2 * tq), F32),
            pltpu.VMEM((1, 2 * tq), F32),
            pltpu.VMEM((V_DIM + SUM_ROWS, 2 * tq), F32),
        ],
        compiler_params=_params(("parallel", "parallel", "arbitrary")),
        name="diff_attn_latent" if has_lat else "diff_attn_ctx",
    )(*args)


def _route_rows(logits_t):
    mx = jnp.max(logits_t, axis=0, keepdims=True)
    ex = jnp.exp(logits_t - mx)
    probs = ex / jnp.sum(ex, axis=0, keepdims=True)
    p = [probs[e:e + 1, :] for e in range(N_EXPERTS)]
    scores = []
    for g in range(N_GROUPS):
        a, b, c, d = p[4 * g:4 * g + 4]
        hi1, lo1 = jnp.maximum(a, b), jnp.minimum(a, b)
        hi2, lo2 = jnp.maximum(c, d), jnp.minimum(c, d)
        top1 = jnp.maximum(hi1, hi2)
        top2 = jnp.maximum(jnp.minimum(hi1, hi2), jnp.maximum(lo1, lo2))
        scores.append(top1 + top2)
    best = jnp.zeros_like(scores[0], dtype=jnp.int32)
    best_score = scores[0]
    for g in range(1, N_GROUPS):
        better = scores[g] > best_score
        best = jnp.where(better, g, best)
        best_score = jnp.where(better, scores[g], best_score)
    sel = []
    for e in range(N_EXPERTS):
        g = e // EXPERTS_PER_GROUP
        rank = jnp.zeros_like(best)
        for j in range(g * EXPERTS_PER_GROUP, (g + 1) * EXPERTS_PER_GROUP):
            if j == e:
                continue
            ahead = (p[j] >= p[e]) if j < e else (p[j] > p[e])
            rank = rank + ahead.astype(jnp.int32)
        sel.append((best == g) & (rank < 2))
    kept = [jnp.where(sel[e], p[e], 0.0) for e in range(N_EXPERTS)]
    denom = kept[0]
    for e in range(1, N_EXPERTS):
        denom = denom + kept[e]
    return jnp.concatenate([kp / denom for kp in kept], axis=0)


def _merge_kernel(x_ref, ya_ref, yb_ref, ga_ref, gb_ref, mod_ref, wpa_ref, wpb_ref, wo_ref,
                  g1_ref, b1_ref, wr_ref, br_ref, x1_ref, h2_ref, gates_ref, *, d_model, alpha):
    d = d_model
    tm = x_ref.shape[1]
    gate1 = mod_ref[0, :, 2 * d:3 * d]
    shift2 = mod_ref[0, :, 3 * d:4 * d]
    scale2 = mod_ref[0, :, 4 * d:5 * d]
    a = jnp.dot(ya_ref[0], wpa_ref[...], preferred_element_type=F32)
    b = jnp.dot(yb_ref[0], wpb_ref[...], preferred_element_type=F32)
    merged = (ga_ref[0].astype(F32) * a + gb_ref[0].astype(F32) * b).astype(BF16)
    mix = jnp.dot(merged, wo_ref[...], preferred_element_type=F32)
    x1 = _ln_rows(alpha * x_ref[0] + gate1 * mix) * g1_ref[...] + b1_ref[...]
    x1_ref[0] = x1
    h2 = _ln_rows(x1) * (1.0 + scale2) + shift2
    h2_ref[0] = h2.astype(BF16)
    logits_t = lax.dot_general(wr_ref[...], h2, (((1,), (1,)), ((), ())),
                               preferred_element_type=F32, precision=lax.Precision.HIGHEST)
    logits_t = logits_t + jnp.tile(br_ref[...], (1, tm // br_ref.shape[1]))
    gates_t = _route_rows(logits_t)
    pad = jnp.zeros((GATE_LANES - N_EXPERTS, tm), F32)
    gates_ref[0] = jnp.concatenate([gates_t, pad], axis=0).T


def _merge(x, ya, yb, ga, gb, mod, w_pa, w_pb, w_o, g1, b1, w_rt, b_r_b, *, alpha):
    bsz, n, d = x.shape
    tm = min(TOKEN_TILE, n)
    nt = n // tm
    tok = lambda b, i: (b, i, 0)
    return pl.pallas_call(
        functools.partial(_merge_kernel, d_model=d, alpha=alpha),
        grid=(bsz, nt),
        in_specs=[
            pl.BlockSpec((1, tm, d), tok),
            pl.BlockSpec((1, tm, A_WIDTH), tok),
            pl.BlockSpec((1, tm, HEAD_COLS), tok),
            pl.BlockSpec((1, tm, d), tok),
            pl.BlockSpec((1, tm, d), tok),
            pl.BlockSpec((1, 1, N_MOD * d), lambda b, i: (b, 0, 0)),
            _resident((A_WIDTH, d)),
            _resident((HEAD_COLS, d)),
            _resident((d, d)),
            _resident((1, d)),
            _resident((1, d)),
            _resident((N_EXPERTS, d)),
            _resident((N_EXPERTS, 128)),
        ],
        out_specs=[
            pl.BlockSpec((1, tm, d), tok),
            pl.BlockSpec((1, tm, d), tok),
            pl.BlockSpec((1, tm, GATE_LANES), tok),
        ],
        out_shape=[
            jax.ShapeDtypeStruct((bsz, n, d), F32),
            jax.ShapeDtypeStruct((bsz, n, d), BF16),
            jax.ShapeDtypeStruct((bsz, n, GATE_LANES), F32),
        ],
        compiler_params=_params(("parallel", "parallel")),
        name="merge_route",
    )(x, ya, yb, ga, gb, mod, w_pa, w_pb, w_o, g1, b1, w_rt, b_r_b)


def _moe_kernel(h_ref, gates_ref, x1_ref, mod_ref, w1_ref, w3_ref, w2_ref, g2_ref, b2_ref,
                o_ref, acc_ref, *, d_model, alpha):
    d = d_model
    e = pl.program_id(2)

    @pl.when(e == 0)
    def _():
        acc_ref[...] = jnp.zeros(acc_ref.shape, F32)

    h = h_ref[0]
    a = jnp.dot(h, w1_ref[0], preferred_element_type=F32)
    b = jnp.dot(h, w3_ref[0], preferred_element_type=F32)
    hid = (a * jax.nn.sigmoid(a) * b).astype(BF16)
    y = jnp.dot(hid, w2_ref[0], preferred_element_type=F32)
    gates = gates_ref[0]
    pick = lax.broadcasted_iota(jnp.int32, gates.shape, 1) == e
    gate_col = jnp.sum(jnp.where(pick, gates, 0.0), axis=1, keepdims=True)
    acc_ref[...] += gate_col * y

    @pl.when(e == N_EXPERTS - 1)
    def _():
        gate2 = mod_ref[0, :, 5 * d:6 * d]
        o_ref[0] = _ln_rows(alpha * x1_ref[0] + gate2 * acc_ref[...]) * g2_ref[...] + b2_ref[...]


def _moe(h2, gates_t, x1, mod, w1, w3, w2, g2, b2, *, alpha):
    bsz, n, d = x1.shape
    de = w1.shape[-1]
    tm = min(MOE_TOKEN_TILE, n)
    nt = n // tm
    tok = lambda b, i, e: (b, i, 0)
    return pl.pallas_call(
        functools.partial(_moe_kernel, d_model=d, alpha=alpha),
        grid=(bsz, nt, N_EXPERTS),
        in_specs=[
            pl.BlockSpec((1, tm, d), tok),
            pl.BlockSpec((1, tm, GATE_LANES), tok),
            pl.BlockSpec((1, tm, d), tok),
            pl.BlockSpec((1, 1, N_MOD * d), lambda b, i, e: (b, 0, 0)),
            pl.BlockSpec((1, d, de), lambda b, i, e: (e, 0, 0)),
            pl.BlockSpec((1, d, de), lambda b, i, e: (e, 0, 0)),
            pl.BlockSpec((1, de, d), lambda b, i, e: (e, 0, 0)),
            pl.BlockSpec((1, d), lambda b, i, e: (0, 0)),
            pl.BlockSpec((1, d), lambda b, i, e: (0, 0)),
        ],
        out_specs=pl.BlockSpec((1, tm, d), tok),
        out_shape=jax.ShapeDtypeStruct((bsz, n, d), F32),
        scratch_shapes=[pltpu.VMEM((tm, d), F32)],
        compiler_params=_params(("parallel", "parallel", "arbitrary")),
        name="moe_dense",
    )(h2, gates_t, x1, mod, w1, w3, w2, g2, b2)


def _rope_tables(n_tokens):
    rows = n_tokens // GRID_W
    row = jnp.repeat(jnp.arange(rows, dtype=F32), GRID_W)
    col = jnp.tile(jnp.arange(GRID_W, dtype=F32), rows)
    n_freq = QK_DIM // 4
    inv_freq = ROPE_BASE ** (-jnp.arange(n_freq, dtype=F32) / n_freq)
    ang_r = row[:, None] * inv_freq
    ang_c = col[:, None] * inv_freq
    cos64 = jnp.concatenate([jnp.cos(ang_r), jnp.cos(ang_r), jnp.cos(ang_c), jnp.cos(ang_c)], axis=-1)
    sin64 = jnp.concatenate([-jnp.sin(ang_r), jnp.sin(ang_r), -jnp.sin(ang_c), jnp.sin(ang_c)], axis=-1)
    return jnp.tile(cos64, (1, 2)), jnp.tile(sin64, (1, 2))


def kernel(x, c, ctx, c_ctx, w_mod, b_mod, w_in, sgu_g, sgu_b, w_s, b_s, lambda_q, lambda_k, subln_g,
           w_pa, w_pb, w_o, ln1_g, ln1_b, w_router, b_router, w1, w3, w2, ln2_g, ln2_b):
    bsz, n_lat, d = x.shape
    n_ctx = ctx.shape[1]
    depth = w_mod.shape[0]
    alpha = (2.0 * depth) ** 0.25
    cos_t, sin_t = _rope_tables(n_lat)

    cond = jnp.zeros((8, d), F32).at[0:bsz].set(c).at[bsz].set(c_ctx)
    w_rt = w_router.T
    b_r_b = jnp.broadcast_to(b_router[:, None], (N_EXPERTS, 128))
    row = lambda v: v.reshape(1, -1)

    for l in range(depth):
        last = l == depth - 1
        lam_init = 0.8 - 0.6 * math.exp(-0.3 * l)
        mod = _modulation(cond, w_mod[l], b_mod[l])
        mod_lat = mod[0:bsz, None, :]
        mod_ctx = jnp.broadcast_to(mod[bsz][None, None, :], (bsz, 1, N_MOD * d))
        w_in_l = w_in[l].astype(BF16)
        w_s_l = w_s[l].astype(BF16)
        b_s_b = jnp.broadcast_to(b_s[l][:, :, None], (A_GROUPS, CHUNK, CHUNK))
        proj_args = (w_in_l, row(sgu_g[l]), row(sgu_b[l]), w_s_l, b_s_b, cos_t, sin_t)
        attn_args = (lambda_q[l], lambda_k[l], row(subln_g[l]))
        merge_w = (w_pa[l].astype(BF16), w_pb[l].astype(BF16), w_o[l].astype(BF16),
                   row(ln1_g[l]), row(ln1_b[l]), w_rt, b_r_b)
        moe_w = (w1[l].astype(BF16), w3[l].astype(BF16), w2[l].astype(BF16), row(ln2_g[l]), row(ln2_b[l]))

        ya, qt, k, vt, ga, gb = _inproj(x, mod_lat, *proj_args, use_rope=True)
        cya, cqt, ck, cvt, cga, cgb = _inproj(ctx, mod_ctx, *proj_args, use_rope=False)
        yb = _attention(qt, k, vt, ck, cvt, *attn_args, lam_init=lam_init)
        x1, h2, gates = _merge(x, ya, yb, ga, gb, mod_lat, *merge_w, alpha=alpha)
        x = _moe(h2, gates, x1, mod_lat, *moe_w, alpha=alpha)
        if not last:
            cyb = _attention(cqt, None, None, ck, cvt, *attn_args, lam_init=lam_init)
            c1, ch2, cgates = _merge(ctx, cya, cyb, cga, cgb, mod_ctx, *merge_w, alpha=alpha)
            flat = lambda t: t.reshape(1, bsz * n_ctx, t.shape[-1])
            ctx = _moe(flat(ch2), flat(cgates), flat(c1), mod_ctx[0:1], *moe_w, alpha=alpha)
            ctx = ctx.reshape(bsz, n_ctx, d)
    return x
```

```python
import functools
import math

import jax
import jax.numpy as jnp
from jax import lax
from jax.experimental import pallas as pl
from jax.experimental.pallas import tpu as pltpu

F32 = jnp.float32
BF16 = jnp.bfloat16

GRID_W = 64
CHUNK = 128
A_GROUPS = 8
A_WIDTH = 1024
HEADS = 8
QK_DIM = 64
V_DIM = 2 * QK_DIM
HEAD_COLS = HEADS * V_DIM
ROPE_BASE = 10000.0
N_EXPERTS = 16
N_GROUPS = 4
EXPERTS_PER_GROUP = N_EXPERTS // N_GROUPS
TOP_K = 2
N_MOD = 6
ROUTE_ROWS = 8
GATE_LANES = 128
LN_EPS = 1e-5
RMS_EPS = 1e-5
NEG_BIG = -1e30

VMEM_LIMIT_BYTES = 56 * 1024 * 1024

TOKEN_TILE = 256
ATTN_Q_TILE = 512
ATTN_KV_CHUNK = 512
SUM_ROWS = 16
MOE_TOKEN_TILE = 256
MOE_ROW_TILE = 512


def _params(semantics):
    return pltpu.CompilerParams(dimension_semantics=semantics, vmem_limit_bytes=VMEM_LIMIT_BYTES)


def _resident(shape):
    nd = len(shape)
    return pl.BlockSpec(shape, lambda *_: (0,) * nd, pipeline_mode=pl.Buffered(1))


def _ln_rows(x):
    mu = jnp.mean(x, axis=-1, keepdims=True)
    xc = x - mu
    var = jnp.mean(xc * xc, axis=-1, keepdims=True)
    return xc * lax.rsqrt(var + LN_EPS)


def _mod_kernel(cond_ref, w_ref, b_ref, o_ref):
    c = cond_ref[...]
    s = c * jax.nn.sigmoid(c)
    o_ref[...] = jnp.dot(s, w_ref[...], preferred_element_type=F32, precision=lax.Precision.HIGHEST) + b_ref[...]


def _modulation(cond, w_mod, b_mod):
    rows, d = cond.shape
    return pl.pallas_call(
        _mod_kernel,
        grid=(N_MOD,),
        in_specs=[
            pl.BlockSpec((rows, d), lambda j: (0, 0)),
            pl.BlockSpec((d, d), lambda j: (0, j)),
            pl.BlockSpec((1, d), lambda j: (0, j)),
        ],
        out_specs=pl.BlockSpec((rows, d), lambda j: (0, j)),
        out_shape=jax.ShapeDtypeStruct((rows, N_MOD * d), F32),
        compiler_params=_params(("arbitrary",)),
        name="modulation",
    )(cond, w_mod, b_mod.reshape(1, -1))


def _rope(x, cos, sin_signed):
    n = x.shape[-1]
    reps = n // cos.shape[-1]
    c = jnp.tile(cos, (1, reps))
    s = jnp.tile(sin_signed, (1, reps))
    lane = lax.broadcasted_iota(jnp.int32, x.shape, 1)
    first = (lane & 31) < 16
    partner = jnp.where(first, pltpu.roll(x, n - 16, 1), pltpu.roll(x, 16, 1))
    return x * c + partner * s


def _inproj_kernel(x_ref, mod_ref, w_ref, sg_ref, sb_ref, ws_ref, bs_ref, cos_ref, sin_ref,
                   ya_ref, qt_ref, k_ref, vt_ref, ga_ref, gb_ref, *, d_model, use_rope, q_scale):
    d = d_model
    tm = x_ref.shape[1]
    shift = mod_ref[0, :, 0:d]
    scale = mod_ref[0, :, d:2 * d]
    h = (_ln_rows(x_ref[0]) * (1.0 + scale) + shift).astype(BF16)

    def proj(c0, width):
        return jnp.dot(h, w_ref[:, c0:c0 + width], preferred_element_type=F32)

    c_u, c_v = 0, A_WIDTH
    c_q = 2 * A_WIDTH
    c_k = c_q + HEAD_COLS
    c_vb = c_k + HEAD_COLS
    c_ga = c_vb + HEAD_COLS
    c_gb = c_ga + d

    u = jax.nn.gelu(proj(c_u, A_WIDTH))
    v = jax.nn.gelu(proj(c_v, A_WIDTH))
    vn = (_ln_rows(v) * sg_ref[...] + sb_ref[...]).astype(BF16)
    gd = A_WIDTH // A_GROUPS
    for pair in range(tm // (2 * CHUNK)):
        r0 = pair * 2 * CHUNK
        r1 = r0 + CHUNK
        for g in range(A_GROUPS):
            cols = slice(g * gd, (g + 1) * gd)
            rhs = jnp.concatenate([vn[r0:r0 + CHUNK, cols], vn[r1:r1 + CHUNK, cols]], axis=1)
            y = jnp.dot(ws_ref[g], rhs, preferred_element_type=F32)
            bias = bs_ref[g]
            ya_ref[0, r0:r0 + CHUNK, cols] = (u[r0:r0 + CHUNK, cols] * (y[:, :gd] + bias)).astype(BF16)
            ya_ref[0, r1:r1 + CHUNK, cols] = (u[r1:r1 + CHUNK, cols] * (y[:, gd:] + bias)).astype(BF16)

    q = proj(c_q, HEAD_COLS)
    k = proj(c_k, HEAD_COLS)
    if use_rope:
        q = _rope(q, cos_ref[...], sin_ref[...])
        k = _rope(k, cos_ref[...], sin_ref[...])
    q = q * q_scale
    k_ref[0] = k.astype(BF16)
    vb = proj(c_vb, HEAD_COLS)
    for hd in range(HEADS):
        cols = slice(hd * V_DIM, (hd + 1) * V_DIM)
        qt_ref[0, hd] = q[:, cols].T.astype(BF16)
        vt_ref[0, hd] = vb[:, cols].T.astype(BF16)

    ga_ref[0] = jax.nn.sigmoid(proj(c_ga, d)).astype(BF16)
    gb_ref[0] = jax.nn.sigmoid(proj(c_gb, d)).astype(BF16)


def _inproj(x, mod, w_in, sgu_g, sgu_b, w_s, b_s_b, cos_t, sin_t, *, use_rope):
    bsz, n, d = x.shape
    tm = min(TOKEN_TILE, n)
    nt = n // tm
    in_cols = w_in.shape[1]
    q_scale = (QK_DIM ** -0.5) * math.log2(math.e)
    tok = lambda b, i: (b, i, 0)
    tr = lambda b, i: (b, 0, 0, i)
    rope_map = (lambda b, i: (i, 0)) if use_rope else (lambda b, i: (0, 0))
    return pl.pallas_call(
        functools.partial(_inproj_kernel, d_model=d, use_rope=use_rope, q_scale=q_scale),
        grid=(bsz, nt),
        in_specs=[
            pl.BlockSpec((1, tm, d), tok),
            pl.BlockSpec((1, 1, N_MOD * d), lambda b, i: (b, 0, 0)),
            _resident((d, in_cols)),
            _resident((1, A_WIDTH)),
            _resident((1, A_WIDTH)),
            _resident((A_GROUPS, CHUNK, CHUNK)),
            _resident((A_GROUPS, CHUNK, CHUNK)),
            pl.BlockSpec((tm, 2 * QK_DIM), rope_map),
            pl.BlockSpec((tm, 2 * QK_DIM), rope_map),
        ],
        out_specs=[
            pl.BlockSpec((1, tm, A_WIDTH), tok),
            pl.BlockSpec((1, HEADS, V_DIM, tm), tr),
            pl.BlockSpec((1, tm, HEAD_COLS), tok),
            pl.BlockSpec((1, HEADS, V_DIM, tm), tr),
            pl.BlockSpec((1, tm, d), tok),
            pl.BlockSpec((1, tm, d), tok),
        ],
        out_shape=[
            jax.ShapeDtypeStruct((bsz, n, A_WIDTH), BF16),
            jax.ShapeDtypeStruct((bsz, HEADS, V_DIM, n), BF16),
            jax.ShapeDtypeStruct((bsz, n, HEAD_COLS), BF16),
            jax.ShapeDtypeStruct((bsz, HEADS, V_DIM, n), BF16),
            jax.ShapeDtypeStruct((bsz, n, d), BF16),
            jax.ShapeDtypeStruct((bsz, n, d), BF16),
        ],
        compiler_params=_params(("parallel", "parallel")),
        name="inproj_rope" if use_rope else "inproj_ctx",
    )(x, mod, w_in, sgu_g, sgu_b, w_s, b_s_b, cos_t, sin_t)


def _attn_kernel(*refs, n_lat_chunks, tk, lam_init):
    if n_lat_chunks:
        (qt_ref, kl_ref, vtl_ref, kc_ref, vtc_ref, lq_ref, lk_ref, sg_ref,
         o_ref, rhs_ref, s_ref, cm_ref, m_ref, acc_ref) = refs
    else:
        (qt_ref, kc_ref, vtc_ref, lq_ref, lk_ref, sg_ref,
         o_ref, rhs_ref, s_ref, cm_ref, m_ref, acc_ref) = refs
    tq = qt_ref.shape[-1]
    nc = kc_ref.shape[1]

    qt = qt_ref[0, 0]
    row = lax.broadcasted_iota(jnp.int32, qt.shape, 0)
    zero = jnp.zeros_like(qt)
    rhs_ref[:, 0:tq] = jnp.where(row < QK_DIM, qt, zero)
    rhs_ref[:, tq:2 * tq] = jnp.where(row >= QK_DIM, qt, zero)
    m_ref[...] = jnp.full(m_ref.shape, NEG_BIG, F32)
    acc_ref[...] = jnp.zeros(acc_ref.shape, F32)

    def stage_a(kc, slot, rows):
        s = jnp.dot(kc, rhs_ref[...], preferred_element_type=F32)
        s_ref[slot, 0:rows, :] = s
        cm_ref[slot] = jnp.max(s, axis=0, keepdims=True)

    def stage_b(vtc, slot, rows):
        m_old = m_ref[...]
        m_new = jnp.maximum(m_old, cm_ref[slot])
        alpha = jnp.exp2(m_old - m_new)
        p = jnp.exp2(s_ref[slot, 0:rows, :] - m_new)
        vt_ones = jnp.concatenate([vtc, jnp.ones((SUM_ROWS, rows), BF16)], axis=0)
        acc_ref[...] = alpha * acc_ref[...] + jnp.dot(vt_ones, p.astype(BF16), preferred_element_type=F32)
        m_ref[...] = m_new

    if n_lat_chunks:
        def lat_k(c):
            return kl_ref[0, pl.ds(pl.multiple_of(c * tk, tk), tk), :]

        def lat_vt(c):
            return vtl_ref[0, 0, :, pl.ds(pl.multiple_of(c * tk, tk), tk)]

        stage_a(lat_k(0), 0, tk)

        def body(j, carry):
            c = 2 * j
            stage_a(lat_k(c + 1), 1, tk)
            stage_b(lat_vt(c), 0, tk)
            stage_a(lat_k(c + 2), 0, tk)
            stage_b(lat_vt(c + 1), 1, tk)
            return carry
        lax.fori_loop(0, n_lat_chunks // 2 - 1, body, 0)
        c = n_lat_chunks - 2
        stage_a(lat_k(c + 1), 1, tk)
        stage_b(lat_vt(c), 0, tk)
        stage_a(kc_ref[0], 0, nc)
        stage_b(lat_vt(c + 1), 1, tk)
        stage_b(vtc_ref[0, 0], 0, nc)
    else:
        stage_a(kc_ref[0], 0, nc)
        stage_b(vtc_ref[0, 0], 0, nc)

    lq = lq_ref[...]
    lk = lk_ref[...]
    lam = (jnp.exp(jnp.sum(lq[0:1] * lk[0:1], keepdims=True))
           - jnp.exp(jnp.sum(lq[1:2] * lk[1:2], keepdims=True)) + lam_init)
    acc = acc_ref[0:V_DIM, :]
    l = acc_ref[V_DIM:V_DIM + 1, :]
    o = acc[:, 0:tq] / l[:, 0:tq] - lam * (acc[:, tq:2 * tq] / l[:, tq:2 * tq])
    ms = jnp.mean(o * o, axis=0, keepdims=True)
    on = o * lax.rsqrt(ms + RMS_EPS)
    o_ref[0] = (on.T * sg_ref[...] * (1.0 - lam_init)).astype(BF16)


def _attention(qt, k_lat, vt_lat, k_ctx, vt_ctx, lq, lk, subln_g, *, lam_init):
    bsz, _, _, nq = qt.shape
    nc = k_ctx.shape[1]
    tq = min(ATTN_Q_TILE, nq)
    has_lat = k_lat is not None
    tk = ATTN_KV_CHUNK
    n_lat_chunks = (k_lat.shape[1] // tk) if has_lat else 0
    assert n_lat_chunks % 2 == 0
    s_rows = tk if has_lat else nc
    in_specs = [pl.BlockSpec((1, 1, V_DIM, tq), lambda b, h, i: (b, h, 0, i))]
    args = [qt]
    if has_lat:
        ns = k_lat.shape[1]
        in_specs += [pl.BlockSpec((1, ns, V_DIM), lambda b, h, i: (b, 0, h)),
                     pl.BlockSpec((1, 1, V_DIM, ns), lambda b, h, i: (b, h, 0, 0))]
        args += [k_lat, vt_lat]
    in_specs += [pl.BlockSpec((1, nc, V_DIM), lambda b, h, i: (b, 0, h)),
                 pl.BlockSpec((1, 1, V_DIM, nc), lambda b, h, i: (b, h, 0, 0)),
                 pl.BlockSpec((2, QK_DIM), lambda b, h, i: (0, 0)),
                 pl.BlockSpec((2, QK_DIM), lambda b, h, i: (0, 0)),
                 pl.BlockSpec((1, V_DIM), lambda b, h, i: (0, 0))]
    args += [k_ctx, vt_ctx, lq, lk, subln_g]
    return pl.pallas_call(
        functools.partial(_attn_kernel, n_lat_chunks=n_lat_chunks, tk=tk, lam_init=lam_init),
        grid=(bsz, HEADS, nq // tq),
        in_specs=in_specs,
        out_specs=pl.BlockSpec((1, tq, V_DIM), lambda b, h, i: (b, i, h)),
        out_shape=jax.ShapeDtypeStruct((bsz, nq, HEAD_COLS), BF16),
        scratch_shapes=[
            pltpu.VMEM((V_DIM, 2 * tq), BF16),
            pltpu.VMEM((2, s_rows, 2 * tq), F32),
            pltpu.VMEM((2, 1, 2 * tq), F32),
            pltpu.VMEM((1, 2 * tq), F32),
            pltpu.VMEM((V_DIM + SUM_ROWS, 2 * tq), F32),
        ],
        compiler_params=_params(("parallel", "parallel", "arbitrary")),
        name="diff_attn_latent" if has_lat else "diff_attn_ctx",
    )(*args)


def _route_rows(logits_t):
    mx = jnp.max(logits_t, axis=0, keepdims=True)
    ex = jnp.exp(logits_t - mx)
    probs = ex / jnp.sum(ex, axis=0, keepdims=True)
    p = [probs[e:e + 1, :] for e in range(N_EXPERTS)]
    scores = []
    for g in range(N_GROUPS):
        a, b, c, d = p[4 * g:4 * g + 4]
        hi1, lo1 = jnp.maximum(a, b), jnp.minimum(a, b)
        hi2, lo2 = jnp.maximum(c, d), jnp.minimum(c, d)
        top1 = jnp.maximum(hi1, hi2)
        top2 = jnp.maximum(jnp.minimum(hi1, hi2), jnp.maximum(lo1, lo2))
        scores.append(top1 + top2)
    best = jnp.zeros_like(scores[0], dtype=jnp.int32)
    best_score = scores[0]
    for g in range(1, N_GROUPS):
        better = scores[g] > best_score
        best = jnp.where(better, g, best)
        best_score = jnp.where(better, scores[g], best_score)
    sel = []
    for e in range(N_EXPERTS):
        g = e // EXPERTS_PER_GROUP
        rank = jnp.zeros_like(best)
        for j in range(g * EXPERTS_PER_GROUP, (g + 1) * EXPERTS_PER_GROUP):
            if j == e:
                continue
            ahead = (p[j] >= p[e]) if j < e else (p[j] > p[e])
            rank = rank + ahead.astype(jnp.int32)
        sel.append((best == g) & (rank < 2))
    kept = [jnp.where(sel[e], p[e], 0.0) for e in range(N_EXPERTS)]
    denom = kept[0]
    for e in range(1, N_EXPERTS):
        denom = denom + kept[e]
    return sel, [kp / denom for kp in kept]


def _merge_kernel(x_ref, ya_ref, yb_ref, ga_ref, gb_ref, mod_ref, wpa_ref, wpb_ref, wo_ref,
                  g1_ref, b1_ref, wr_ref, br_ref, x1_ref, h2_ref, route_ref, gates_ref, counts_ref,
                  *, d_model, alpha):
    d = d_model
    tm = x_ref.shape[1]
    gate1 = mod_ref[0, :, 2 * d:3 * d]
    shift2 = mod_ref[0, :, 3 * d:4 * d]
    scale2 = mod_ref[0, :, 4 * d:5 * d]
    a = jnp.dot(ya_ref[0], wpa_ref[...], preferred_element_type=F32)
    b = jnp.dot(yb_ref[0], wpb_ref[...], preferred_element_type=F32)
    merged = (ga_ref[0].astype(F32) * a + gb_ref[0].astype(F32) * b).astype(BF16)
    mix = jnp.dot(merged, wo_ref[...], preferred_element_type=F32)
    x1 = _ln_rows(alpha * x_ref[0] + gate1 * mix) * g1_ref[...] + b1_ref[...]
    x1_ref[0] = x1
    h2 = _ln_rows(x1) * (1.0 + scale2) + shift2
    h2_ref[0] = h2
    logits_t = lax.dot_general(wr_ref[...], h2, (((1,), (1,)), ((), ())),
                               preferred_element_type=F32, precision=lax.Precision.HIGHEST)
    logits_t = logits_t + jnp.tile(br_ref[...], (1, tm // br_ref.shape[1]))
    sel, gate_rows = _route_rows(logits_t)

    sel_t = jnp.concatenate([jnp.where(s, 1.0, 0.0) for s in sel], axis=0)
    earlier = (lax.broadcasted_iota(jnp.int32, (tm, tm), 0)
               < lax.broadcasted_iota(jnp.int32, (tm, tm), 1))
    rank_t = jnp.dot(sel_t.astype(BF16), jnp.where(earlier, 1.0, 0.0).astype(BF16),
                     preferred_element_type=F32)
    e_lo = jnp.full((1, tm), N_EXPERTS, jnp.int32)
    e_hi = jnp.full((1, tm), -1, jnp.int32)
    for e in range(N_EXPERTS):
        e_lo = jnp.where(sel[e], jnp.minimum(e_lo, e), e_lo)
        e_hi = jnp.where(sel[e], jnp.maximum(e_hi, e), e_hi)
    zero = jnp.zeros((1, tm), F32)
    r_lo, r_hi, g_lo, g_hi = zero, zero, zero, zero
    for e in range(N_EXPERTS):
        is_lo, is_hi = e_lo == e, e_hi == e
        r_lo = jnp.where(is_lo, rank_t[e:e + 1, :], r_lo)
        r_hi = jnp.where(is_hi, rank_t[e:e + 1, :], r_hi)
        g_lo = jnp.where(is_lo, gate_rows[e], g_lo)
        g_hi = jnp.where(is_hi, gate_rows[e], g_hi)
    route_ref[0] = jnp.concatenate(
        [e_lo, e_hi, r_lo.astype(jnp.int32), r_hi.astype(jnp.int32), jnp.zeros((4, tm), jnp.int32)], axis=0)
    gates_ref[0] = jnp.concatenate([g_lo, g_hi, jnp.zeros((GATE_LANES - 2, tm), F32)], axis=0).T
    counts = jnp.sum(sel_t, axis=1, keepdims=True)
    counts_ref[0, 0] = jnp.broadcast_to(counts, (N_EXPERTS, 128)).astype(jnp.int32)


def _merge(x, ya, yb, ga, gb, mod, w_pa, w_pb, w_o, g1, b1, w_rt, b_r_b, *, alpha):
    bsz, n, d = x.shape
    tm = min(TOKEN_TILE, n)
    nt = n // tm
    tok = lambda b, i: (b, i, 0)
    return pl.pallas_call(
        functools.partial(_merge_kernel, d_model=d, alpha=alpha),
        grid=(bsz, nt),
        in_specs=[
            pl.BlockSpec((1, tm, d), tok),
            pl.BlockSpec((1, tm, A_WIDTH), tok),
            pl.BlockSpec((1, tm, HEAD_COLS), tok),
            pl.BlockSpec((1, tm, d), tok),
            pl.BlockSpec((1, tm, d), tok),
            pl.BlockSpec((1, 1, N_MOD * d), lambda b, i: (b, 0, 0)),
            _resident((A_WIDTH, d)),
            _resident((HEAD_COLS, d)),
            _resident((d, d)),
            _resident((1, d)),
            _resident((1, d)),
            _resident((N_EXPERTS, d)),
            _resident((N_EXPERTS, 128)),
        ],
        out_specs=[
            pl.BlockSpec((1, tm, d), tok),
            pl.BlockSpec((1, tm, d), tok),
            pl.BlockSpec((1, ROUTE_ROWS, tm), lambda b, i: (b, 0, i)),
            pl.BlockSpec((1, tm, GATE_LANES), tok),
            pl.BlockSpec((1, 1, N_EXPERTS, 128), lambda b, i: (b, i, 0, 0)),
        ],
        out_shape=[
            jax.ShapeDtypeStruct((bsz, n, d), F32),
            jax.ShapeDtypeStruct((bsz, n, d), F32),
            jax.ShapeDtypeStruct((bsz, ROUTE_ROWS, n), jnp.int32),
            jax.ShapeDtypeStruct((bsz, n, GATE_LANES), F32),
            jax.ShapeDtypeStruct((bsz, nt, N_EXPERTS, 128), jnp.int32),
        ],
        compiler_params=_params(("parallel", "parallel")),
        name="merge_route",
    )(x, ya, yb, ga, gb, mod, w_pa, w_pb, w_o, g1, b1, w_rt, b_r_b)


def _dispatch_plan(route, counts, n_tiles_max):
    bsz, _, n = route.shape
    nt = counts.shape[1]
    tm = n // nt
    cnt = counts[..., 0].reshape(bsz * nt, N_EXPERTS)
    total = jnp.sum(cnt, axis=0)
    tiles_per_expert = (total + MOE_ROW_TILE - 1) // MOE_ROW_TILE
    tile_end = jnp.cumsum(tiles_per_expert)
    seg_start = (tile_end - tiles_per_expert) * MOE_ROW_TILE
    base = seg_start[None, :] + jnp.cumsum(cnt, axis=0) - cnt
    base_tok = jnp.repeat(base.reshape(bsz, nt, N_EXPERTS), tm, axis=1)
    pos_lo = jnp.take_along_axis(base_tok, route[:, 0, :, None], axis=-1)[..., 0] + route[:, 2]
    pos_hi = jnp.take_along_axis(base_tok, route[:, 1, :, None], axis=-1)[..., 0] + route[:, 3]
    tile_expert = jnp.searchsorted(tile_end, jnp.arange(n_tiles_max, dtype=jnp.int32), side="right")
    tile_expert = jnp.minimum(tile_expert, N_EXPERTS - 1).astype(jnp.int32)
    n_valid = tile_end[-1:].astype(jnp.int32)
    return pos_lo.astype(jnp.int32), pos_hi.astype(jnp.int32), tile_expert, n_valid


def _tile_positions(pos_lo, pos_hi, tile):
    bsz, n = pos_lo.shape
    both = jnp.concatenate([pos_lo.reshape(bsz, n // tile, tile), pos_hi.reshape(bsz, n // tile, tile)], axis=-1)
    return both.reshape(bsz * (n // tile), 1, 2 * tile)


def _dispatch_kernel(pos_ref, h_hbm, hs_in_hbm, hs_hbm, sem, *, tile):
    del hs_in_hbm
    b = pl.program_id(0)
    row0 = pl.program_id(1) * tile

    def row_copy(t, dst_row):
        return pltpu.make_async_copy(h_hbm.at[b, pl.ds(row0 + t, 1), :], hs_hbm.at[pl.ds(dst_row, 1), :], sem)

    def issue(t, carry):
        row_copy(t, pos_ref[0, 0, t]).start()
        row_copy(t, pos_ref[0, 0, tile + t]).start()
        return carry
    lax.fori_loop(0, tile, issue, 0, unroll=4)

    def drain(t, carry):
        row_copy(0, 0).wait()
        row_copy(0, 0).wait()
        return carry
    lax.fori_loop(0, tile, drain, 0, unroll=4)


def _dispatch(h2, pos_tiles, n_rows):
    bsz, n, d = h2.shape
    tile = pos_tiles.shape[-1] // 2
    nt = n // tile
    return pl.pallas_call(
        functools.partial(_dispatch_kernel, tile=tile),
        grid=(bsz, nt),
        in_specs=[
            pl.BlockSpec((1, 1, 2 * tile), lambda b, i: (b * nt + i, 0, 0), memory_space=pltpu.SMEM),
            pl.BlockSpec(memory_space=pl.ANY),
            pl.BlockSpec(memory_space=pl.ANY),
        ],
        out_specs=pl.BlockSpec(memory_space=pl.ANY),
        out_shape=jax.ShapeDtypeStruct((n_rows, d), F32),
        scratch_shapes=[pltpu.SemaphoreType.DMA(())],
        input_output_aliases={2: 0},
        compiler_params=_params(("arbitrary", "arbitrary")),
        name="moe_dispatch",
    )(pos_tiles, h2, jnp.zeros((n_rows, d), F32))


def _experts_kernel(te_ref, nv_ref, hs_ref, w1_ref, w3_ref, w2_ref, ys_ref):
    del te_ref
    j = pl.program_id(0)

    @pl.when(j < nv_ref[0])
    def _():
        h = hs_ref[...].astype(BF16)
        a = jnp.dot(h, w1_ref[0], preferred_element_type=F32)
        b = jnp.dot(h, w3_ref[0], preferred_element_type=F32)
        hid = (a * jax.nn.sigmoid(a) * b).astype(BF16)
        ys_ref[...] = jnp.dot(hid, w2_ref[0], preferred_element_type=F32)

    @pl.when(j >= nv_ref[0])
    def _():
        ys_ref[...] = jnp.zeros(ys_ref.shape, F32)


def _experts(hs, tile_expert, n_valid, w1, w3, w2):
    n_rows, d = hs.shape
    de = w1.shape[-1]
    n_tiles = n_rows // MOE_ROW_TILE
    w_in_map = lambda j, te, nv: (te[j], 0, 0)
    return pl.pallas_call(
        _experts_kernel,
        grid_spec=pltpu.PrefetchScalarGridSpec(
            num_scalar_prefetch=2,
            grid=(n_tiles,),
            in_specs=[
                pl.BlockSpec((MOE_ROW_TILE, d), lambda j, te, nv: (jnp.minimum(j, nv[0] - 1), 0)),
                pl.BlockSpec((1, d, de), w_in_map),
                pl.BlockSpec((1, d, de), w_in_map),
                pl.BlockSpec((1, de, d), w_in_map),
            ],
            out_specs=pl.BlockSpec((MOE_ROW_TILE, d), lambda j, te, nv: (j, 0)),
        ),
        out_shape=jax.ShapeDtypeStruct((n_rows, d), F32),
        compiler_params=_params(("arbitrary",)),
        name="moe_experts",
    )(tile_expert, n_valid, hs, w1, w3, w2)


def _combine_kernel(pos_ref, nxt_ref, x1_ref, gates_ref, mod_ref, g2_ref, b2_ref, ys_hbm,
                    o_ref, ybuf, sem, *, tile, d_model, alpha):
    d = d_model
    step = pl.program_id(0) * pl.num_programs(1) + pl.program_id(1)
    n_steps = pl.num_programs(0) * pl.num_programs(1)
    slot = step % 2

    def row_copy(src_row, slot_, dst_row):
        return pltpu.make_async_copy(ys_hbm.at[pl.ds(src_row, 1), :],
                                     ybuf.at[slot_, pl.ds(dst_row, 1), :], sem.at[slot_])

    def gather(p_ref, slot_):
        def issue(t, carry):
            row_copy(p_ref[0, 0, t], slot_, t).start()
            row_copy(p_ref[0, 0, tile + t], slot_, tile + t).start()
            return carry
        lax.fori_loop(0, tile, issue, 0, unroll=4)

    @pl.when(step == 0)
    def _():
        gather(pos_ref, 0)

    @pl.when(step + 1 < n_steps)
    def _():
        gather(nxt_ref, 1 - slot)

    def drain(t, carry):
        row_copy(0, slot, 0).wait()
        row_copy(0, slot, 0).wait()
        return carry
    lax.fori_loop(0, tile, drain, 0, unroll=4)

    gates = gates_ref[0]
    y = gates[:, 0:1] * ybuf[slot, 0:tile, :] + gates[:, 1:2] * ybuf[slot, tile:2 * tile, :]
    gate2 = mod_ref[0, :, 5 * d:6 * d]
    o_ref[0] = _ln_rows(alpha * x1_ref[0] + gate2 * y) * g2_ref[...] + b2_ref[...]


def _combine(ys, pos_tiles, x1, gates, mod, g2, b2, *, alpha):
    bsz, n, d = x1.shape
    tile = pos_tiles.shape[-1] // 2
    nt = n // tile
    n_steps = bsz * nt
    tok = lambda b, i: (b, i, 0)
    smem_tile = lambda index_map: pl.BlockSpec((1, 1, 2 * tile), index_map, memory_space=pltpu.SMEM)
    return pl.pallas_call(
        functools.partial(_combine_kernel, tile=tile, d_model=d, alpha=alpha),
        grid=(bsz, nt),
        in_specs=[
            smem_tile(lambda b, i: (b * nt + i, 0, 0)),
            smem_tile(lambda b, i: (jnp.minimum(b * nt + i + 1, n_steps - 1), 0, 0)),
            pl.BlockSpec((1, tile, d), tok),
            pl.BlockSpec((1, tile, GATE_LANES), tok),
            pl.BlockSpec((1, 1, N_MOD * d), lambda b, i: (b, 0, 0)),
            pl.BlockSpec((1, d), lambda b, i: (0, 0)),
            pl.BlockSpec((1, d), lambda b, i: (0, 0)),
            pl.BlockSpec(memory_space=pl.ANY),
        ],
        out_specs=pl.BlockSpec((1, tile, d), tok),
        out_shape=jax.ShapeDtypeStruct((bsz, n, d), F32),
        scratch_shapes=[pltpu.VMEM((2, 2 * tile, d), F32), pltpu.SemaphoreType.DMA((2,))],
        compiler_params=_params(("arbitrary", "arbitrary")),
        name="moe_combine",
    )(pos_tiles, pos_tiles, x1, gates, mod, g2, b2, ys)


def _moe(h2, route, gates, counts, x1, mod, w1, w3, w2, g2, b2, *, alpha):
    bsz, n, d = x1.shape
    n_rows = TOP_K * bsz * n + N_EXPERTS * MOE_ROW_TILE
    pos_lo, pos_hi, tile_expert, n_valid = _dispatch_plan(route, counts, n_rows // MOE_ROW_TILE)
    pos_tiles = _tile_positions(pos_lo, pos_hi, min(MOE_TOKEN_TILE, n))
    hs = _dispatch(h2, pos_tiles, n_rows)
    ys = _experts(hs, tile_expert, n_valid, w1, w3, w2)
    return _combine(ys, pos_tiles, x1, gates, mod, g2, b2, alpha=alpha)


def _rope_tables(n_tokens):
    rows = n_tokens // GRID_W
    row = jnp.repeat(jnp.arange(rows, dtype=F32), GRID_W)
    col = jnp.tile(jnp.arange(GRID_W, dtype=F32), rows)
    n_freq = QK_DIM // 4
    inv_freq = ROPE_BASE ** (-jnp.arange(n_freq, dtype=F32) / n_freq)
    ang_r = row[:, None] * inv_freq
    ang_c = col[:, None] * inv_freq
    cos64 = jnp.concatenate([jnp.cos(ang_r), jnp.cos(ang_r), jnp.cos(ang_c), jnp.cos(ang_c)], axis=-1)
    sin64 = jnp.concatenate([-jnp.sin(ang_r), jnp.sin(ang_r), -jnp.sin(ang_c), jnp.sin(ang_c)], axis=-1)
    return jnp.tile(cos64, (1, 2)), jnp.tile(sin64, (1, 2))


def kernel(x, c, ctx, c_ctx, w_mod, b_mod, w_in, sgu_g, sgu_b, w_s, b_s, lambda_q, lambda_k, subln_g,
           w_pa, w_pb, w_o, ln1_g, ln1_b, w_router, b_router, w1, w3, w2, ln2_g, ln2_b):
    bsz, n_lat, d = x.shape
    depth = w_mod.shape[0]
    alpha = (2.0 * depth) ** 0.25
    cos_t, sin_t = _rope_tables(n_lat)

    cond = jnp.zeros((8, d), F32).at[0:bsz].set(c).at[bsz].set(c_ctx)
    w_rt = w_router.T
    b_r_b = jnp.broadcast_to(b_router[:, None], (N_EXPERTS, 128))
    row = lambda v: v.reshape(1, -1)

    for l in range(depth):
        last = l == depth - 1
        lam_init = 0.8 - 0.6 * math.exp(-0.3 * l)
        mod = _modulation(cond, w_mod[l], b_mod[l])
        mod_lat = mod[0:bsz, None, :]
        mod_ctx = jnp.broadcast_to(mod[bsz][None, None, :], (bsz, 1, N_MOD * d))
        w_in_l = w_in[l].astype(BF16)
        w_s_l = w_s[l].astype(BF16)
        b_s_b = jnp.broadcast_to(b_s[l][:, :, None], (A_GROUPS, CHUNK, CHUNK))
        proj_args = (w_in_l, row(sgu_g[l]), row(sgu_b[l]), w_s_l, b_s_b, cos_t, sin_t)
        attn_args = (lambda_q[l], lambda_k[l], row(subln_g[l]))
        merge_w = (w_pa[l].astype(BF16), w_pb[l].astype(BF16), w_o[l].astype(BF16),
                   row(ln1_g[l]), row(ln1_b[l]), w_rt, b_r_b)
        moe_w = (w1[l].astype(BF16), w3[l].astype(BF16), w2[l].astype(BF16), row(ln2_g[l]), row(ln2_b[l]))

        ya, qt, k, vt, ga, gb = _inproj(x, mod_lat, *proj_args, use_rope=True)
        cya, cqt, ck, cvt, cga, cgb = _inproj(ctx, mod_ctx, *proj_args, use_rope=False)
        yb = _attention(qt, k, vt, ck, cvt, *attn_args, lam_init=lam_init)
        x1, h2, route, gates, counts = _merge(x, ya, yb, ga, gb, mod_lat, *merge_w, alpha=alpha)
        x = _moe(h2, route, gates, counts, x1, mod_lat, *moe_w, alpha=alpha)
        if not last:
            cyb = _attention(cqt, None, None, ck, cvt, *attn_args, lam_init=lam_init)
            c1, ch2, croute, cgates, ccounts = _merge(ctx, cya, cyb, cga, cgb, mod_ctx, *merge_w, alpha=alpha)
            ctx = _moe(ch2, croute, cgates, ccounts, c1, mod_ctx, *moe_w, alpha=alpha)
    return x
```

```python
import functools
import math

import jax
import jax.numpy as jnp
from jax import lax
from jax.experimental import pallas as pl
from jax.experimental.pallas import tpu as pltpu

F32 = jnp.float32
BF16 = jnp.bfloat16

GRID_W = 64
CHUNK = 128
A_GROUPS = 8
A_WIDTH = 1024
HEADS = 8
QK_DIM = 64
V_DIM = 2 * QK_DIM
HEAD_COLS = HEADS * V_DIM
ROPE_BASE = 10000.0
N_EXPERTS = 16
N_GROUPS = 4
EXPERTS_PER_GROUP = N_EXPERTS // N_GROUPS
TOP_K = 2
N_MOD = 6
ROUTE_ROWS = 8
GATE_LANES = 128
LN_EPS = 1e-5
RMS_EPS = 1e-5
NEG_BIG = -1e30

VMEM_LIMIT_BYTES = 56 * 1024 * 1024

TOKEN_TILE = 256
ATTN_Q_TILE = 512
ATTN_KV_CHUNK = 512
SUM_ROWS = 16
MOE_TOKEN_TILE = 256
MOE_ROW_TILE = 512


def _params(semantics):
    return pltpu.CompilerParams(dimension_semantics=semantics, vmem_limit_bytes=VMEM_LIMIT_BYTES)


def _resident(shape):
    nd = len(shape)
    return pl.BlockSpec(shape, lambda *_: (0,) * nd, pipeline_mode=pl.Buffered(1))


def _ln_rows(x):
    mu = jnp.mean(x, axis=-1, keepdims=True)
    xc = x - mu
    var = jnp.mean(xc * xc, axis=-1, keepdims=True)
    return xc * lax.rsqrt(var + LN_EPS)


def _mod_kernel(cond_ref, w_ref, b_ref, o_ref):
    c = cond_ref[...]
    s = c * jax.nn.sigmoid(c)
    o_ref[...] = jnp.dot(s, w_ref[...], preferred_element_type=F32, precision=lax.Precision.HIGHEST) + b_ref[...]


def _modulation(cond, w_mod, b_mod):
    rows, d = cond.shape
    return pl.pallas_call(
        _mod_kernel,
        grid=(N_MOD,),
        in_specs=[
            pl.BlockSpec((rows, d), lambda j: (0, 0)),
            pl.BlockSpec((d, d), lambda j: (0, j)),
            pl.BlockSpec((1, d), lambda j: (0, j)),
        ],
        out_specs=pl.BlockSpec((rows, d), lambda j: (0, j)),
        out_shape=jax.ShapeDtypeStruct((rows, N_MOD * d), F32),
        compiler_params=_params(("arbitrary",)),
        name="modulation",
    )(cond, w_mod, b_mod.reshape(1, -1))


def _rope(x, cos, sin_signed):
    n = x.shape[-1]
    reps = n // cos.shape[-1]
    c = jnp.tile(cos, (1, reps))
    s = jnp.tile(sin_signed, (1, reps))
    lane = lax.broadcasted_iota(jnp.int32, x.shape, 1)
    first = (lane & 31) < 16
    partner = jnp.where(first, pltpu.roll(x, n - 16, 1), pltpu.roll(x, 16, 1))
    return x * c + partner * s


def _inproj_kernel(x_ref, mod_ref, w_ref, sg_ref, sb_ref, ws_ref, bs_ref, cos_ref, sin_ref,
                   ya_ref, qt_ref, k_ref, vt_ref, ga_ref, gb_ref, *, d_model, use_rope, q_scale):
    d = d_model
    tm = x_ref.shape[1]
    shift = mod_ref[0, :, 0:d]
    scale = mod_ref[0, :, d:2 * d]
    h = (_ln_rows(x_ref[0]) * (1.0 + scale) + shift).astype(BF16)

    def proj(c0, width):
        return jnp.dot(h, w_ref[:, c0:c0 + width], preferred_element_type=F32)

    c_u, c_v = 0, A_WIDTH
    c_q = 2 * A_WIDTH
    c_k = c_q + HEAD_COLS
    c_vb = c_k + HEAD_COLS
    c_ga = c_vb + HEAD_COLS
    c_gb = c_ga + d

    u = jax.nn.gelu(proj(c_u, A_WIDTH))
    v = jax.nn.gelu(proj(c_v, A_WIDTH))
    vn = (_ln_rows(v) * sg_ref[...] + sb_ref[...]).astype(BF16)
    gd = A_WIDTH // A_GROUPS
    for pair in range(tm // (2 * CHUNK)):
        r0 = pair * 2 * CHUNK
        r1 = r0 + CHUNK
        for g in range(A_GROUPS):
            cols = slice(g * gd, (g + 1) * gd)
            rhs = jnp.concatenate([vn[r0:r0 + CHUNK, cols], vn[r1:r1 + CHUNK, cols]], axis=1)
            y = jnp.dot(ws_ref[g], rhs, preferred_element_type=F32)
            bias = bs_ref[g]
            ya_ref[0, r0:r0 + CHUNK, cols] = (u[r0:r0 + CHUNK, cols] * (y[:, :gd] + bias)).astype(BF16)
            ya_ref[0, r1:r1 + CHUNK, cols] = (u[r1:r1 + CHUNK, cols] * (y[:, gd:] + bias)).astype(BF16)

    q = proj(c_q, HEAD_COLS)
    k = proj(c_k, HEAD_COLS)
    if use_rope:
        q = _rope(q, cos_ref[...], sin_ref[...])
        k = _rope(k, cos_ref[...], sin_ref[...])
    q = q * q_scale
    k_ref[0] = k.astype(BF16)
    vb = proj(c_vb, HEAD_COLS)
    for hd in range(HEADS):
        cols = slice(hd * V_DIM, (hd + 1) * V_DIM)
        qt_ref[0, hd] = q[:, cols].T.astype(BF16)
        vt_ref[0, hd] = vb[:, cols].T.astype(BF16)

    ga_ref[0] = jax.nn.sigmoid(proj(c_ga, d)).astype(BF16)
    gb_ref[0] = jax.nn.sigmoid(proj(c_gb, d)).astype(BF16)


def _inproj(x, mod, w_in, sgu_g, sgu_b, w_s, b_s_b, cos_t, sin_t, *, use_rope):
    bsz, n, d = x.shape
    tm = min(TOKEN_TILE, n)
    nt = n // tm
    in_cols = w_in.shape[1]
    q_scale = (QK_DIM ** -0.5) * math.log2(math.e)
    tok = lambda b, i: (b, i, 0)
    tr = lambda b, i: (b, 0, 0, i)
    rope_map = (lambda b, i: (i, 0)) if use_rope else (lambda b, i: (0, 0))
    return pl.pallas_call(
        functools.partial(_inproj_kernel, d_model=d, use_rope=use_rope, q_scale=q_scale),
        grid=(bsz, nt),
        in_specs=[
            pl.BlockSpec((1, tm, d), tok),
            pl.BlockSpec((1, 1, N_MOD * d), lambda b, i: (b, 0, 0)),
            _resident((d, in_cols)),
            _resident((1, A_WIDTH)),
            _resident((1, A_WIDTH)),
            _resident((A_GROUPS, CHUNK, CHUNK)),
            _resident((A_GROUPS, CHUNK, CHUNK)),
            pl.BlockSpec((tm, 2 * QK_DIM), rope_map),
            pl.BlockSpec((tm, 2 * QK_DIM), rope_map),
        ],
        out_specs=[
            pl.BlockSpec((1, tm, A_WIDTH), tok),
            pl.BlockSpec((1, HEADS, V_DIM, tm), tr),
            pl.BlockSpec((1, tm, HEAD_COLS), tok),
            pl.BlockSpec((1, HEADS, V_DIM, tm), tr),
            pl.BlockSpec((1, tm, d), tok),
            pl.BlockSpec((1, tm, d), tok),
        ],
        out_shape=[
            jax.ShapeDtypeStruct((bsz, n, A_WIDTH), BF16),
            jax.ShapeDtypeStruct((bsz, HEADS, V_DIM, n), BF16),
            jax.ShapeDtypeStruct((bsz, n, HEAD_COLS), BF16),
            jax.ShapeDtypeStruct((bsz, HEADS, V_DIM, n), BF16),
            jax.ShapeDtypeStruct((bsz, n, d), BF16),
            jax.ShapeDtypeStruct((bsz, n, d), BF16),
        ],
        compiler_params=_params(("parallel", "parallel")),
        name="inproj_rope" if use_rope else "inproj_ctx",
    )(x, mod, w_in, sgu_g, sgu_b, w_s, b_s_b, cos_t, sin_t)


def _attn_kernel(*refs, n_lat_chunks, tk, lam_init):
    if n_lat_chunks:
        (qt_ref, kl_ref, vtl_ref, kc_ref, vtc_ref, lq_ref, lk_ref, sg_ref,
         o_ref, rhs_ref, s_ref, cm_ref, m_ref, acc_ref) = refs
    else:
        (qt_ref, kc_ref, vtc_ref, lq_ref, lk_ref, sg_ref,
         o_ref, rhs_ref, s_ref, cm_ref, m_ref, acc_ref) = refs
    tq = qt_ref.shape[-1]
    nc = kc_ref.shape[1]

    qt = qt_ref[0, 0]
    row = lax.broadcasted_iota(jnp.int32, qt.shape, 0)
    zero = jnp.zeros_like(qt)
    rhs_ref[:, 0:tq] = jnp.where(row < QK_DIM, qt, zero)
    rhs_ref[:, tq:2 * tq] = jnp.where(row >= QK_DIM, qt, zero)
    m_ref[...] = jnp.full(m_ref.shape, NEG_BIG, F32)
    acc_ref[...] = jnp.zeros(acc_ref.shape, F32)

    def stage_a(kc, slot, rows):
        s = jnp.dot(kc, rhs_ref[...], preferred_element_type=F32)
        s_ref[slot, 0:rows, :] = s
        cm_ref[slot] = jnp.max(s, axis=0, keepdims=True)

    def stage_b(vtc, slot, rows):
        m_old = m_ref[...]
        m_new = jnp.maximum(m_old, cm_ref[slot])
        alpha = jnp.exp2(m_old - m_new)
        p = jnp.exp2(s_ref[slot, 0:rows, :] - m_new)
        vt_ones = jnp.concatenate([vtc, jnp.ones((SUM_ROWS, rows), BF16)], axis=0)
        acc_ref[...] = alpha * acc_ref[...] + jnp.dot(vt_ones, p.astype(BF16), preferred_element_type=F32)
        m_ref[...] = m_new

    if n_lat_chunks:
        def lat_k(c):
            return kl_ref[0, pl.ds(pl.multiple_of(c * tk, tk), tk), :]

        def lat_vt(c):
            return vtl_ref[0, 0, :, pl.ds(pl.multiple_of(c * tk, tk), tk)]

        stage_a(lat_k(0), 0, tk)

        def body(j, carry):
            c = 2 * j
            stage_a(lat_k(c + 1), 1, tk)
            stage_b(lat_vt(c), 0, tk)
            stage_a(lat_k(c + 2), 0, tk)
            stage_b(lat_vt(c + 1), 1, tk)
            return carry
        lax.fori_loop(0, n_lat_chunks // 2 - 1, body, 0)
        c = n_lat_chunks - 2
        stage_a(lat_k(c + 1), 1, tk)
        stage_b(lat_vt(c), 0, tk)
        stage_a(kc_ref[0], 0, nc)
        stage_b(lat_vt(c + 1), 1, tk)
        stage_b(vtc_ref[0, 0], 0, nc)
    else:
        stage_a(kc_ref[0], 0, nc)
        stage_b(vtc_ref[0, 0], 0, nc)

    lq = lq_ref[...]
    lk = lk_ref[...]
    lam = (jnp.exp(jnp.sum(lq[0:1] * lk[0:1], keepdims=True))
           - jnp.exp(jnp.sum(lq[1:2] * lk[1:2], keepdims=True)) + lam_init)
    acc = acc_ref[0:V_DIM, :]
    l = acc_ref[V_DIM:V_DIM + 1, :]
    o = acc[:, 0:tq] / l[:, 0:tq] - lam * (acc[:, tq:2 * tq] / l[:, tq:2 * tq])
    ms = jnp.mean(o * o, axis=0, keepdims=True)
    on = o * lax.rsqrt(ms + RMS_EPS)
    o_ref[0] = (on.T * sg_ref[...] * (1.0 - lam_init)).astype(BF16)


def _attention(qt, k_lat, vt_lat, k_ctx, vt_ctx, lq, lk, subln_g, *, lam_init):
    bsz, _, _, nq = qt.shape
    nc = k_ctx.shape[1]
    tq = min(ATTN_Q_TILE, nq)
    has_lat = k_lat is not None
    tk = ATTN_KV_CHUNK
    n_lat_chunks = (k_lat.shape[1] // tk) if has_lat else 0
    assert n_lat_chunks % 2 == 0
    s_rows = tk if has_lat else nc
    in_specs = [pl.BlockSpec((1, 1, V_DIM, tq), lambda b, h, i: (b, h, 0, i))]
    args = [qt]
    if has_lat:
        ns = k_lat.shape[1]
        in_specs += [pl.BlockSpec((1, ns, V_DIM), lambda b, h, i: (b, 0, h)),
                     pl.BlockSpec((1, 1, V_DIM, ns), lambda b, h, i: (b, h, 0, 0))]
        args += [k_lat, vt_lat]
    in_specs += [pl.BlockSpec((1, nc, V_DIM), lambda b, h, i: (b, 0, h)),
                 pl.BlockSpec((1, 1, V_DIM, nc), lambda b, h, i: (b, h, 0, 0)),
                 pl.BlockSpec((2, QK_DIM), lambda b, h, i: (0, 0)),
                 pl.BlockSpec((2, QK_DIM), lambda b, h, i: (0, 0)),
                 pl.BlockSpec((1, V_DIM), lambda b, h, i: (0, 0))]
    args += [k_ctx, vt_ctx, lq, lk, subln_g]
    return pl.pallas_call(
        functools.partial(_attn_kernel, n_lat_chunks=n_lat_chunks, tk=tk, lam_init=lam_init),
        grid=(bsz, HEADS, nq // tq),
        in_specs=in_specs,
        out_specs=pl.BlockSpec((1, tq, V_DIM), lambda b, h, i: (b, i, h)),
        out_shape=jax.ShapeDtypeStruct((bsz, nq, HEAD_COLS), BF16),
        scratch_shapes=[
            pltpu.VMEM((V_DIM, 2 * tq), BF16),
            pltpu.VMEM((2, s_rows, 2 * tq), F32),
            pltpu.VMEM((2, 1, 2 * tq), F32),
            pltpu.VMEM((1, 2 * tq), F32),
            pltpu.VMEM((V_DIM + SUM_ROWS, 2 * tq), F32),
        ],
        compiler_params=_params(("parallel", "parallel", "arbitrary")),
        name="diff_attn_latent" if has_lat else "diff_attn_ctx",
    )(*args)


def _route_rows(logits_t):
    mx = jnp.max(logits_t, axis=0, keepdims=True)
    ex = jnp.exp(logits_t - mx)
    probs = ex / jnp.sum(ex, axis=0, keepdims=True)
    p = [probs[e:e + 1, :] for e in range(N_EXPERTS)]
    scores = []
    for g in range(N_GROUPS):
        a, b, c, d = p[4 * g:4 * g + 4]
        hi1, lo1 = jnp.maximum(a, b), jnp.minimum(a, b)
        hi2, lo2 = jnp.maximum(c, d), jnp.minimum(c, d)
        top1 = jnp.maximum(hi1, hi2)
        top2 = jnp.maximum(jnp.minimum(hi1, hi2), jnp.maximum(lo1, lo2))
        scores.append(top1 + top2)
    best = jnp.zeros_like(scores[0], dtype=jnp.int32)
    best_score = scores[0]
    for g in range(1, N_GROUPS):
        better = scores[g] > best_score
        best = jnp.where(better, g, best)
        best_score = jnp.where(better, scores[g], best_score)
    sel = []
    for e in range(N_EXPERTS):
        g = e // EXPERTS_PER_GROUP
        rank = jnp.zeros_like(best)
        for j in range(g * EXPERTS_PER_GROUP, (g + 1) * EXPERTS_PER_GROUP):
            if j == e:
                continue
            ahead = (p[j] >= p[e]) if j < e else (p[j] > p[e])
            rank = rank + ahead.astype(jnp.int32)
        sel.append((best == g) & (rank < 2))
    kept = [jnp.where(sel[e], p[e], 0.0) for e in range(N_EXPERTS)]
    denom = kept[0]
    for e in range(1, N_EXPERTS):
        denom = denom + kept[e]
    return sel, [kp / denom for kp in kept]


def _merge_kernel(x_ref, ya_ref, yb_ref, ga_ref, gb_ref, mod_ref, wpa_ref, wpb_ref, wo_ref,
                  g1_ref, b1_ref, wr_ref, br_ref, x1_ref, h2_ref, route_ref, gates_ref, counts_ref,
                  *, d_model, alpha):
    d = d_model
    tm = x_ref.shape[1]
    gate1 = mod_ref[0, :, 2 * d:3 * d]
    shift2 = mod_ref[0, :, 3 * d:4 * d]
    scale2 = mod_ref[0, :, 4 * d:5 * d]
    a = jnp.dot(ya_ref[0], wpa_ref[...], preferred_element_type=F32)
    b = jnp.dot(yb_ref[0], wpb_ref[...], preferred_element_type=F32)
    merged = (ga_ref[0].astype(F32) * a + gb_ref[0].astype(F32) * b).astype(BF16)
    mix = jnp.dot(merged, wo_ref[...], preferred_element_type=F32)
    x1 = _ln_rows(alpha * x_ref[0] + gate1 * mix) * g1_ref[...] + b1_ref[...]
    x1_ref[0] = x1
    h2 = _ln_rows(x1) * (1.0 + scale2) + shift2
    h2_ref[0] = h2
    logits_t = lax.dot_general(wr_ref[...], h2, (((1,), (1,)), ((), ())),
                               preferred_element_type=F32, precision=lax.Precision.HIGHEST)
    logits_t = logits_t + jnp.tile(br_ref[...], (1, tm // br_ref.shape[1]))
    sel, gate_rows = _route_rows(logits_t)

    sel_t = jnp.concatenate([jnp.where(s, 1.0, 0.0) for s in sel], axis=0)
    earlier = (lax.broadcasted_iota(jnp.int32, (tm, tm), 0)
               < lax.broadcasted_iota(jnp.int32, (tm, tm), 1))
    rank_t = jnp.dot(sel_t.astype(BF16), jnp.where(earlier, 1.0, 0.0).astype(BF16),
                     preferred_element_type=F32)
    e_lo = jnp.full((1, tm), N_EXPERTS, jnp.int32)
    e_hi = jnp.full((1, tm), -1, jnp.int32)
    for e in range(N_EXPERTS):
        e_lo = jnp.where(sel[e], jnp.minimum(e_lo, e), e_lo)
        e_hi = jnp.where(sel[e], jnp.maximum(e_hi, e), e_hi)
    zero = jnp.zeros((1, tm), F32)
    r_lo, r_hi, g_lo, g_hi = zero, zero, zero, zero
    for e in range(N_EXPERTS):
        is_lo, is_hi = e_lo == e, e_hi == e
        r_lo = jnp.where(is_lo, rank_t[e:e + 1, :], r_lo)
        r_hi = jnp.where(is_hi, rank_t[e:e + 1, :], r_hi)
        g_lo = jnp.where(is_lo, gate_rows[e], g_lo)
        g_hi = jnp.where(is_hi, gate_rows[e], g_hi)
    route_ref[0] = jnp.concatenate(
        [e_lo, e_hi, r_lo.astype(jnp.int32), r_hi.astype(jnp.int32), jnp.zeros((4, tm), jnp.int32)], axis=0)
    gates_ref[0] = jnp.concatenate([g_lo, g_hi, jnp.zeros((GATE_LANES - 2, tm), F32)], axis=0).T
    counts = jnp.sum(sel_t, axis=1, keepdims=True)
    counts_ref[0, 0] = jnp.broadcast_to(counts, (N_EXPERTS, 128)).astype(jnp.int32)


def _merge(x, ya, yb, ga, gb, mod, w_pa, w_pb, w_o, g1, b1, w_rt, b_r_b, *, alpha):
    bsz, n, d = x.shape
    tm = min(TOKEN_TILE, n)
    nt = n // tm
    tok = lambda b, i: (b, i, 0)
    return pl.pallas_call(
        functools.partial(_merge_kernel, d_model=d, alpha=alpha),
        grid=(bsz, nt),
        in_specs=[
            pl.BlockSpec((1, tm, d), tok),
            pl.BlockSpec((1, tm, A_WIDTH), tok),
            pl.BlockSpec((1, tm, HEAD_COLS), tok),
            pl.BlockSpec((1, tm, d), tok),
            pl.BlockSpec((1, tm, d), tok),
            pl.BlockSpec((1, 1, N_MOD * d), lambda b, i: (b, 0, 0)),
            _resident((A_WIDTH, d)),
            _resident((HEAD_COLS, d)),
            _resident((d, d)),
            _resident((1, d)),
            _resident((1, d)),
            _resident((N_EXPERTS, d)),
            _resident((N_EXPERTS, 128)),
        ],
        out_specs=[
            pl.BlockSpec((1, tm, d), tok),
            pl.BlockSpec((1, tm, d), tok),
            pl.BlockSpec((1, ROUTE_ROWS, tm), lambda b, i: (b, 0, i)),
            pl.BlockSpec((1, tm, GATE_LANES), tok),
            pl.BlockSpec((1, 1, N_EXPERTS, 128), lambda b, i: (b, i, 0, 0)),
        ],
        out_shape=[
            jax.ShapeDtypeStruct((bsz, n, d), F32),
            jax.ShapeDtypeStruct((bsz, n, d), F32),
            jax.ShapeDtypeStruct((bsz, ROUTE_ROWS, n), jnp.int32),
            jax.ShapeDtypeStruct((bsz, n, GATE_LANES), F32),
            jax.ShapeDtypeStruct((bsz, nt, N_EXPERTS, 128), jnp.int32),
        ],
        compiler_params=_params(("parallel", "parallel")),
        name="merge_route",
    )(x, ya, yb, ga, gb, mod, w_pa, w_pb, w_o, g1, b1, w_rt, b_r_b)


def _dispatch_plan(route, counts, n_tiles_max):
    bsz, _, n = route.shape
    nt = counts.shape[1]
    tm = n // nt
    n_src = bsz * nt
    cnt = counts[..., 0].reshape(n_src, N_EXPERTS)
    total = jnp.sum(cnt, axis=0)
    tiles_per_expert = (total + MOE_ROW_TILE - 1) // MOE_ROW_TILE
    experts = jnp.arange(N_EXPERTS, dtype=jnp.int32)
    tile_end = jnp.sum(jnp.where(experts[None, :] <= experts[:, None], tiles_per_expert[None, :], 0), axis=1)
    seg_start = (tile_end - tiles_per_expert) * MOE_ROW_TILE
    src = jnp.arange(n_src, dtype=jnp.int32)
    before = jnp.sum(jnp.where((src[None, :] < src[:, None])[:, :, None], cnt[None, :, :], 0), axis=1)
    base = (seg_start[None, :] + before).reshape(bsz, nt, 1, N_EXPERTS)
    e_lo, e_hi, r_lo, r_hi = (route[:, i].reshape(bsz, nt, tm) for i in range(4))
    pick = lambda e_sel: jnp.sum(jnp.where(e_sel[..., None] == experts, base, 0), axis=-1)
    pos_lo = (pick(e_lo) + r_lo).reshape(bsz, n)
    pos_hi = (pick(e_hi) + r_hi).reshape(bsz, n)
    tile_ids = jnp.arange(n_tiles_max, dtype=jnp.int32)
    tile_expert = jnp.sum((tile_end[None, :] <= tile_ids[:, None]).astype(jnp.int32), axis=1)
    tile_expert = jnp.minimum(tile_expert, N_EXPERTS - 1)
    return pos_lo, pos_hi, tile_expert, tile_end[-1:]


def _tile_positions(pos_lo, pos_hi, tile):
    bsz, n = pos_lo.shape
    both = jnp.concatenate([pos_lo.reshape(bsz, n // tile, tile), pos_hi.reshape(bsz, n // tile, tile)], axis=-1)
    return both.reshape(bsz * (n // tile), 1, 2 * tile)


def _dispatch_kernel(pos_ref, h_ref, hs_in_hbm, hs_hbm, sem, *, tile):
    del hs_in_hbm

    def row_copy(t, dst_row):
        return pltpu.make_async_copy(h_ref.at[0, pl.ds(t, 1), :], hs_hbm.at[pl.ds(dst_row, 1), :], sem)

    def issue(t, carry):
        row_copy(t, pos_ref[0, 0, t]).start()
        row_copy(t, pos_ref[0, 0, tile + t]).start()
        return carry
    lax.fori_loop(0, tile, issue, 0, unroll=4)

    def drain(t, carry):
        row_copy(0, 0).wait()
        row_copy(0, 0).wait()
        return carry
    lax.fori_loop(0, tile, drain, 0, unroll=4)


def _dispatch(h2, pos_tiles, hs_init):
    bsz, n, d = h2.shape
    n_rows = hs_init.shape[0]
    tile = pos_tiles.shape[-1] // 2
    nt = n // tile
    return pl.pallas_call(
        functools.partial(_dispatch_kernel, tile=tile),
        grid=(bsz, nt),
        in_specs=[
            pl.BlockSpec((1, 1, 2 * tile), lambda b, i: (b * nt + i, 0, 0), memory_space=pltpu.SMEM),
            pl.BlockSpec((1, tile, d), lambda b, i: (b, i, 0)),
            pl.BlockSpec(memory_space=pl.ANY),
        ],
        out_specs=pl.BlockSpec(memory_space=pl.ANY),
        out_shape=jax.ShapeDtypeStruct((n_rows, d), F32),
        scratch_shapes=[pltpu.SemaphoreType.DMA(())],
        input_output_aliases={2: 0},
        compiler_params=_params(("arbitrary", "arbitrary")),
        name="moe_dispatch",
    )(pos_tiles, h2, hs_init)


def _experts_kernel(te_ref, nv_ref, hs_ref, w1_ref, w3_ref, w2_ref, ys_ref, w1b_ref, w3b_ref, w2b_ref):
    j = pl.program_id(0)

    @pl.when((j == 0) | (te_ref[j] != te_ref[jnp.maximum(j - 1, 0)]))
    def _():
        w1b_ref[...] = w1_ref[0, 0].astype(BF16)
        w3b_ref[...] = w3_ref[0, 0].astype(BF16)
        w2b_ref[...] = w2_ref[0, 0].astype(BF16)

    @pl.when(j < nv_ref[0])
    def _():
        h = hs_ref[...].astype(BF16)
        a = jnp.dot(h, w1b_ref[...], preferred_element_type=F32)
        b = jnp.dot(h, w3b_ref[...], preferred_element_type=F32)
        hid = (a * jax.nn.sigmoid(a) * b).astype(BF16)
        ys_ref[...] = jnp.dot(hid, w2b_ref[...], preferred_element_type=F32)

    @pl.when(j >= nv_ref[0])
    def _():
        ys_ref[...] = jnp.zeros(ys_ref.shape, F32)


def _experts(hs, tile_expert, n_valid, w1, w3, w2, *, layer):
    n_rows, d = hs.shape
    de = w1.shape[-1]
    n_tiles = n_rows // MOE_ROW_TILE
    w_in_map = lambda j, te, nv: (layer, te[j], 0, 0)
    return pl.pallas_call(
        _experts_kernel,
        grid_spec=pltpu.PrefetchScalarGridSpec(
            num_scalar_prefetch=2,
            grid=(n_tiles,),
            in_specs=[
                pl.BlockSpec((MOE_ROW_TILE, d), lambda j, te, nv: (jnp.minimum(j, nv[0] - 1), 0)),
                pl.BlockSpec((1, 1, d, de), w_in_map),
                pl.BlockSpec((1, 1, d, de), w_in_map),
                pl.BlockSpec((1, 1, de, d), w_in_map),
            ],
            out_specs=pl.BlockSpec((MOE_ROW_TILE, d), lambda j, te, nv: (j, 0)),
            scratch_shapes=[pltpu.VMEM((d, de), BF16), pltpu.VMEM((d, de), BF16), pltpu.VMEM((de, d), BF16)],
        ),
        out_shape=jax.ShapeDtypeStruct((n_rows, d), F32),
        compiler_params=_params(("arbitrary",)),
        name="moe_experts",
    )(tile_expert, n_valid, hs, w1, w3, w2)


def _combine_kernel(pos_ref, nxt_ref, x1_ref, gates_ref, mod_ref, g2_ref, b2_ref, ys_hbm,
                    o_ref, ybuf, sem, *, tile, d_model, alpha):
    d = d_model
    step = pl.program_id(0) * pl.num_programs(1) + pl.program_id(1)
    n_steps = pl.num_programs(0) * pl.num_programs(1)
    slot = step % 2

    def row_copy(src_row, slot_, dst_row):
        return pltpu.make_async_copy(ys_hbm.at[pl.ds(src_row, 1), :],
                                     ybuf.at[slot_, pl.ds(dst_row, 1), :], sem.at[slot_])

    def gather(p_ref, slot_):
        def issue(t, carry):
            row_copy(p_ref[0, 0, t], slot_, t).start()
            row_copy(p_ref[0, 0, tile + t], slot_, tile + t).start()
            return carry
        lax.fori_loop(0, tile, issue, 0, unroll=4)

    @pl.when(step == 0)
    def _():
        gather(pos_ref, 0)

    @pl.when(step + 1 < n_steps)
    def _():
        gather(nxt_ref, 1 - slot)

    def drain(t, carry):
        row_copy(0, slot, 0).wait()
        row_copy(0, slot, 0).wait()
        return carry
    lax.fori_loop(0, tile, drain, 0, unroll=4)

    gates = gates_ref[0]
    y = gates[:, 0:1] * ybuf[slot, 0:tile, :] + gates[:, 1:2] * ybuf[slot, tile:2 * tile, :]
    gate2 = mod_ref[0, :, 5 * d:6 * d]
    o_ref[0] = _ln_rows(alpha * x1_ref[0] + gate2 * y) * g2_ref[...] + b2_ref[...]


def _combine(ys, pos_tiles, x1, gates, mod, g2, b2, *, alpha):
    bsz, n, d = x1.shape
    tile = pos_tiles.shape[-1] // 2
    nt = n // tile
    n_steps = bsz * nt
    tok = lambda b, i: (b, i, 0)
    smem_tile = lambda index_map: pl.BlockSpec((1, 1, 2 * tile), index_map, memory_space=pltpu.SMEM)
    return pl.pallas_call(
        functools.partial(_combine_kernel, tile=tile, d_model=d, alpha=alpha),
        grid=(bsz, nt),
        in_specs=[
            smem_tile(lambda b, i: (b * nt + i, 0, 0)),
            smem_tile(lambda b, i: (jnp.minimum(b * nt + i + 1, n_steps - 1), 0, 0)),
            pl.BlockSpec((1, tile, d), tok),
            pl.BlockSpec((1, tile, GATE_LANES), tok),
            pl.BlockSpec((1, 1, N_MOD * d), lambda b, i: (b, 0, 0)),
            pl.BlockSpec((1, d), lambda b, i: (0, 0)),
            pl.BlockSpec((1, d), lambda b, i: (0, 0)),
            pl.BlockSpec(memory_space=pl.ANY),
        ],
        out_specs=pl.BlockSpec((1, tile, d), tok),
        out_shape=jax.ShapeDtypeStruct((bsz, n, d), F32),
        scratch_shapes=[pltpu.VMEM((2, 2 * tile, d), F32), pltpu.SemaphoreType.DMA((2,))],
        compiler_params=_params(("arbitrary", "arbitrary")),
        name="moe_combine",
    )(pos_tiles, pos_tiles, x1, gates, mod, g2, b2, ys)


def _moe(h2, route, gates, counts, x1, mod, w1, w3, w2, g2, b2, *, alpha, layer):
    bsz, n, d = x1.shape
    n_rows = TOP_K * bsz * n + N_EXPERTS * MOE_ROW_TILE
    pos_lo, pos_hi, tile_expert, n_valid = _dispatch_plan(route, counts, n_rows // MOE_ROW_TILE)
    pos_tiles = _tile_positions(pos_lo, pos_hi, min(MOE_TOKEN_TILE, n))
    hs = _dispatch(h2, pos_tiles, jnp.zeros((n_rows, d), F32))
    ys = _experts(hs, tile_expert, n_valid, w1, w3, w2, layer=layer)
    return _combine(ys, pos_tiles, x1, gates, mod, g2, b2, alpha=alpha)


def _rope_tables(n_tokens):
    rows = n_tokens // GRID_W
    row = jnp.repeat(jnp.arange(rows, dtype=F32), GRID_W)
    col = jnp.tile(jnp.arange(GRID_W, dtype=F32), rows)
    n_freq = QK_DIM // 4
    inv_freq = ROPE_BASE ** (-jnp.arange(n_freq, dtype=F32) / n_freq)
    ang_r = row[:, None] * inv_freq
    ang_c = col[:, None] * inv_freq
    cos64 = jnp.concatenate([jnp.cos(ang_r), jnp.cos(ang_r), jnp.cos(ang_c), jnp.cos(ang_c)], axis=-1)
    sin64 = jnp.concatenate([-jnp.sin(ang_r), jnp.sin(ang_r), -jnp.sin(ang_c), jnp.sin(ang_c)], axis=-1)
    return jnp.tile(cos64, (1, 2)), jnp.tile(sin64, (1, 2))


def kernel(x, c, ctx, c_ctx, w_mod, b_mod, w_in, sgu_g, sgu_b, w_s, b_s, lambda_q, lambda_k, subln_g,
           w_pa, w_pb, w_o, ln1_g, ln1_b, w_router, b_router, w1, w3, w2, ln2_g, ln2_b):
    bsz, n_lat, d = x.shape
    depth = w_mod.shape[0]
    alpha = (2.0 * depth) ** 0.25
    cos_t, sin_t = _rope_tables(n_lat)

    cond = jnp.zeros((8, d), F32).at[0:bsz].set(c).at[bsz].set(c_ctx)
    w_rt = w_router.T
    b_r_b = jnp.broadcast_to(b_router[:, None], (N_EXPERTS, 128))
    row = lambda v: v.reshape(1, -1)

    for l in range(depth):
        last = l == depth - 1
        lam_init = 0.8 - 0.6 * math.exp(-0.3 * l)
        mod = _modulation(cond, w_mod[l], b_mod[l])
        mod_lat = mod[0:bsz, None, :]
        mod_ctx = jnp.broadcast_to(mod[bsz][None, None, :], (bsz, 1, N_MOD * d))
        w_in_l = w_in[l].astype(BF16)
        w_s_l = w_s[l].astype(BF16)
        b_s_b = jnp.broadcast_to(b_s[l][:, :, None], (A_GROUPS, CHUNK, CHUNK))
        proj_args = (w_in_l, row(sgu_g[l]), row(sgu_b[l]), w_s_l, b_s_b, cos_t, sin_t)
        attn_args = (lambda_q[l], lambda_k[l], row(subln_g[l]))
        merge_w = (w_pa[l].astype(BF16), w_pb[l].astype(BF16), w_o[l].astype(BF16),
                   row(ln1_g[l]), row(ln1_b[l]), w_rt, b_r_b)
        moe_w = (w1, w3, w2, row(ln2_g[l]), row(ln2_b[l]))

        ya, qt, k, vt, ga, gb = _inproj(x, mod_lat, *proj_args, use_rope=True)
        cya, cqt, ck, cvt, cga, cgb = _inproj(ctx, mod_ctx, *proj_args, use_rope=False)
        yb = _attention(qt, k, vt, ck, cvt, *attn_args, lam_init=lam_init)
        x1, h2, route, gates, counts = _merge(x, ya, yb, ga, gb, mod_lat, *merge_w, alpha=alpha)
        x = _moe(h2, route, gates, counts, x1, mod_lat, *moe_w, alpha=alpha, layer=l)
        if not last:
            cyb = _attention(cqt, None, None, ck, cvt, *attn_args, lam_init=lam_init)
            c1, ch2, croute, cgates, ccounts = _merge(ctx, cya, cyb, cga, cgb, mod_ctx, *merge_w, alpha=alpha)
            ctx = _moe(ch2, croute, cgates, ccounts, c1, mod_ctx, *moe_w, alpha=alpha, layer=l)
    return x
```

```python
import functools
import math

import jax
import jax.numpy as jnp
from jax import lax
from jax.experimental import pallas as pl
from jax.experimental.pallas import tpu as pltpu

F32 = jnp.float32
BF16 = jnp.bfloat16

GRID_W = 64
CHUNK = 128
A_GROUPS = 8
A_WIDTH = 1024
HEADS = 8
QK_DIM = 64
V_DIM = 2 * QK_DIM
HEAD_COLS = HEADS * V_DIM
ROPE_BASE = 10000.0
N_EXPERTS = 16
N_GROUPS = 4
EXPERTS_PER_GROUP = N_EXPERTS // N_GROUPS
TOP_K = 2
N_MOD = 6
ROUTE_ROWS = 8
GATE_LANES = 128
LN_EPS = 1e-5
RMS_EPS = 1e-5
NEG_BIG = -1e30

VMEM_LIMIT_BYTES = 56 * 1024 * 1024

TOKEN_TILE = 256
ATTN_Q_TILE = 512
ATTN_KV_CHUNK = 512
ATTN_UNROLL = 4
SUM_ROWS = 16
MOE_TOKEN_TILE = 256
MOE_ROW_TILE = 512


def _params(semantics):
    return pltpu.CompilerParams(dimension_semantics=semantics, vmem_limit_bytes=VMEM_LIMIT_BYTES)


def _resident(shape):
    nd = len(shape)
    return pl.BlockSpec(shape, lambda *_: (0,) * nd, pipeline_mode=pl.Buffered(1))


def _split_bf16(x):
    bits = lax.bitcast_convert_type(x, jnp.uint32) & jnp.uint32(0xFFFF0000)
    hi = lax.bitcast_convert_type(bits, F32)
    return hi.astype(BF16), (x - hi).astype(BF16)


def _ln_rows(x):
    mu = jnp.mean(x, axis=-1, keepdims=True)
    xc = x - mu
    var = jnp.mean(xc * xc, axis=-1, keepdims=True)
    return xc * lax.rsqrt(var + LN_EPS)


def _mod_kernel(cond_ref, w_ref, b_ref, o_ref):
    c = cond_ref[...]
    s = c * jax.nn.sigmoid(c)
    o_ref[...] = jnp.dot(s, w_ref[...], preferred_element_type=F32, precision=lax.Precision.HIGHEST) + b_ref[...]


def _modulation(cond, w_mod, b_mod):
    rows, d = cond.shape
    return pl.pallas_call(
        _mod_kernel,
        grid=(N_MOD,),
        in_specs=[
            pl.BlockSpec((rows, d), lambda j: (0, 0)),
            pl.BlockSpec((d, d), lambda j: (0, j)),
            pl.BlockSpec((1, d), lambda j: (0, j)),
        ],
        out_specs=pl.BlockSpec((rows, d), lambda j: (0, j)),
        out_shape=jax.ShapeDtypeStruct((rows, N_MOD * d), F32),
        compiler_params=_params(("arbitrary",)),
        name="modulation",
    )(cond, w_mod, b_mod.reshape(1, -1))


def _rope(x, cos, sin_signed):
    n = x.shape[-1]
    reps = n // cos.shape[-1]
    c = jnp.tile(cos, (1, reps))
    s = jnp.tile(sin_signed, (1, reps))
    lane = lax.broadcasted_iota(jnp.int32, x.shape, 1)
    first = (lane & 31) < 16
    partner = jnp.where(first, pltpu.roll(x, n - 16, 1), pltpu.roll(x, 16, 1))
    return x * c + partner * s


def _inproj_kernel(x_ref, mod_ref, w_ref, sg_ref, sb_ref, ws_ref, bs_ref, cos_ref, sin_ref,
                   ya_ref, qt_ref, k_ref, vt_ref, ga_ref, gb_ref, *, d_model, use_rope, q_scale):
    d = d_model
    tm = x_ref.shape[1]
    shift = mod_ref[0, :, 0:d]
    scale = mod_ref[0, :, d:2 * d]
    h = (_ln_rows(x_ref[0]) * (1.0 + scale) + shift).astype(BF16)

    def proj(c0, width):
        return jnp.dot(h, w_ref[:, c0:c0 + width], preferred_element_type=F32)

    c_u, c_v = 0, A_WIDTH
    c_q = 2 * A_WIDTH
    c_k = c_q + HEAD_COLS
    c_vb = c_k + HEAD_COLS
    c_ga = c_vb + HEAD_COLS
    c_gb = c_ga + d

    u = jax.nn.gelu(proj(c_u, A_WIDTH))
    v = jax.nn.gelu(proj(c_v, A_WIDTH))
    vn = (_ln_rows(v) * sg_ref[...] + sb_ref[...]).astype(BF16)
    gd = A_WIDTH // A_GROUPS
    for pair in range(tm // (2 * CHUNK)):
        r0 = pair * 2 * CHUNK
        r1 = r0 + CHUNK
        for g in range(A_GROUPS):
            cols = slice(g * gd, (g + 1) * gd)
            rhs = jnp.concatenate([vn[r0:r0 + CHUNK, cols], vn[r1:r1 + CHUNK, cols]], axis=1)
            y = jnp.dot(ws_ref[g], rhs, preferred_element_type=F32)
            bias = bs_ref[g]
            ya_ref[0, r0:r0 + CHUNK, cols] = (u[r0:r0 + CHUNK, cols] * (y[:, :gd] + bias)).astype(BF16)
            ya_ref[0, r1:r1 + CHUNK, cols] = (u[r1:r1 + CHUNK, cols] * (y[:, gd:] + bias)).astype(BF16)

    q = proj(c_q, HEAD_COLS)
    k = proj(c_k, HEAD_COLS)
    if use_rope:
        q = _rope(q, cos_ref[...], sin_ref[...])
        k = _rope(k, cos_ref[...], sin_ref[...])
    q = q * q_scale
    k_ref[0] = k.astype(BF16)
    vb = proj(c_vb, HEAD_COLS)
    for hd in range(HEADS):
        cols = slice(hd * V_DIM, (hd + 1) * V_DIM)
        qt_ref[0, hd] = q[:, cols].T.astype(BF16)
        vt_ref[0, hd] = vb[:, cols].T.astype(BF16)

    ga_ref[0] = jax.nn.sigmoid(proj(c_ga, d)).astype(BF16)
    gb_ref[0] = jax.nn.sigmoid(proj(c_gb, d)).astype(BF16)


def _inproj(x, mod, w_in, sgu_g, sgu_b, w_s, b_s_b, cos_t, sin_t, *, use_rope):
    bsz, n, d = x.shape
    tm = min(TOKEN_TILE, n)
    nt = n // tm
    in_cols = w_in.shape[1]
    q_scale = (QK_DIM ** -0.5) * math.log2(math.e)
    tok = lambda b, i: (b, i, 0)
    tr = lambda b, i: (b, 0, 0, i)
    rope_map = (lambda b, i: (i, 0)) if use_rope else (lambda b, i: (0, 0))
    return pl.pallas_call(
        functools.partial(_inproj_kernel, d_model=d, use_rope=use_rope, q_scale=q_scale),
        grid=(bsz, nt),
        in_specs=[
            pl.BlockSpec((1, tm, d), tok),
            pl.BlockSpec((1, 1, N_MOD * d), lambda b, i: (b, 0, 0)),
            _resident((d, in_cols)),
            _resident((1, A_WIDTH)),
            _resident((1, A_WIDTH)),
            _resident((A_GROUPS, CHUNK, CHUNK)),
            _resident((A_GROUPS, CHUNK, CHUNK)),
            pl.BlockSpec((tm, 2 * QK_DIM), rope_map),
            pl.BlockSpec((tm, 2 * QK_DIM), rope_map),
        ],
        out_specs=[
            pl.BlockSpec((1, tm, A_WIDTH), tok),
            pl.BlockSpec((1, HEADS, V_DIM, tm), tr),
            pl.BlockSpec((1, tm, HEAD_COLS), tok),
            pl.BlockSpec((1, HEADS, V_DIM, tm), tr),
            pl.BlockSpec((1, tm, d), tok),
            pl.BlockSpec((1, tm, d), tok),
        ],
        out_shape=[
            jax.ShapeDtypeStruct((bsz, n, A_WIDTH), BF16),
            jax.ShapeDtypeStruct((bsz, HEADS, V_DIM, n), BF16),
            jax.ShapeDtypeStruct((bsz, n, HEAD_COLS), BF16),
            jax.ShapeDtypeStruct((bsz, HEADS, V_DIM, n), BF16),
            jax.ShapeDtypeStruct((bsz, n, d), BF16),
            jax.ShapeDtypeStruct((bsz, n, d), BF16),
        ],
        compiler_params=_params(("parallel", "parallel")),
        name="inproj_rope" if use_rope else "inproj_ctx",
    )(x, mod, w_in, sgu_g, sgu_b, w_s, b_s_b, cos_t, sin_t)


def _attn_kernel(*refs, n_lat_chunks, tk, lam_init):
    if n_lat_chunks:
        (qt_ref, kl_ref, vtl_ref, kc_ref, vtc_ref, lq_ref, lk_ref, sg_ref,
         o_ref, rhs_ref, s_ref, cm_ref, m_ref, acc_ref) = refs
    else:
        (qt_ref, kc_ref, vtc_ref, lq_ref, lk_ref, sg_ref,
         o_ref, rhs_ref, s_ref, cm_ref, m_ref, acc_ref) = refs
    tq = qt_ref.shape[-1]
    nc = kc_ref.shape[1]

    qt = qt_ref[0, 0]
    row = lax.broadcasted_iota(jnp.int32, qt.shape, 0)
    zero = jnp.zeros_like(qt)
    rhs_ref[:, 0:tq] = jnp.where(row < QK_DIM, qt, zero)
    rhs_ref[:, tq:2 * tq] = jnp.where(row >= QK_DIM, qt, zero)
    m_ref[...] = jnp.full(m_ref.shape, NEG_BIG, F32)
    acc_ref[...] = jnp.zeros(acc_ref.shape, F32)

    def stage_a(kc, slot, rows):
        s = jnp.dot(kc, rhs_ref[...], preferred_element_type=F32)
        s_ref[slot, 0:rows, :] = s
        cm_ref[slot] = jnp.max(s, axis=0, keepdims=True)

    def stage_b(vtc, slot, rows):
        m_old = m_ref[...]
        m_new = jnp.maximum(m_old, cm_ref[slot])
        alpha = jnp.exp2(m_old - m_new)
        p = jnp.exp2(s_ref[slot, 0:rows, :] - m_new)
        vt_ones = jnp.concatenate([vtc, jnp.ones((SUM_ROWS, rows), BF16)], axis=0)
        acc_ref[...] = alpha * acc_ref[...] + jnp.dot(vt_ones, p.astype(BF16), preferred_element_type=F32)
        m_ref[...] = m_new

    if n_lat_chunks:
        def lat_k(c):
            return kl_ref[0, pl.ds(pl.multiple_of(c * tk, tk), tk), :]

        def lat_vt(c):
            return vtl_ref[0, 0, :, pl.ds(pl.multiple_of(c * tk, tk), tk)]

        stage_a(lat_k(0), 0, tk)

        def pipelined(c0, n):
            for i in range(n):
                c = c0 + i
                if isinstance(c, int) and c + 1 == n_lat_chunks:
                    stage_a(kc_ref[0], (i + 1) % 2, nc)
                else:
                    stage_a(lat_k(c + 1), (i + 1) % 2, tk)
                stage_b(lat_vt(c), i % 2, tk)

        def body(j, carry):
            pipelined(ATTN_UNROLL * j, ATTN_UNROLL)
            return carry
        n_loop = n_lat_chunks // ATTN_UNROLL - 1
        lax.fori_loop(0, n_loop, body, 0)
        pipelined(n_loop * ATTN_UNROLL, n_lat_chunks - n_loop * ATTN_UNROLL)
        stage_b(vtc_ref[0, 0], n_lat_chunks % 2, nc)
    else:
        stage_a(kc_ref[0], 0, nc)
        stage_b(vtc_ref[0, 0], 0, nc)

    lq = lq_ref[...]
    lk = lk_ref[...]
    lam = (jnp.exp(jnp.sum(lq[0:1] * lk[0:1], keepdims=True))
           - jnp.exp(jnp.sum(lq[1:2] * lk[1:2], keepdims=True)) + lam_init)
    acc = acc_ref[0:V_DIM, :]
    l = acc_ref[V_DIM:V_DIM + 1, :]
    o = acc[:, 0:tq] / l[:, 0:tq] - lam * (acc[:, tq:2 * tq] / l[:, tq:2 * tq])
    ms = jnp.mean(o * o, axis=0, keepdims=True)
    on = o * lax.rsqrt(ms + RMS_EPS)
    o_ref[0] = (on.T * sg_ref[...] * (1.0 - lam_init)).astype(BF16)


def _attention(qt, k_lat, vt_lat, k_ctx, vt_ctx, lq, lk, subln_g, *, lam_init):
    bsz, _, _, nq = qt.shape
    nc = k_ctx.shape[1]
    tq = min(ATTN_Q_TILE, nq)
    has_lat = k_lat is not None
    tk = ATTN_KV_CHUNK
    n_lat_chunks = (k_lat.shape[1] // tk) if has_lat else 0
    assert n_lat_chunks % ATTN_UNROLL == 0 and ATTN_UNROLL % 2 == 0
    s_rows = tk if has_lat else nc
    in_specs = [pl.BlockSpec((1, 1, V_DIM, tq), lambda b, h, i: (b, h, 0, i))]
    args = [qt]
    if has_lat:
        ns = k_lat.shape[1]
        in_specs += [pl.BlockSpec((1, ns, V_DIM), lambda b, h, i: (b, 0, h)),
                     pl.BlockSpec((1, 1, V_DIM, ns), lambda b, h, i: (b, h, 0, 0))]
        args += [k_lat, vt_lat]
    in_specs += [pl.BlockSpec((1, nc, V_DIM), lambda b, h, i: (b, 0, h)),
                 pl.BlockSpec((1, 1, V_DIM, nc), lambda b, h, i: (b, h, 0, 0)),
                 pl.BlockSpec((2, QK_DIM), lambda b, h, i: (0, 0)),
                 pl.BlockSpec((2, QK_DIM), lambda b, h, i: (0, 0)),
                 pl.BlockSpec((1, V_DIM), lambda b, h, i: (0, 0))]
    args += [k_ctx, vt_ctx, lq, lk, subln_g]
    return pl.pallas_call(
        functools.partial(_attn_kernel, n_lat_chunks=n_lat_chunks, tk=tk, lam_init=lam_init),
        grid=(bsz, HEADS, nq // tq),
        in_specs=in_specs,
        out_specs=pl.BlockSpec((1, tq, V_DIM), lambda b, h, i: (b, i, h)),
        out_shape=jax.ShapeDtypeStruct((bsz, nq, HEAD_COLS), BF16),
        scratch_shapes=[
            pltpu.VMEM((V_DIM, 2 * tq), BF16),
            pltpu.VMEM((2, s_rows, 2 * tq), F32),
            pltpu.VMEM((2, 1, 2 * tq), F32),
            pltpu.VMEM((1, 2 * tq), F32),
            pltpu.VMEM((V_DIM + SUM_ROWS, 2 * tq), F32),
        ],
        compiler_params=_params(("parallel", "parallel", "arbitrary")),
        name="diff_attn_latent" if has_lat else "diff_attn_ctx",
    )(*args)


def _route_rows(logits_t):
    mx = jnp.max(logits_t, axis=0, keepdims=True)
    ex = jnp.exp(logits_t - mx)
    probs = ex / jnp.sum(ex, axis=0, keepdims=True)
    p = [probs[e:e + 1, :] for e in range(N_EXPERTS)]
    scores = []
    for g in range(N_GROUPS):
        a, b, c, d = p[4 * g:4 * g + 4]
        hi1, lo1 = jnp.maximum(a, b), jnp.minimum(a, b)
        hi2, lo2 = jnp.maximum(c, d), jnp.minimum(c, d)
        top1 = jnp.maximum(hi1, hi2)
        top2 = jnp.maximum(jnp.minimum(hi1, hi2), jnp.maximum(lo1, lo2))
        scores.append(top1 + top2)
    best = jnp.zeros_like(scores[0], dtype=jnp.int32)
    best_score = scores[0]
    for g in range(1, N_GROUPS):
        better = scores[g] > best_score
        best = jnp.where(better, g, best)
        best_score = jnp.where(better, scores[g], best_score)
    sel = []
    for e in range(N_EXPERTS):
        g = e // EXPERTS_PER_GROUP
        rank = jnp.zeros_like(best)
        for j in range(g * EXPERTS_PER_GROUP, (g + 1) * EXPERTS_PER_GROUP):
            if j == e:
                continue
            ahead = (p[j] >= p[e]) if j < e else (p[j] > p[e])
            rank = rank + ahead.astype(jnp.int32)
        sel.append((best == g) & (rank < 2))
    kept = [jnp.where(sel[e], p[e], 0.0) for e in range(N_EXPERTS)]
    denom = kept[0]
    for e in range(1, N_EXPERTS):
        denom = denom + kept[e]
    return sel, [kp / denom for kp in kept]


def _merge_kernel(x_ref, ya_ref, yb_ref, ga_ref, gb_ref, mod_ref, wpa_ref, wpb_ref, wo_ref,
                  g1_ref, b1_ref, wrh_ref, wrl_ref, br_ref, x1_ref, h2_ref, route_ref, gates_ref, counts_ref,
                  *, d_model, alpha):
    d = d_model
    tm = x_ref.shape[1]
    gate1 = mod_ref[0, :, 2 * d:3 * d]
    shift2 = mod_ref[0, :, 3 * d:4 * d]
    scale2 = mod_ref[0, :, 4 * d:5 * d]
    a = jnp.dot(ya_ref[0], wpa_ref[...], preferred_element_type=F32)
    b = jnp.dot(yb_ref[0], wpb_ref[...], preferred_element_type=F32)
    merged = (ga_ref[0].astype(F32) * a + gb_ref[0].astype(F32) * b).astype(BF16)
    mix = jnp.dot(merged, wo_ref[...], preferred_element_type=F32)
    x1 = _ln_rows(alpha * x_ref[0] + gate1 * mix) * g1_ref[...] + b1_ref[...]
    x1_ref[0] = x1
    h2 = _ln_rows(x1) * (1.0 + scale2) + shift2
    h2_ref[0] = h2
    h2_hi, h2_lo = _split_bf16(h2)
    logits = (jnp.dot(h2_hi, wrh_ref[...], preferred_element_type=F32)
              + jnp.dot(h2_lo, wrh_ref[...], preferred_element_type=F32)
              + jnp.dot(h2_hi, wrl_ref[...], preferred_element_type=F32))
    logits_t = logits.T[0:N_EXPERTS, :] + jnp.tile(br_ref[...], (1, tm // br_ref.shape[1]))
    sel, gate_rows = _route_rows(logits_t)

    sel_t = jnp.concatenate([jnp.where(s, 1.0, 0.0) for s in sel], axis=0)
    earlier = (lax.broadcasted_iota(jnp.int32, (tm, tm), 0)
               < lax.broadcasted_iota(jnp.int32, (tm, tm), 1))
    rank_t = jnp.dot(sel_t.astype(BF16), jnp.where(earlier, 1.0, 0.0).astype(BF16),
                     preferred_element_type=F32)
    e_lo = jnp.full((1, tm), N_EXPERTS, jnp.int32)
    e_hi = jnp.full((1, tm), -1, jnp.int32)
    for e in range(N_EXPERTS):
        e_lo = jnp.where(sel[e], jnp.minimum(e_lo, e), e_lo)
        e_hi = jnp.where(sel[e], jnp.maximum(e_hi, e), e_hi)
    zero = jnp.zeros((1, tm), F32)
    r_lo, r_hi, g_lo, g_hi = zero, zero, zero, zero
    for e in range(N_EXPERTS):
        is_lo, is_hi = e_lo == e, e_hi == e
        r_lo = jnp.where(is_lo, rank_t[e:e + 1, :], r_lo)
        r_hi = jnp.where(is_hi, rank_t[e:e + 1, :], r_hi)
        g_lo = jnp.where(is_lo, gate_rows[e], g_lo)
        g_hi = jnp.where(is_hi, gate_rows[e], g_hi)
    route_ref[0] = jnp.concatenate(
        [e_lo, e_hi, r_lo.astype(jnp.int32), r_hi.astype(jnp.int32), jnp.zeros((4, tm), jnp.int32)], axis=0)
    gates_ref[0] = jnp.concatenate([g_lo, g_hi, jnp.zeros((GATE_LANES - 2, tm), F32)], axis=0).T
    counts = jnp.sum(sel_t, axis=1, keepdims=True)
    counts_ref[0, 0] = jnp.broadcast_to(counts, (N_EXPERTS, 128)).astype(jnp.int32)


def _merge(x, ya, yb, ga, gb, mod, w_pa, w_pb, w_o, g1, b1, w_r_hi, w_r_lo, b_r_b, *, alpha):
    bsz, n, d = x.shape
    tm = min(TOKEN_TILE, n)
    nt = n // tm
    tok = lambda b, i: (b, i, 0)
    return pl.pallas_call(
        functools.partial(_merge_kernel, d_model=d, alpha=alpha),
        grid=(bsz, nt),
        in_specs=[
            pl.BlockSpec((1, tm, d), tok),
            pl.BlockSpec((1, tm, A_WIDTH), tok),
            pl.BlockSpec((1, tm, HEAD_COLS), tok),
            pl.BlockSpec((1, tm, d), tok),
            pl.BlockSpec((1, tm, d), tok),
            pl.BlockSpec((1, 1, N_MOD * d), lambda b, i: (b, 0, 0)),
            _resident((A_WIDTH, d)),
            _resident((HEAD_COLS, d)),
            _resident((d, d)),
            _resident((1, d)),
            _resident((1, d)),
            _resident((d, GATE_LANES)),
            _resident((d, GATE_LANES)),
            _resident((N_EXPERTS, 128)),
        ],
        out_specs=[
            pl.BlockSpec((1, tm, d), tok),
            pl.BlockSpec((1, tm, d), tok),
            pl.BlockSpec((1, ROUTE_ROWS, tm), lambda b, i: (b, 0, i)),
            pl.BlockSpec((1, tm, GATE_LANES), tok),
            pl.BlockSpec((1, 1, N_EXPERTS, 128), lambda b, i: (b, i, 0, 0)),
        ],
        out_shape=[
            jax.ShapeDtypeStruct((bsz, n, d), F32),
            jax.ShapeDtypeStruct((bsz, n, d), F32),
            jax.ShapeDtypeStruct((bsz, ROUTE_ROWS, n), jnp.int32),
            jax.ShapeDtypeStruct((bsz, n, GATE_LANES), F32),
            jax.ShapeDtypeStruct((bsz, nt, N_EXPERTS, 128), jnp.int32),
        ],
        compiler_params=_params(("parallel", "parallel")),
        name="merge_route",
    )(x, ya, yb, ga, gb, mod, w_pa, w_pb, w_o, g1, b1, w_r_hi, w_r_lo, b_r_b)


def _dispatch_plan(route, counts, n_tiles_max):
    bsz, _, n = route.shape
    nt = counts.shape[1]
    tm = n // nt
    n_src = bsz * nt
    cnt = counts[..., 0].reshape(n_src, N_EXPERTS)
    total = jnp.sum(cnt, axis=0)
    tiles_per_expert = (total + MOE_ROW_TILE - 1) // MOE_ROW_TILE
    experts = jnp.arange(N_EXPERTS, dtype=jnp.int32)
    tile_end = jnp.sum(jnp.where(experts[None, :] <= experts[:, None], tiles_per_expert[None, :], 0), axis=1)
    seg_start = (tile_end - tiles_per_expert) * MOE_ROW_TILE
    src = jnp.arange(n_src, dtype=jnp.int32)
    before = jnp.sum(jnp.where((src[None, :] < src[:, None])[:, :, None], cnt[None, :, :], 0), axis=1)
    base = (seg_start[None, :] + before).reshape(bsz, nt, 1, N_EXPERTS)
    e_lo, e_hi, r_lo, r_hi = (route[:, i].reshape(bsz, nt, tm) for i in range(4))
    pick = lambda e_sel: jnp.sum(jnp.where(e_sel[..., None] == experts, base, 0), axis=-1)
    pos_lo = (pick(e_lo) + r_lo).reshape(bsz, n)
    pos_hi = (pick(e_hi) + r_hi).reshape(bsz, n)
    tile_ids = jnp.arange(n_tiles_max, dtype=jnp.int32)
    tile_expert = jnp.sum((tile_end[None, :] <= tile_ids[:, None]).astype(jnp.int32), axis=1)
    tile_expert = jnp.minimum(tile_expert, N_EXPERTS - 1)
    return pos_lo, pos_hi, tile_expert, tile_end[-1:]


def _tile_positions(pos_lo, pos_hi, tile):
    bsz, n = pos_lo.shape
    both = jnp.concatenate([pos_lo.reshape(bsz, n // tile, tile), pos_hi.reshape(bsz, n // tile, tile)], axis=-1)
    return both.reshape(bsz * (n // tile), 1, 2 * tile)


def _dispatch_kernel(pos_ref, h_ref, hs_in_hbm, hs_hbm, sem, *, tile):
    del hs_in_hbm

    def row_copy(t, dst_row):
        return pltpu.make_async_copy(h_ref.at[0, pl.ds(t, 1), :], hs_hbm.at[pl.ds(dst_row, 1), :], sem)

    def issue(t, carry):
        row_copy(t, pos_ref[0, 0, t]).start()
        row_copy(t, pos_ref[0, 0, tile + t]).start()
        return carry
    lax.fori_loop(0, tile, issue, 0, unroll=4)

    def drain(t, carry):
        row_copy(0, 0).wait()
        row_copy(0, 0).wait()
        return carry
    lax.fori_loop(0, tile, drain, 0, unroll=4)


def _dispatch(h2, pos_tiles, hs_init):
    bsz, n, d = h2.shape
    n_rows = hs_init.shape[0]
    tile = pos_tiles.shape[-1] // 2
    nt = n // tile
    return pl.pallas_call(
        functools.partial(_dispatch_kernel, tile=tile),
        grid=(bsz, nt),
        in_specs=[
            pl.BlockSpec((1, 1, 2 * tile), lambda b, i: (b * nt + i, 0, 0), memory_space=pltpu.SMEM),
            pl.BlockSpec((1, tile, d), lambda b, i: (b, i, 0)),
            pl.BlockSpec(memory_space=pl.ANY),
        ],
        out_specs=pl.BlockSpec(memory_space=pl.ANY),
        out_shape=jax.ShapeDtypeStruct((n_rows, d), F32),
        scratch_shapes=[pltpu.SemaphoreType.DMA(())],
        input_output_aliases={2: 0},
        compiler_params=_params(("arbitrary", "arbitrary")),
        name="moe_dispatch",
    )(pos_tiles, h2, hs_init)


def _experts_kernel(te_ref, nv_ref, hs_ref, w1_ref, w3_ref, w2_ref, ys_ref, w1b_ref, w3b_ref, w2b_ref):
    j = pl.program_id(0)

    @pl.when((j == 0) | (te_ref[j] != te_ref[jnp.maximum(j - 1, 0)]))
    def _():
        w1b_ref[...] = w1_ref[0, 0].astype(BF16)
        w3b_ref[...] = w3_ref[0, 0].astype(BF16)
        w2b_ref[...] = w2_ref[0, 0].astype(BF16)

    @pl.when(j < nv_ref[0])
    def _():
        h = hs_ref[...].astype(BF16)
        a = jnp.dot(h, w1b_ref[...], preferred_element_type=F32)
        b = jnp.dot(h, w3b_ref[...], preferred_element_type=F32)
        hid = (a * jax.nn.sigmoid(a) * b).astype(BF16)
        ys_ref[...] = jnp.dot(hid, w2b_ref[...], preferred_element_type=F32)

    @pl.when(j >= nv_ref[0])
    def _():
        ys_ref[...] = jnp.zeros(ys_ref.shape, F32)


def _experts(hs, tile_expert, n_valid, w1, w3, w2, *, layer):
    n_rows, d = hs.shape
    de = w1.shape[-1]
    n_tiles = n_rows // MOE_ROW_TILE
    w_in_map = lambda j, te, nv: (layer, te[j], 0, 0)
    return pl.pallas_call(
        _experts_kernel,
        grid_spec=pltpu.PrefetchScalarGridSpec(
            num_scalar_prefetch=2,
            grid=(n_tiles,),
            in_specs=[
                pl.BlockSpec((MOE_ROW_TILE, d), lambda j, te, nv: (jnp.minimum(j, nv[0] - 1), 0)),
                pl.BlockSpec((1, 1, d, de), w_in_map),
                pl.BlockSpec((1, 1, d, de), w_in_map),
                pl.BlockSpec((1, 1, de, d), w_in_map),
            ],
            out_specs=pl.BlockSpec((MOE_ROW_TILE, d), lambda j, te, nv: (j, 0)),
            scratch_shapes=[pltpu.VMEM((d, de), BF16), pltpu.VMEM((d, de), BF16), pltpu.VMEM((de, d), BF16)],
        ),
        out_shape=jax.ShapeDtypeStruct((n_rows, d), F32),
        compiler_params=_params(("arbitrary",)),
        name="moe_experts",
    )(tile_expert, n_valid, hs, w1, w3, w2)


def _combine_kernel(pos_ref, nxt_ref, x1_ref, gates_ref, mod_ref, g2_ref, b2_ref, ys_hbm,
                    o_ref, ybuf, sem, *, tile, d_model, alpha):
    d = d_model
    step = pl.program_id(0) * pl.num_programs(1) + pl.program_id(1)
    n_steps = pl.num_programs(0) * pl.num_programs(1)
    slot = step % 2

    def row_copy(src_row, slot_, dst_row):
        return pltpu.make_async_copy(ys_hbm.at[pl.ds(src_row, 1), :],
                                     ybuf.at[slot_, pl.ds(dst_row, 1), :], sem.at[slot_])

    def gather(p_ref, slot_):
        def issue(t, carry):
            row_copy(p_ref[0, 0, t], slot_, t).start()
            row_copy(p_ref[0, 0, tile + t], slot_, tile + t).start()
            return carry
        lax.fori_loop(0, tile, issue, 0, unroll=4)

    @pl.when(step == 0)
    def _():
        gather(pos_ref, 0)

    @pl.when(step + 1 < n_steps)
    def _():
        gather(nxt_ref, 1 - slot)

    def drain(t, carry):
        row_copy(0, slot, 0).wait()
        row_copy(0, slot, 0).wait()
        return carry
    lax.fori_loop(0, tile, drain, 0, unroll=4)

    gates = gates_ref[0]
    y = gates[:, 0:1] * ybuf[slot, 0:tile, :] + gates[:, 1:2] * ybuf[slot, tile:2 * tile, :]
    gate2 = mod_ref[0, :, 5 * d:6 * d]
    o_ref[0] = _ln_rows(alpha * x1_ref[0] + gate2 * y) * g2_ref[...] + b2_ref[...]


def _combine(ys, pos_tiles, x1, gates, mod, g2, b2, *, alpha):
    bsz, n, d = x1.shape
    tile = pos_tiles.shape[-1] // 2
    nt = n // tile
    n_steps = bsz * nt
    tok = lambda b, i: (b, i, 0)
    smem_tile = lambda index_map: pl.BlockSpec((1, 1, 2 * tile), index_map, memory_space=pltpu.SMEM)
    return pl.pallas_call(
        functools.partial(_combine_kernel, tile=tile, d_model=d, alpha=alpha),
        grid=(bsz, nt),
        in_specs=[
            smem_tile(lambda b, i: (b * nt + i, 0, 0)),
            smem_tile(lambda b, i: (jnp.minimum(b * nt + i + 1, n_steps - 1), 0, 0)),
            pl.BlockSpec((1, tile, d), tok),
            pl.BlockSpec((1, tile, GATE_LANES), tok),
            pl.BlockSpec((1, 1, N_MOD * d), lambda b, i: (b, 0, 0)),
            pl.BlockSpec((1, d), lambda b, i: (0, 0)),
            pl.BlockSpec((1, d), lambda b, i: (0, 0)),
            pl.BlockSpec(memory_space=pl.ANY),
        ],
        out_specs=pl.BlockSpec((1, tile, d), tok),
        out_shape=jax.ShapeDtypeStruct((bsz, n, d), F32),
        scratch_shapes=[pltpu.VMEM((2, 2 * tile, d), F32), pltpu.SemaphoreType.DMA((2,))],
        compiler_params=_params(("arbitrary", "arbitrary")),
        name="moe_combine",
    )(pos_tiles, pos_tiles, x1, gates, mod, g2, b2, ys)


def _moe(h2, route, gates, counts, x1, mod, w1, w3, w2, g2, b2, *, alpha, layer):
    bsz, n, d = x1.shape
    n_rows = TOP_K * bsz * n + N_EXPERTS * MOE_ROW_TILE
    pos_lo, pos_hi, tile_expert, n_valid = _dispatch_plan(route, counts, n_rows // MOE_ROW_TILE)
    pos_tiles = _tile_positions(pos_lo, pos_hi, min(MOE_TOKEN_TILE, n))
    hs = _dispatch(h2, pos_tiles, jnp.zeros((n_rows, d), F32))
    ys = _experts(hs, tile_expert, n_valid, w1, w3, w2, layer=layer)
    return _combine(ys, pos_tiles, x1, gates, mod, g2, b2, alpha=alpha)


def _rope_tables(n_tokens):
    rows = n_tokens // GRID_W
    row = jnp.repeat(jnp.arange(rows, dtype=F32), GRID_W)
    col = jnp.tile(jnp.arange(GRID_W, dtype=F32), rows)
    n_freq = QK_DIM // 4
    inv_freq = ROPE_BASE ** (-jnp.arange(n_freq, dtype=F32) / n_freq)
    ang_r = row[:, None] * inv_freq
    ang_c = col[:, None] * inv_freq
    cos64 = jnp.concatenate([jnp.cos(ang_r), jnp.cos(ang_r), jnp.cos(ang_c), jnp.cos(ang_c)], axis=-1)
    sin64 = jnp.concatenate([-jnp.sin(ang_r), jnp.sin(ang_r), -jnp.sin(ang_c), jnp.sin(ang_c)], axis=-1)
    return jnp.tile(cos64, (1, 2)), jnp.tile(sin64, (1, 2))


def kernel(x, c, ctx, c_ctx, w_mod, b_mod, w_in, sgu_g, sgu_b, w_s, b_s, lambda_q, lambda_k, subln_g,
           w_pa, w_pb, w_o, ln1_g, ln1_b, w_router, b_router, w1, w3, w2, ln2_g, ln2_b):
    bsz, n_lat, d = x.shape
    depth = w_mod.shape[0]
    alpha = (2.0 * depth) ** 0.25
    cos_t, sin_t = _rope_tables(n_lat)

    cond = jnp.zeros((8, d), F32).at[0:bsz].set(c).at[bsz].set(c_ctx)
    w_r_pad = jnp.pad(w_router, ((0, 0), (0, GATE_LANES - N_EXPERTS)))
    w_r_hi, w_r_lo = _split_bf16(w_r_pad)
    b_r_b = jnp.broadcast_to(b_router[:, None], (N_EXPERTS, 128))
    row = lambda v: v.reshape(1, -1)

    for l in range(depth):
        last = l == depth - 1
        lam_init = 0.8 - 0.6 * math.exp(-0.3 * l)
        mod = _modulation(cond, w_mod[l], b_mod[l])
        mod_lat = mod[0:bsz, None, :]
        mod_ctx = jnp.broadcast_to(mod[bsz][None, None, :], (bsz, 1, N_MOD * d))
        w_in_l = w_in[l].astype(BF16)
        w_s_l = w_s[l].astype(BF16)
        b_s_b = jnp.broadcast_to(b_s[l][:, :, None], (A_GROUPS, CHUNK, CHUNK))
        proj_args = (w_in_l, row(sgu_g[l]), row(sgu_b[l]), w_s_l, b_s_b, cos_t, sin_t)
        attn_args = (lambda_q[l], lambda_k[l], row(subln_g[l]))
        merge_w = (w_pa[l].astype(BF16), w_pb[l].astype(BF16), w_o[l].astype(BF16),
                   row(ln1_g[l]), row(ln1_b[l]), w_r_hi, w_r_lo, b_r_b)
        moe_w = (w1, w3, w2, row(ln2_g[l]), row(ln2_b[l]))

        ya, qt, k, vt, ga, gb = _inproj(x, mod_lat, *proj_args, use_rope=True)
        cya, cqt, ck, cvt, cga, cgb = _inproj(ctx, mod_ctx, *proj_args, use_rope=False)
        yb = _attention(qt, k, vt, ck, cvt, *attn_args, lam_init=lam_init)
        x1, h2, route, gates, counts = _merge(x, ya, yb, ga, gb, mod_lat, *merge_w, alpha=alpha)
        x = _moe(h2, route, gates, counts, x1, mod_lat, *moe_w, alpha=alpha, layer=l)
        if not last:
            cyb = _attention(cqt, None, None, ck, cvt, *attn_args, lam_init=lam_init)
            c1, ch2, croute, cgates, ccounts = _merge(ctx, cya, cyb, cga, cgb, mod_ctx, *merge_w, alpha=alpha)
            ctx = _moe(ch2, croute, cgates, ccounts, c1, mod_ctx, *moe_w, alpha=alpha, layer=l)
    return x
```

```python
import functools
import math

import jax
import jax.numpy as jnp
from jax import lax
from jax.experimental import pallas as pl
from jax.experimental.pallas import tpu as pltpu

F32 = jnp.float32
BF16 = jnp.bfloat16

GRID_W = 64
CHUNK = 128
A_GROUPS = 8
A_WIDTH = 1024
HEADS = 8
QK_DIM = 64
V_DIM = 2 * QK_DIM
HEAD_COLS = HEADS * V_DIM
ROPE_BASE = 10000.0
N_EXPERTS = 16
N_GROUPS = 4
EXPERTS_PER_GROUP = N_EXPERTS // N_GROUPS
N_MOD = 6
PAIR_CLASSES = tuple((4 * g + a, 4 * g + b) for g in range(N_GROUPS)
                     for a in range(EXPERTS_PER_GROUP) for b in range(a + 1, EXPERTS_PER_GROUP))
PAIRS_PER_GROUP = len(PAIR_CLASSES) // N_GROUPS
CLASS_ROWS = 32
ROUTE_ROWS = 8
GATE_LANES = 128
LN_EPS = 1e-5
RMS_EPS = 1e-5
NEG_BIG = -1e30

VMEM_LIMIT_BYTES = 56 * 1024 * 1024

TOKEN_TILE = 256
ATTN_Q_TILE = 512
ATTN_KV_CHUNK = 512
ATTN_UNROLL = 4
SUM_ROWS = 16
MOE_TOKEN_TILE = 256
MOE_ROW_TILE = 256


def _params(semantics):
    return pltpu.CompilerParams(dimension_semantics=semantics, vmem_limit_bytes=VMEM_LIMIT_BYTES)


def _resident(shape):
    nd = len(shape)
    return pl.BlockSpec(shape, lambda *_: (0,) * nd, pipeline_mode=pl.Buffered(1))


def _split_bf16(x):
    bits = lax.bitcast_convert_type(x, jnp.uint32) & jnp.uint32(0xFFFF0000)
    hi = lax.bitcast_convert_type(bits, F32)
    return hi.astype(BF16), (x - hi).astype(BF16)


def _ln_rows(x):
    mu = jnp.mean(x, axis=-1, keepdims=True)
    xc = x - mu
    var = jnp.mean(xc * xc, axis=-1, keepdims=True)
    return xc * lax.rsqrt(var + LN_EPS)


def _mod_kernel(cond_ref, w_ref, b_ref, o_ref):
    c = cond_ref[...]
    s = c * jax.nn.sigmoid(c)
    o_ref[...] = jnp.dot(s, w_ref[...], preferred_element_type=F32, precision=lax.Precision.HIGHEST) + b_ref[...]


def _modulation(cond, w_mod, b_mod):
    rows, d = cond.shape
    return pl.pallas_call(
        _mod_kernel,
        grid=(N_MOD,),
        in_specs=[
            pl.BlockSpec((rows, d), lambda j: (0, 0)),
            pl.BlockSpec((d, d), lambda j: (0, j)),
            pl.BlockSpec((1, d), lambda j: (0, j)),
        ],
        out_specs=pl.BlockSpec((rows, d), lambda j: (0, j)),
        out_shape=jax.ShapeDtypeStruct((rows, N_MOD * d), F32),
        compiler_params=_params(("arbitrary",)),
        name="modulation",
    )(cond, w_mod, b_mod.reshape(1, -1))


def _rope(x, cos, sin_signed):
    n = x.shape[-1]
    reps = n // cos.shape[-1]
    c = jnp.tile(cos, (1, reps))
    s = jnp.tile(sin_signed, (1, reps))
    lane = lax.broadcasted_iota(jnp.int32, x.shape, 1)
    first = (lane & 31) < 16
    partner = jnp.where(first, pltpu.roll(x, n - 16, 1), pltpu.roll(x, 16, 1))
    return x * c + partner * s


def _inproj_kernel(x_ref, mod_ref, w_ref, sg_ref, sb_ref, ws_ref, bs_ref, cos_ref, sin_ref,
                   ya_ref, qt_ref, k_ref, vt_ref, ga_ref, gb_ref, *, d_model, use_rope, q_scale):
    d = d_model
    tm = x_ref.shape[1]
    shift = mod_ref[0, :, 0:d]
    scale = mod_ref[0, :, d:2 * d]
    h = (_ln_rows(x_ref[0]) * (1.0 + scale) + shift).astype(BF16)

    def proj(c0, width):
        return jnp.dot(h, w_ref[:, c0:c0 + width], preferred_element_type=F32)

    c_u, c_v = 0, A_WIDTH
    c_q = 2 * A_WIDTH
    c_k = c_q + HEAD_COLS
    c_vb = c_k + HEAD_COLS
    c_ga = c_vb + HEAD_COLS
    c_gb = c_ga + d

    u = jax.nn.gelu(proj(c_u, A_WIDTH))
    v = jax.nn.gelu(proj(c_v, A_WIDTH))
    vn = (_ln_rows(v) * sg_ref[...] + sb_ref[...]).astype(BF16)
    gd = A_WIDTH // A_GROUPS
    for pair in range(tm // (2 * CHUNK)):
        r0 = pair * 2 * CHUNK
        r1 = r0 + CHUNK
        for g in range(A_GROUPS):
            cols = slice(g * gd, (g + 1) * gd)
            rhs = jnp.concatenate([vn[r0:r0 + CHUNK, cols], vn[r1:r1 + CHUNK, cols]], axis=1)
            y = jnp.dot(ws_ref[g], rhs, preferred_element_type=F32)
            bias = bs_ref[g]
            ya_ref[0, r0:r0 + CHUNK, cols] = (u[r0:r0 + CHUNK, cols] * (y[:, :gd] + bias)).astype(BF16)
            ya_ref[0, r1:r1 + CHUNK, cols] = (u[r1:r1 + CHUNK, cols] * (y[:, gd:] + bias)).astype(BF16)

    q = proj(c_q, HEAD_COLS)
    k = proj(c_k, HEAD_COLS)
    if use_rope:
        q = _rope(q, cos_ref[...], sin_ref[...])
        k = _rope(k, cos_ref[...], sin_ref[...])
    q = q * q_scale
    k_ref[0] = k.astype(BF16)
    vb = proj(c_vb, HEAD_COLS)
    for hd in range(HEADS):
        cols = slice(hd * V_DIM, (hd + 1) * V_DIM)
        qt_ref[0, hd] = q[:, cols].T.astype(BF16)
        vt_ref[0, hd] = vb[:, cols].T.astype(BF16)

    ga_ref[0] = jax.nn.sigmoid(proj(c_ga, d)).astype(BF16)
    gb_ref[0] = jax.nn.sigmoid(proj(c_gb, d)).astype(BF16)


def _inproj(x, mod, w_in, sgu_g, sgu_b, w_s, b_s_b, cos_t, sin_t, *, use_rope):
    bsz, n, d = x.shape
    tm = min(TOKEN_TILE, n)
    nt = n // tm
    in_cols = w_in.shape[1]
    q_scale = (QK_DIM ** -0.5) * math.log2(math.e)
    tok = lambda b, i: (b, i, 0)
    tr = lambda b, i: (b, 0, 0, i)
    rope_map = (lambda b, i: (i, 0)) if use_rope else (lambda b, i: (0, 0))
    return pl.pallas_call(
        functools.partial(_inproj_kernel, d_model=d, use_rope=use_rope, q_scale=q_scale),
        grid=(bsz, nt),
        in_specs=[
            pl.BlockSpec((1, tm, d), tok),
            pl.BlockSpec((1, 1, N_MOD * d), lambda b, i: (b, 0, 0)),
            _resident((d, in_cols)),
            _resident((1, A_WIDTH)),
            _resident((1, A_WIDTH)),
            _resident((A_GROUPS, CHUNK, CHUNK)),
            _resident((A_GROUPS, CHUNK, CHUNK)),
            pl.BlockSpec((tm, 2 * QK_DIM), rope_map),
            pl.BlockSpec((tm, 2 * QK_DIM), rope_map),
        ],
        out_specs=[
            pl.BlockSpec((1, tm, A_WIDTH), tok),
            pl.BlockSpec((1, HEADS, V_DIM, tm), tr),
            pl.BlockSpec((1, tm, HEAD_COLS), tok),
            pl.BlockSpec((1, HEADS, V_DIM, tm), tr),
            pl.BlockSpec((1, tm, d), tok),
            pl.BlockSpec((1, tm, d), tok),
        ],
        out_shape=[
            jax.ShapeDtypeStruct((bsz, n, A_WIDTH), BF16),
            jax.ShapeDtypeStruct((bsz, HEADS, V_DIM, n), BF16),
            jax.ShapeDtypeStruct((bsz, n, HEAD_COLS), BF16),
            jax.ShapeDtypeStruct((bsz, HEADS, V_DIM, n), BF16),
            jax.ShapeDtypeStruct((bsz, n, d), BF16),
            jax.ShapeDtypeStruct((bsz, n, d), BF16),
        ],
        compiler_params=_params(("parallel", "parallel")),
        name="inproj_rope" if use_rope else "inproj_ctx",
    )(x, mod, w_in, sgu_g, sgu_b, w_s, b_s_b, cos_t, sin_t)


def _attn_kernel(*refs, n_lat_chunks, tk, lam_init):
    if n_lat_chunks:
        (qt_ref, kl_ref, vtl_ref, kc_ref, vtc_ref, lq_ref, lk_ref, sg_ref,
         o_ref, rhs_ref, s_ref, cm_ref, m_ref, acc_ref) = refs
    else:
        (qt_ref, kc_ref, vtc_ref, lq_ref, lk_ref, sg_ref,
         o_ref, rhs_ref, s_ref, cm_ref, m_ref, acc_ref) = refs
    tq = qt_ref.shape[-1]
    nc = kc_ref.shape[1]

    qt = qt_ref[0, 0]
    row = lax.broadcasted_iota(jnp.int32, qt.shape, 0)
    zero = jnp.zeros_like(qt)
    rhs_ref[:, 0:tq] = jnp.where(row < QK_DIM, qt, zero)
    rhs_ref[:, tq:2 * tq] = jnp.where(row >= QK_DIM, qt, zero)
    m_ref[...] = jnp.full(m_ref.shape, NEG_BIG, F32)
    acc_ref[...] = jnp.zeros(acc_ref.shape, F32)

    def stage_a(kc, slot, rows):
        s = jnp.dot(kc, rhs_ref[...], preferred_element_type=F32)
        s_ref[slot, 0:rows, :] = s
        cm_ref[slot] = jnp.max(s, axis=0, keepdims=True)

    def stage_b(vtc, slot, rows):
        m_old = m_ref[...]
        m_new = jnp.maximum(m_old, cm_ref[slot])
        alpha = jnp.exp2(m_old - m_new)
        p = jnp.exp2(s_ref[slot, 0:rows, :] - m_new)
        vt_ones = jnp.concatenate([vtc, jnp.ones((SUM_ROWS, rows), BF16)], axis=0)
        acc_ref[...] = alpha * acc_ref[...] + jnp.dot(vt_ones, p.astype(BF16), preferred_element_type=F32)
        m_ref[...] = m_new

    if n_lat_chunks:
        def lat_k(c):
            return kl_ref[0, pl.ds(pl.multiple_of(c * tk, tk), tk), :]

        def lat_vt(c):
            return vtl_ref[0, 0, :, pl.ds(pl.multiple_of(c * tk, tk), tk)]

        stage_a(lat_k(0), 0, tk)

        def pipelined(c0, n):
            for i in range(n):
                c = c0 + i
                if isinstance(c, int) and c + 1 == n_lat_chunks:
                    stage_a(kc_ref[0], (i + 1) % 2, nc)
                else:
                    stage_a(lat_k(c + 1), (i + 1) % 2, tk)
                stage_b(lat_vt(c), i % 2, tk)

        def body(j, carry):
            pipelined(ATTN_UNROLL * j, ATTN_UNROLL)
            return carry
        n_loop = n_lat_chunks // ATTN_UNROLL - 1
        lax.fori_loop(0, n_loop, body, 0)
        pipelined(n_loop * ATTN_UNROLL, n_lat_chunks - n_loop * ATTN_UNROLL)
        stage_b(vtc_ref[0, 0], n_lat_chunks % 2, nc)
    else:
        stage_a(kc_ref[0], 0, nc)
        stage_b(vtc_ref[0, 0], 0, nc)

    lq = lq_ref[...]
    lk = lk_ref[...]
    lam = (jnp.exp(jnp.sum(lq[0:1] * lk[0:1], keepdims=True))
           - jnp.exp(jnp.sum(lq[1:2] * lk[1:2], keepdims=True)) + lam_init)
    acc = acc_ref[0:V_DIM, :]
    l = acc_ref[V_DIM:V_DIM + 1, :]
    o = acc[:, 0:tq] / l[:, 0:tq] - lam * (acc[:, tq:2 * tq] / l[:, tq:2 * tq])
    ms = jnp.mean(o * o, axis=0, keepdims=True)
    on = o * lax.rsqrt(ms + RMS_EPS)
    o_ref[0] = (on.T * sg_ref[...] * (1.0 - lam_init)).astype(BF16)


def _attention(qt, k_lat, vt_lat, k_ctx, vt_ctx, lq, lk, subln_g, *, lam_init):
    bsz, _, _, nq = qt.shape
    nc = k_ctx.shape[1]
    tq = min(ATTN_Q_TILE, nq)
    has_lat = k_lat is not None
    tk = ATTN_KV_CHUNK
    n_lat_chunks = (k_lat.shape[1] // tk) if has_lat else 0
    assert n_lat_chunks % ATTN_UNROLL == 0 and ATTN_UNROLL % 2 == 0
    s_rows = tk if has_lat else nc
    in_specs = [pl.BlockSpec((1, 1, V_DIM, tq), lambda b, h, i: (b, h, 0, i))]
    args = [qt]
    if has_lat:
        ns = k_lat.shape[1]
        in_specs += [pl.BlockSpec((1, ns, V_DIM), lambda b, h, i: (b, 0, h)),
                     pl.BlockSpec((1, 1, V_DIM, ns), lambda b, h, i: (b, h, 0, 0))]
        args += [k_lat, vt_lat]
    in_specs += [pl.BlockSpec((1, nc, V_DIM), lambda b, h, i: (b, 0, h)),
                 pl.BlockSpec((1, 1, V_DIM, nc), lambda b, h, i: (b, h, 0, 0)),
                 pl.BlockSpec((2, QK_DIM), lambda b, h, i: (0, 0)),
                 pl.BlockSpec((2, QK_DIM), lambda b, h, i: (0, 0)),
                 pl.BlockSpec((1, V_DIM), lambda b, h, i: (0, 0))]
    args += [k_ctx, vt_ctx, lq, lk, subln_g]
    return pl.pallas_call(
        functools.partial(_attn_kernel, n_lat_chunks=n_lat_chunks, tk=tk, lam_init=lam_init),
        grid=(bsz, HEADS, nq // tq),
        in_specs=in_specs,
        out_specs=pl.BlockSpec((1, tq, V_DIM), lambda b, h, i: (b, i, h)),
        out_shape=jax.ShapeDtypeStruct((bsz, nq, HEAD_COLS), BF16),
        scratch_shapes=[
            pltpu.VMEM((V_DIM, 2 * tq), BF16),
            pltpu.VMEM((2, s_rows, 2 * tq), F32),
            pltpu.VMEM((2, 1, 2 * tq), F32),
            pltpu.VMEM((1, 2 * tq), F32),
            pltpu.VMEM((V_DIM + SUM_ROWS, 2 * tq), F32),
        ],
        compiler_params=_params(("parallel", "parallel", "arbitrary")),
        name="diff_attn_latent" if has_lat else "diff_attn_ctx",
    )(*args)


def _route_rows(logits_t):
    mx = jnp.max(logits_t, axis=0, keepdims=True)
    ex = jnp.exp(logits_t - mx)
    probs = ex / jnp.sum(ex, axis=0, keepdims=True)
    p = [probs[e:e + 1, :] for e in range(N_EXPERTS)]
    scores = []
    for g in range(N_GROUPS):
        a, b, c, d = p[4 * g:4 * g + 4]
        hi1, lo1 = jnp.maximum(a, b), jnp.minimum(a, b)
        hi2, lo2 = jnp.maximum(c, d), jnp.minimum(c, d)
        top1 = jnp.maximum(hi1, hi2)
        top2 = jnp.maximum(jnp.minimum(hi1, hi2), jnp.maximum(lo1, lo2))
        scores.append(top1 + top2)
    best = jnp.zeros_like(scores[0], dtype=jnp.int32)
    best_score = scores[0]
    for g in range(1, N_GROUPS):
        better = scores[g] > best_score
        best = jnp.where(better, g, best)
        best_score = jnp.where(better, scores[g], best_score)
    sel = []
    for e in range(N_EXPERTS):
        g = e // EXPERTS_PER_GROUP
        rank = jnp.zeros_like(best)
        for j in range(g * EXPERTS_PER_GROUP, (g + 1) * EXPERTS_PER_GROUP):
            if j == e:
                continue
            ahead = (p[j] >= p[e]) if j < e else (p[j] > p[e])
            rank = rank + ahead.astype(jnp.int32)
        sel.append((best == g) & (rank < 2))
    kept = [jnp.where(sel[e], p[e], 0.0) for e in range(N_EXPERTS)]
    denom = kept[0]
    for e in range(1, N_EXPERTS):
        denom = denom + kept[e]
    return sel, [kp / denom for kp in kept]


def _merge_kernel(x_ref, ya_ref, yb_ref, ga_ref, gb_ref, mod_ref, wpa_ref, wpb_ref, wo_ref,
                  g1_ref, b1_ref, wrh_ref, wrl_ref, br_ref, x1_ref, h2_ref, route_ref, counts_ref,
                  *, d_model, alpha):
    d = d_model
    tm = x_ref.shape[1]
    gate1 = mod_ref[0, :, 2 * d:3 * d]
    shift2 = mod_ref[0, :, 3 * d:4 * d]
    scale2 = mod_ref[0, :, 4 * d:5 * d]
    a = jnp.dot(ya_ref[0], wpa_ref[...], preferred_element_type=F32)
    b = jnp.dot(yb_ref[0], wpb_ref[...], preferred_element_type=F32)
    merged = (ga_ref[0].astype(F32) * a + gb_ref[0].astype(F32) * b).astype(BF16)
    mix = jnp.dot(merged, wo_ref[...], preferred_element_type=F32)
    x1 = _ln_rows(alpha * x_ref[0] + gate1 * mix) * g1_ref[...] + b1_ref[...]
    x1_ref[0] = x1
    h2 = _ln_rows(x1) * (1.0 + scale2) + shift2
    h2_ref[0, :, 0:d] = h2
    h2_hi, h2_lo = _split_bf16(h2)
    logits = (jnp.dot(h2_hi, wrh_ref[...], preferred_element_type=F32)
              + jnp.dot(h2_lo, wrh_ref[...], preferred_element_type=F32)
              + jnp.dot(h2_hi, wrl_ref[...], preferred_element_type=F32))
    logits_t = logits.T[0:N_EXPERTS, :] + jnp.tile(br_ref[...], (1, tm // br_ref.shape[1]))
    sel, gate_rows = _route_rows(logits_t)

    e_lo = jnp.full((1, tm), N_EXPERTS, jnp.int32)
    e_hi = jnp.full((1, tm), -1, jnp.int32)
    for e in range(N_EXPERTS):
        e_lo = jnp.where(sel[e], jnp.minimum(e_lo, e), e_lo)
        e_hi = jnp.where(sel[e], jnp.maximum(e_hi, e), e_hi)
    g_lo = jnp.zeros((1, tm), F32)
    g_hi = jnp.zeros((1, tm), F32)
    for e in range(N_EXPERTS):
        g_lo = jnp.where(e_lo == e, gate_rows[e], g_lo)
        g_hi = jnp.where(e_hi == e, gate_rows[e], g_hi)
    a = e_lo & (EXPERTS_PER_GROUP - 1)
    b = e_hi & (EXPERTS_PER_GROUP - 1)
    cls = (e_lo >> 2) * PAIRS_PER_GROUP + ((a * (7 - a)) >> 1) + (b - a - 1)
    cls_t = jnp.where(lax.broadcasted_iota(jnp.int32, (CLASS_ROWS, tm), 0) == cls, 1.0, 0.0)
    earlier = (lax.broadcasted_iota(jnp.int32, (tm, tm), 0)
               < lax.broadcasted_iota(jnp.int32, (tm, tm), 1))
    rank_t = jnp.dot(cls_t.astype(BF16), jnp.where(earlier, 1.0, 0.0).astype(BF16),
                     preferred_element_type=F32)
    rank = jnp.sum(cls_t * rank_t, axis=0, keepdims=True)
    route_ref[0] = jnp.concatenate(
        [cls, rank.astype(jnp.int32), jnp.zeros((ROUTE_ROWS - 2, tm), jnp.int32)], axis=0)
    h2_ref[0, :, d:d + GATE_LANES] = jnp.concatenate(
        [g_lo, g_hi, jnp.zeros((GATE_LANES - 2, tm), F32)], axis=0).T
    counts = jnp.sum(cls_t, axis=1, keepdims=True)
    counts_ref[0, 0] = jnp.broadcast_to(counts, (CLASS_ROWS, 128)).astype(jnp.int32)


def _merge(x, ya, yb, ga, gb, mod, w_pa, w_pb, w_o, g1, b1, w_r_hi, w_r_lo, b_r_b, *, alpha):
    bsz, n, d = x.shape
    tm = min(TOKEN_TILE, n)
    nt = n // tm
    tok = lambda b, i: (b, i, 0)
    return pl.pallas_call(
        functools.partial(_merge_kernel, d_model=d, alpha=alpha),
        grid=(bsz, nt),
        in_specs=[
            pl.BlockSpec((1, tm, d), tok),
            pl.BlockSpec((1, tm, A_WIDTH), tok),
            pl.BlockSpec((1, tm, HEAD_COLS), tok),
            pl.BlockSpec((1, tm, d), tok),
            pl.BlockSpec((1, tm, d), tok),
            pl.BlockSpec((1, 1, N_MOD * d), lambda b, i: (b, 0, 0)),
            _resident((A_WIDTH, d)),
            _resident((HEAD_COLS, d)),
            _resident((d, d)),
            _resident((1, d)),
            _resident((1, d)),
            _resident((d, GATE_LANES)),
            _resident((d, GATE_LANES)),
            _resident((N_EXPERTS, 128)),
        ],
        out_specs=[
            pl.BlockSpec((1, tm, d), tok),
            pl.BlockSpec((1, tm, d + GATE_LANES), tok),
            pl.BlockSpec((1, ROUTE_ROWS, tm), lambda b, i: (b, 0, i)),
            pl.BlockSpec((1, 1, CLASS_ROWS, 128), lambda b, i: (b, i, 0, 0)),
        ],
        out_shape=[
            jax.ShapeDtypeStruct((bsz, n, d), F32),
            jax.ShapeDtypeStruct((bsz, n, d + GATE_LANES), F32),
            jax.ShapeDtypeStruct((bsz, ROUTE_ROWS, n), jnp.int32),
            jax.ShapeDtypeStruct((bsz, nt, CLASS_ROWS, 128), jnp.int32),
        ],
        compiler_params=_params(("parallel", "parallel")),
        name="merge_route",
    )(x, ya, yb, ga, gb, mod, w_pa, w_pb, w_o, g1, b1, w_r_hi, w_r_lo, b_r_b)


def _dispatch_plan(route, counts, n_tiles_max):
    bsz, _, n = route.shape
    nt = counts.shape[1]
    tm = n // nt
    n_src = bsz * nt
    cnt = counts[..., 0].reshape(n_src, CLASS_ROWS)
    total = jnp.sum(cnt, axis=0)
    tiles_per_class = (total + MOE_ROW_TILE - 1) // MOE_ROW_TILE
    classes = jnp.arange(CLASS_ROWS, dtype=jnp.int32)
    tile_end = jnp.sum(jnp.where(classes[None, :] <= classes[:, None], tiles_per_class[None, :], 0), axis=1)
    seg_start = (tile_end - tiles_per_class) * MOE_ROW_TILE
    src = jnp.arange(n_src, dtype=jnp.int32)
    before = jnp.sum(jnp.where((src[None, :] < src[:, None])[:, :, None], cnt[None, :, :], 0), axis=1)
    base = (seg_start[None, :] + before).reshape(bsz, nt, 1, CLASS_ROWS)
    cls = route[:, 0].reshape(bsz, nt, tm)
    rank = route[:, 1].reshape(bsz, nt, tm)
    pos = jnp.sum(jnp.where(cls[..., None] == classes, base, 0), axis=-1) + rank
    tile_ids = jnp.arange(n_tiles_max, dtype=jnp.int32)
    tile_class = jnp.sum((tile_end[None, :] <= tile_ids[:, None]).astype(jnp.int32), axis=1)
    tile_class = jnp.minimum(tile_class, len(PAIR_CLASSES) - 1)
    pairs = jnp.asarray(PAIR_CLASSES, dtype=jnp.int32)
    tile_pair = jnp.sum(jnp.where((tile_class[:, None] == classes[None, :len(PAIR_CLASSES)])[:, :, None],
                                  pairs[None, :, :], 0), axis=1)
    pos_tiles = pos.reshape(bsz * nt * (tm // MOE_TOKEN_TILE), 1, MOE_TOKEN_TILE)
    return pos_tiles, tile_pair[:, 0], tile_pair[:, 1], tile_end[-1:]


def _dispatch_kernel(pos_ref, h_ref, hs_in_hbm, hs_hbm, sem, *, tile):
    del hs_in_hbm

    def row_copy(t, dst_row):
        return pltpu.make_async_copy(h_ref.at[0, pl.ds(t, 1), :], hs_hbm.at[pl.ds(dst_row, 1), :], sem)

    def issue(t, carry):
        row_copy(t, pos_ref[0, 0, t]).start()
        return carry
    lax.fori_loop(0, tile, issue, 0, unroll=8)

    def drain(t, carry):
        row_copy(0, 0).wait()
        return carry
    lax.fori_loop(0, tile, drain, 0, unroll=8)


def _dispatch(h2, pos_tiles, hs_init):
    bsz, n, width = h2.shape
    n_rows = hs_init.shape[0]
    tile = pos_tiles.shape[-1]
    nt = n // tile
    return pl.pallas_call(
        functools.partial(_dispatch_kernel, tile=tile),
        grid=(bsz, nt),
        in_specs=[
            pl.BlockSpec((1, 1, tile), lambda b, i: (b * nt + i, 0, 0), memory_space=pltpu.SMEM),
            pl.BlockSpec((1, tile, width), lambda b, i: (b, i, 0)),
            pl.BlockSpec(memory_space=pl.ANY),
        ],
        out_specs=pl.BlockSpec(memory_space=pl.ANY),
        out_shape=jax.ShapeDtypeStruct((n_rows, width), F32),
        scratch_shapes=[pltpu.SemaphoreType.DMA(())],
        input_output_aliases={2: 0},
        compiler_params=_params(("arbitrary", "arbitrary")),
        name="moe_dispatch",
    )(pos_tiles, h2, hs_init)


def _experts_kernel(lo_ref, hi_ref, nv_ref, hs_ref, w1l_ref, w3l_ref, w2l_ref, w1h_ref, w3h_ref, w2h_ref,
                    ys_ref, w1l_b, w3l_b, w2l_b, w1h_b, w3h_b, w2h_b, *, d_model):
    d = d_model
    j = pl.program_id(0)
    prev = jnp.maximum(j - 1, 0)

    @pl.when((j == 0) | (lo_ref[j] != lo_ref[prev]))
    def _():
        w1l_b[...] = w1l_ref[0, 0].astype(BF16)
        w3l_b[...] = w3l_ref[0, 0].astype(BF16)
        w2l_b[...] = w2l_ref[0, 0].astype(BF16)

    @pl.when((j == 0) | (hi_ref[j] != hi_ref[prev]))
    def _():
        w1h_b[...] = w1h_ref[0, 0].astype(BF16)
        w3h_b[...] = w3h_ref[0, 0].astype(BF16)
        w2h_b[...] = w2h_ref[0, 0].astype(BF16)

    def expert(h, w1, w3, w2):
        a = jnp.dot(h, w1[...], preferred_element_type=F32)
        b = jnp.dot(h, w3[...], preferred_element_type=F32)
        hid = (a * jax.nn.sigmoid(a) * b).astype(BF16)
        return jnp.dot(hid, w2[...], preferred_element_type=F32)

    @pl.when(j < nv_ref[0])
    def _():
        h = hs_ref[:, 0:d].astype(BF16)
        gates = hs_ref[:, d:d + GATE_LANES]
        ys_ref[...] = (gates[:, 0:1] * expert(h, w1l_b, w3l_b, w2l_b)
                       + gates[:, 1:2] * expert(h, w1h_b, w3h_b, w2h_b))

    @pl.when(j >= nv_ref[0])
    def _():
        ys_ref[...] = jnp.zeros(ys_ref.shape, F32)


def _experts(hs, tile_lo, tile_hi, n_valid, w1, w3, w2, *, layer):
    n_rows, width = hs.shape
    d = width - GATE_LANES
    de = w1.shape[-1]
    n_tiles = n_rows // MOE_ROW_TILE
    lo_map = lambda j, lo, hi, nv: (layer, lo[j], 0, 0)
    hi_map = lambda j, lo, hi, nv: (layer, hi[j], 0, 0)
    up, down = (1, 1, d, de), (1, 1, de, d)
    return pl.pallas_call(
        functools.partial(_experts_kernel, d_model=d),
        grid_spec=pltpu.PrefetchScalarGridSpec(
            num_scalar_prefetch=3,
            grid=(n_tiles,),
            in_specs=[
                pl.BlockSpec((MOE_ROW_TILE, width), lambda j, lo, hi, nv: (jnp.minimum(j, nv[0] - 1), 0)),
                pl.BlockSpec(up, lo_map), pl.BlockSpec(up, lo_map), pl.BlockSpec(down, lo_map),
                pl.BlockSpec(up, hi_map), pl.BlockSpec(up, hi_map), pl.BlockSpec(down, hi_map),
            ],
            out_specs=pl.BlockSpec((MOE_ROW_TILE, d), lambda j, lo, hi, nv: (j, 0)),
            scratch_shapes=[pltpu.VMEM((d, de), BF16), pltpu.VMEM((d, de), BF16), pltpu.VMEM((de, d), BF16)] * 2,
        ),
        out_shape=jax.ShapeDtypeStruct((n_rows, d), F32),
        compiler_params=_params(("arbitrary",)),
        name="moe_experts",
    )(tile_lo, tile_hi, n_valid, hs, w1, w3, w2, w1, w3, w2)


def _combine_kernel(pos_ref, nxt_ref, x1_ref, mod_ref, g2_ref, b2_ref, ys_hbm,
                    o_ref, ybuf, sem, *, tile, d_model, alpha):
    d = d_model
    step = pl.program_id(0) * pl.num_programs(1) + pl.program_id(1)
    n_steps = pl.num_programs(0) * pl.num_programs(1)
    slot = step % 2

    def row_copy(src_row, slot_, dst_row):
        return pltpu.make_async_copy(ys_hbm.at[pl.ds(src_row, 1), :],
                                     ybuf.at[slot_, pl.ds(dst_row, 1), :], sem.at[slot_])

    def gather(p_ref, slot_):
        def issue(t, carry):
            row_copy(p_ref[0, 0, t], slot_, t).start()
            return carry
        lax.fori_loop(0, tile, issue, 0, unroll=8)

    @pl.when(step == 0)
    def _():
        gather(pos_ref, 0)

    @pl.when(step + 1 < n_steps)
    def _():
        gather(nxt_ref, 1 - slot)

    def drain(t, carry):
        row_copy(0, slot, 0).wait()
        return carry
    lax.fori_loop(0, tile, drain, 0, unroll=8)

    gate2 = mod_ref[0, :, 5 * d:6 * d]
    o_ref[0] = _ln_rows(alpha * x1_ref[0] + gate2 * ybuf[slot]) * g2_ref[...] + b2_ref[...]


def _combine(ys, pos_tiles, x1, mod, g2, b2, *, alpha):
    bsz, n, d = x1.shape
    tile = pos_tiles.shape[-1]
    nt = n // tile
    n_steps = bsz * nt
    tok = lambda b, i: (b, i, 0)
    smem_tile = lambda index_map: pl.BlockSpec((1, 1, tile), index_map, memory_space=pltpu.SMEM)
    return pl.pallas_call(
        functools.partial(_combine_kernel, tile=tile, d_model=d, alpha=alpha),
        grid=(bsz, nt),
        in_specs=[
            smem_tile(lambda b, i: (b * nt + i, 0, 0)),
            smem_tile(lambda b, i: (jnp.minimum(b * nt + i + 1, n_steps - 1), 0, 0)),
            pl.BlockSpec((1, tile, d), tok),
            pl.BlockSpec((1, 1, N_MOD * d), lambda b, i: (b, 0, 0)),
            pl.BlockSpec((1, d), lambda b, i: (0, 0)),
            pl.BlockSpec((1, d), lambda b, i: (0, 0)),
            pl.BlockSpec(memory_space=pl.ANY),
        ],
        out_specs=pl.BlockSpec((1, tile, d), tok),
        out_shape=jax.ShapeDtypeStruct((bsz, n, d), F32),
        scratch_shapes=[pltpu.VMEM((2, tile, d), F32), pltpu.SemaphoreType.DMA((2,))],
        compiler_params=_params(("arbitrary", "arbitrary")),
        name="moe_combine",
    )(pos_tiles, pos_tiles, x1, mod, g2, b2, ys)


def _moe(h2g, route, counts, x1, mod, w1, w3, w2, g2, b2, *, alpha, layer):
    bsz, n, _ = x1.shape
    n_rows = bsz * n + len(PAIR_CLASSES) * MOE_ROW_TILE
    pos_tiles, tile_lo, tile_hi, n_valid = _dispatch_plan(route, counts, n_rows // MOE_ROW_TILE)
    hs = _dispatch(h2g, pos_tiles, jnp.zeros((n_rows, h2g.shape[-1]), F32))
    ys = _experts(hs, tile_lo, tile_hi, n_valid, w1, w3, w2, layer=layer)
    return _combine(ys, pos_tiles, x1, mod, g2, b2, alpha=alpha)


def _rope_tables(n_tokens):
    rows = n_tokens // GRID_W
    row = jnp.repeat(jnp.arange(rows, dtype=F32), GRID_W)
    col = jnp.tile(jnp.arange(GRID_W, dtype=F32), rows)
    n_freq = QK_DIM // 4
    inv_freq = ROPE_BASE ** (-jnp.arange(n_freq, dtype=F32) / n_freq)
    ang_r = row[:, None] * inv_freq
    ang_c = col[:, None] * inv_freq
    cos64 = jnp.concatenate([jnp.cos(ang_r), jnp.cos(ang_r), jnp.cos(ang_c), jnp.cos(ang_c)], axis=-1)
    sin64 = jnp.concatenate([-jnp.sin(ang_r), jnp.sin(ang_r), -jnp.sin(ang_c), jnp.sin(ang_c)], axis=-1)
    return jnp.tile(cos64, (1, 2)), jnp.tile(sin64, (1, 2))


def kernel(x, c, ctx, c_ctx, w_mod, b_mod, w_in, sgu_g, sgu_b, w_s, b_s, lambda_q, lambda_k, subln_g,
           w_pa, w_pb, w_o, ln1_g, ln1_b, w_router, b_router, w1, w3, w2, ln2_g, ln2_b):
    bsz, n_lat, d = x.shape
    depth = w_mod.shape[0]
    alpha = (2.0 * depth) ** 0.25
    cos_t, sin_t = _rope_tables(n_lat)

    cond = jnp.zeros((8, d), F32).at[0:bsz].set(c).at[bsz].set(c_ctx)
    w_r_pad = jnp.pad(w_router, ((0, 0), (0, GATE_LANES - N_EXPERTS)))
    w_r_hi, w_r_lo = _split_bf16(w_r_pad)
    b_r_b = jnp.broadcast_to(b_router[:, None], (N_EXPERTS, 128))
    row = lambda v: v.reshape(1, -1)

    for l in range(depth):
        last = l == depth - 1
        lam_init = 0.8 - 0.6 * math.exp(-0.3 * l)
        mod = _modulation(cond, w_mod[l], b_mod[l])
        mod_lat = mod[0:bsz, None, :]
        mod_ctx = jnp.broadcast_to(mod[bsz][None, None, :], (bsz, 1, N_MOD * d))
        w_in_l = w_in[l].astype(BF16)
        w_s_l = w_s[l].astype(BF16)
        b_s_b = jnp.broadcast_to(b_s[l][:, :, None], (A_GROUPS, CHUNK, CHUNK))
        proj_args = (w_in_l, row(sgu_g[l]), row(sgu_b[l]), w_s_l, b_s_b, cos_t, sin_t)
        attn_args = (lambda_q[l], lambda_k[l], row(subln_g[l]))
        merge_w = (w_pa[l].astype(BF16), w_pb[l].astype(BF16), w_o[l].astype(BF16),
                   row(ln1_g[l]), row(ln1_b[l]), w_r_hi, w_r_lo, b_r_b)
        moe_w = (w1, w3, w2, row(ln2_g[l]), row(ln2_b[l]))

        ya, qt, k, vt, ga, gb = _inproj(x, mod_lat, *proj_args, use_rope=True)
        cya, cqt, ck, cvt, cga, cgb = _inproj(ctx, mod_ctx, *proj_args, use_rope=False)
        yb = _attention(qt, k, vt, ck, cvt, *attn_args, lam_init=lam_init)
        x1, h2g, route, counts = _merge(x, ya, yb, ga, gb, mod_lat, *merge_w, alpha=alpha)
        x = _moe(h2g, route, counts, x1, mod_lat, *moe_w, alpha=alpha, layer=l)
        if not last:
            cyb = _attention(cqt, None, None, ck, cvt, *attn_args, lam_init=lam_init)
            c1, ch2g, croute, ccounts = _merge(ctx, cya, cyb, cga, cgb, mod_ctx, *merge_w, alpha=alpha)
            ctx = _moe(ch2g, croute, ccounts, c1, mod_ctx, *moe_w, alpha=alpha, layer=l)
    return x
```

```python
import functools
import math

import jax
import jax.numpy as jnp
from jax import lax
from jax.experimental import pallas as pl
from jax.experimental.pallas import tpu as pltpu

F32 = jnp.float32
BF16 = jnp.bfloat16

GRID_W = 64
CHUNK = 128
A_GROUPS = 8
A_WIDTH = 1024
HEADS = 8
QK_DIM = 64
V_DIM = 2 * QK_DIM
HEAD_COLS = HEADS * V_DIM
ROPE_BASE = 10000.0
N_EXPERTS = 16
N_GROUPS = 4
EXPERTS_PER_GROUP = N_EXPERTS // N_GROUPS
N_MOD = 6
PAIR_CLASSES = tuple((4 * g + a, 4 * g + b) for g in range(N_GROUPS)
                     for a in range(EXPERTS_PER_GROUP) for b in range(a + 1, EXPERTS_PER_GROUP))
PAIRS_PER_GROUP = len(PAIR_CLASSES) // N_GROUPS
CLASS_ROWS = 32
ROUTE_ROWS = 8
GATE_LANES = 128
LN_EPS = 1e-5
RMS_EPS = 1e-5
NEG_BIG = -1e30

VMEM_LIMIT_BYTES = 56 * 1024 * 1024

TOKEN_TILE = 256
ATTN_Q_TILE = 512
ATTN_KV_CHUNK = 512
ATTN_UNROLL = 6
SUM_ROWS = 16
MOE_TOKEN_TILE = 256
MOE_ROW_TILE = 256
DMA_QUEUES = 2


def _params(semantics):
    return pltpu.CompilerParams(dimension_semantics=semantics, vmem_limit_bytes=VMEM_LIMIT_BYTES)


def _resident(shape):
    nd = len(shape)
    return pl.BlockSpec(shape, lambda *_: (0,) * nd, pipeline_mode=pl.Buffered(1))


def _split_bf16(x):
    bits = lax.bitcast_convert_type(x, jnp.uint32) & jnp.uint32(0xFFFF0000)
    hi = lax.bitcast_convert_type(bits, F32)
    return hi.astype(BF16), (x - hi).astype(BF16)


def _ln_rows(x):
    mu = jnp.mean(x, axis=-1, keepdims=True)
    xc = x - mu
    var = jnp.mean(xc * xc, axis=-1, keepdims=True)
    return xc * lax.rsqrt(var + LN_EPS)


def _mod_kernel(cond_ref, w_ref, b_ref, o_ref):
    c = cond_ref[...]
    s = c * jax.nn.sigmoid(c)
    o_ref[...] = jnp.dot(s, w_ref[...], preferred_element_type=F32, precision=lax.Precision.HIGHEST) + b_ref[...]


def _modulation(cond, w_mod, b_mod):
    rows, d = cond.shape
    return pl.pallas_call(
        _mod_kernel,
        grid=(N_MOD,),
        in_specs=[
            pl.BlockSpec((rows, d), lambda j: (0, 0)),
            pl.BlockSpec((d, d), lambda j: (0, j)),
            pl.BlockSpec((1, d), lambda j: (0, j)),
        ],
        out_specs=pl.BlockSpec((rows, d), lambda j: (0, j)),
        out_shape=jax.ShapeDtypeStruct((rows, N_MOD * d), F32),
        compiler_params=_params(("arbitrary",)),
        name="modulation",
    )(cond, w_mod, b_mod.reshape(1, -1))


def _rope(x, cos, sin_signed):
    n = x.shape[-1]
    reps = n // cos.shape[-1]
    c = jnp.tile(cos, (1, reps))
    s = jnp.tile(sin_signed, (1, reps))
    lane = lax.broadcasted_iota(jnp.int32, x.shape, 1)
    first = (lane & 31) < 16
    partner = jnp.where(first, pltpu.roll(x, n - 16, 1), pltpu.roll(x, 16, 1))
    return x * c + partner * s


def _inproj_kernel(x_ref, mod_ref, w_ref, sg_ref, sb_ref, ws_ref, bs_ref, cos_ref, sin_ref,
                   ya_ref, qt_ref, k_ref, vt_ref, ga_ref, gb_ref, *, d_model, use_rope, q_scale):
    d = d_model
    tm = x_ref.shape[1]
    shift = mod_ref[0, :, 0:d]
    scale = mod_ref[0, :, d:2 * d]
    h = (_ln_rows(x_ref[0]) * (1.0 + scale) + shift).astype(BF16)

    def proj(c0, width):
        return jnp.dot(h, w_ref[:, c0:c0 + width], preferred_element_type=F32)

    c_u, c_v = 0, A_WIDTH
    c_q = 2 * A_WIDTH
    c_k = c_q + HEAD_COLS
    c_vb = c_k + HEAD_COLS
    c_ga = c_vb + HEAD_COLS
    c_gb = c_ga + d

    u = jax.nn.gelu(proj(c_u, A_WIDTH))
    v = jax.nn.gelu(proj(c_v, A_WIDTH))
    vn = (_ln_rows(v) * sg_ref[...] + sb_ref[...]).astype(BF16)
    gd = A_WIDTH // A_GROUPS
    for pair in range(tm // (2 * CHUNK)):
        r0 = pair * 2 * CHUNK
        r1 = r0 + CHUNK
        for g in range(A_GROUPS):
            cols = slice(g * gd, (g + 1) * gd)
            rhs = jnp.concatenate([vn[r0:r0 + CHUNK, cols], vn[r1:r1 + CHUNK, cols]], axis=1)
            y = jnp.dot(ws_ref[g], rhs, preferred_element_type=F32)
            bias = bs_ref[g]
            ya_ref[0, r0:r0 + CHUNK, cols] = (u[r0:r0 + CHUNK, cols] * (y[:, :gd] + bias)).astype(BF16)
            ya_ref[0, r1:r1 + CHUNK, cols] = (u[r1:r1 + CHUNK, cols] * (y[:, gd:] + bias)).astype(BF16)

    q = proj(c_q, HEAD_COLS)
    k = proj(c_k, HEAD_COLS)
    if use_rope:
        q = _rope(q, cos_ref[...], sin_ref[...])
        k = _rope(k, cos_ref[...], sin_ref[...])
    q = q * q_scale
    k_ref[0] = k.astype(BF16)
    vb = proj(c_vb, HEAD_COLS)
    for hd in range(HEADS):
        cols = slice(hd * V_DIM, (hd + 1) * V_DIM)
        qt_ref[0, hd] = q[:, cols].T.astype(BF16)
        vt_ref[0, hd] = vb[:, cols].T.astype(BF16)

    ga_ref[0] = jax.nn.sigmoid(proj(c_ga, d)).astype(BF16)
    gb_ref[0] = jax.nn.sigmoid(proj(c_gb, d)).astype(BF16)


def _inproj(x, mod, w_in, sgu_g, sgu_b, w_s, b_s_b, cos_t, sin_t, *, use_rope):
    bsz, n, d = x.shape
    tm = min(TOKEN_TILE, n)
    nt = n // tm
    in_cols = w_in.shape[1]
    q_scale = (QK_DIM ** -0.5) * math.log2(math.e)
    tok = lambda b, i: (b, i, 0)
    tr = lambda b, i: (b, 0, 0, i)
    rope_map = (lambda b, i: (i, 0)) if use_rope else (lambda b, i: (0, 0))
    return pl.pallas_call(
        functools.partial(_inproj_kernel, d_model=d, use_rope=use_rope, q_scale=q_scale),
        grid=(bsz, nt),
        in_specs=[
            pl.BlockSpec((1, tm, d), tok),
            pl.BlockSpec((1, 1, N_MOD * d), lambda b, i: (b, 0, 0)),
            _resident((d, in_cols)),
            _resident((1, A_WIDTH)),
            _resident((1, A_WIDTH)),
            _resident((A_GROUPS, CHUNK, CHUNK)),
            _resident((A_GROUPS, CHUNK, CHUNK)),
            pl.BlockSpec((tm, 2 * QK_DIM), rope_map),
            pl.BlockSpec((tm, 2 * QK_DIM), rope_map),
        ],
        out_specs=[
            pl.BlockSpec((1, tm, A_WIDTH), tok),
            pl.BlockSpec((1, HEADS, V_DIM, tm), tr),
            pl.BlockSpec((1, tm, HEAD_COLS), tok),
            pl.BlockSpec((1, HEADS, V_DIM, tm), tr),
            pl.BlockSpec((1, tm, d), tok),
            pl.BlockSpec((1, tm, d), tok),
        ],
        out_shape=[
            jax.ShapeDtypeStruct((bsz, n, A_WIDTH), BF16),
            jax.ShapeDtypeStruct((bsz, HEADS, V_DIM, n), BF16),
            jax.ShapeDtypeStruct((bsz, n, HEAD_COLS), BF16),
            jax.ShapeDtypeStruct((bsz, HEADS, V_DIM, n), BF16),
            jax.ShapeDtypeStruct((bsz, n, d), BF16),
            jax.ShapeDtypeStruct((bsz, n, d), BF16),
        ],
        compiler_params=_params(("parallel", "parallel")),
        name="inproj_rope" if use_rope else "inproj_ctx",
    )(x, mod, w_in, sgu_g, sgu_b, w_s, b_s_b, cos_t, sin_t)


def _attn_kernel(*refs, n_lat_chunks, tk, lam_init):
    if n_lat_chunks:
        (qt_ref, kl_ref, vtl_ref, kc_ref, vtc_ref, lq_ref, lk_ref, sg_ref,
         o_ref, rhs_ref, s_ref, cm_ref, m_ref, acc_ref) = refs
    else:
        (qt_ref, kc_ref, vtc_ref, lq_ref, lk_ref, sg_ref,
         o_ref, rhs_ref, s_ref, cm_ref, m_ref, acc_ref) = refs
    tq = qt_ref.shape[-1]
    nc = kc_ref.shape[1]

    qt = qt_ref[0, 0]
    row = lax.broadcasted_iota(jnp.int32, qt.shape, 0)
    zero = jnp.zeros_like(qt)
    rhs_ref[:, 0:tq] = jnp.where(row < QK_DIM, qt, zero)
    rhs_ref[:, tq:2 * tq] = jnp.where(row >= QK_DIM, qt, zero)
    m_ref[...] = jnp.full(m_ref.shape, NEG_BIG, F32)
    acc_ref[...] = jnp.zeros(acc_ref.shape, F32)

    def stage_a(kc, slot, rows):
        s = jnp.dot(kc, rhs_ref[...], preferred_element_type=F32)
        s_ref[slot, 0:rows, :] = s
        cm_ref[slot] = jnp.max(s, axis=0, keepdims=True)

    def stage_b(vtc, slot, rows):
        m_old = m_ref[...]
        m_new = jnp.maximum(m_old, cm_ref[slot])
        alpha = jnp.exp2(m_old - m_new)
        p = jnp.exp2(s_ref[slot, 0:rows, :] - m_new)
        vt_ones = jnp.concatenate([vtc, jnp.ones((SUM_ROWS, rows), BF16)], axis=0)
        acc_ref[...] = alpha * acc_ref[...] + jnp.dot(vt_ones, p.astype(BF16), preferred_element_type=F32)
        m_ref[...] = m_new

    if n_lat_chunks:
        def lat_k(c):
            return kl_ref[0, pl.ds(pl.multiple_of(c * tk, tk), tk), :]

        def lat_vt(c):
            return vtl_ref[0, 0, :, pl.ds(pl.multiple_of(c * tk, tk), tk)]

        stage_a(lat_k(0), 0, tk)

        def pipelined(c0, n):
            for i in range(n):
                c = c0 + i
                if isinstance(c, int) and c + 1 == n_lat_chunks:
                    stage_a(kc_ref[0], (i + 1) % 2, nc)
                else:
                    stage_a(lat_k(c + 1), (i + 1) % 2, tk)
                stage_b(lat_vt(c), i % 2, tk)

        def body(j, carry):
            pipelined(ATTN_UNROLL * j, ATTN_UNROLL)
            return carry
        n_loop = (n_lat_chunks - 2) // ATTN_UNROLL
        lax.fori_loop(0, n_loop, body, 0)
        pipelined(n_loop * ATTN_UNROLL, n_lat_chunks - n_loop * ATTN_UNROLL)
        stage_b(vtc_ref[0, 0], n_lat_chunks % 2, nc)
    else:
        stage_a(kc_ref[0], 0, nc)
        stage_b(vtc_ref[0, 0], 0, nc)

    lq = lq_ref[...]
    lk = lk_ref[...]
    lam = (jnp.exp(jnp.sum(lq[0:1] * lk[0:1], keepdims=True))
           - jnp.exp(jnp.sum(lq[1:2] * lk[1:2], keepdims=True)) + lam_init)
    acc = acc_ref[0:V_DIM, :]
    l = acc_ref[V_DIM:V_DIM + 1, :]
    o = acc[:, 0:tq] / l[:, 0:tq] - lam * (acc[:, tq:2 * tq] / l[:, tq:2 * tq])
    ms = jnp.mean(o * o, axis=0, keepdims=True)
    on = o * lax.rsqrt(ms + RMS_EPS)
    o_ref[0] = (on.T * sg_ref[...] * (1.0 - lam_init)).astype(BF16)


def _attention(qt, k_lat, vt_lat, k_ctx, vt_ctx, lq, lk, subln_g, *, lam_init):
    bsz, _, _, nq = qt.shape
    nc = k_ctx.shape[1]
    tq = min(ATTN_Q_TILE, nq)
    has_lat = k_lat is not None
    tk = ATTN_KV_CHUNK
    n_lat_chunks = (k_lat.shape[1] // tk) if has_lat else 0
    assert n_lat_chunks % 2 == 0 and ATTN_UNROLL % 2 == 0
    s_rows = tk if has_lat else nc
    in_specs = [pl.BlockSpec((1, 1, V_DIM, tq), lambda b, h, i: (b, h, 0, i))]
    args = [qt]
    if has_lat:
        ns = k_lat.shape[1]
        in_specs += [pl.BlockSpec((1, ns, V_DIM), lambda b, h, i: (b, 0, h)),
                     pl.BlockSpec((1, 1, V_DIM, ns), lambda b, h, i: (b, h, 0, 0))]
        args += [k_lat, vt_lat]
    in_specs += [pl.BlockSpec((1, nc, V_DIM), lambda b, h, i: (b, 0, h)),
                 pl.BlockSpec((1, 1, V_DIM, nc), lambda b, h, i: (b, h, 0, 0)),
                 pl.BlockSpec((2, QK_DIM), lambda b, h, i: (0, 0)),
                 pl.BlockSpec((2, QK_DIM), lambda b, h, i: (0, 0)),
                 pl.BlockSpec((1, V_DIM), lambda b, h, i: (0, 0))]
    args += [k_ctx, vt_ctx, lq, lk, subln_g]
    return pl.pallas_call(
        functools.partial(_attn_kernel, n_lat_chunks=n_lat_chunks, tk=tk, lam_init=lam_init),
        grid=(bsz, HEADS, nq // tq),
        in_specs=in_specs,
        out_specs=pl.BlockSpec((1, tq, V_DIM), lambda b, h, i: (b, i, h)),
        out_shape=jax.ShapeDtypeStruct((bsz, nq, HEAD_COLS), BF16),
        scratch_shapes=[
            pltpu.VMEM((V_DIM, 2 * tq), BF16),
            pltpu.VMEM((2, s_rows, 2 * tq), F32),
            pltpu.VMEM((2, 1, 2 * tq), F32),
            pltpu.VMEM((1, 2 * tq), F32),
            pltpu.VMEM((V_DIM + SUM_ROWS, 2 * tq), F32),
        ],
        compiler_params=_params(("parallel", "parallel", "arbitrary")),
        name="diff_attn_latent" if has_lat else "diff_attn_ctx",
    )(*args)


def _route_rows(logits_t):
    mx = jnp.max(logits_t, axis=0, keepdims=True)
    ex = jnp.exp(logits_t - mx)
    probs = ex / jnp.sum(ex, axis=0, keepdims=True)
    p = [probs[e:e + 1, :] for e in range(N_EXPERTS)]
    scores = []
    for g in range(N_GROUPS):
        a, b, c, d = p[4 * g:4 * g + 4]
        hi1, lo1 = jnp.maximum(a, b), jnp.minimum(a, b)
        hi2, lo2 = jnp.maximum(c, d), jnp.minimum(c, d)
        top1 = jnp.maximum(hi1, hi2)
        top2 = jnp.maximum(jnp.minimum(hi1, hi2), jnp.maximum(lo1, lo2))
        scores.append(top1 + top2)
    best = jnp.zeros_like(scores[0], dtype=jnp.int32)
    best_score = scores[0]
    for g in range(1, N_GROUPS):
        better = scores[g] > best_score
        best = jnp.where(better, g, best)
        best_score = jnp.where(better, scores[g], best_score)
    sel = []
    for e in range(N_EXPERTS):
        g = e // EXPERTS_PER_GROUP
        rank = jnp.zeros_like(best)
        for j in range(g * EXPERTS_PER_GROUP, (g + 1) * EXPERTS_PER_GROUP):
            if j == e:
                continue
            ahead = (p[j] >= p[e]) if j < e else (p[j] > p[e])
            rank = rank + ahead.astype(jnp.int32)
        sel.append((best == g) & (rank < 2))
    kept = [jnp.where(sel[e], p[e], 0.0) for e in range(N_EXPERTS)]
    denom = kept[0]
    for e in range(1, N_EXPERTS):
        denom = denom + kept[e]
    return sel, [kp / denom for kp in kept]


def _merge_kernel(x_ref, ya_ref, yb_ref, ga_ref, gb_ref, mod_ref, wpa_ref, wpb_ref, wo_ref,
                  g1_ref, b1_ref, wrh_ref, wrl_ref, br_ref, x1_ref, h2_ref, route_ref, counts_ref,
                  *, d_model, alpha):
    d = d_model
    tm = x_ref.shape[1]
    gate1 = mod_ref[0, :, 2 * d:3 * d]
    shift2 = mod_ref[0, :, 3 * d:4 * d]
    scale2 = mod_ref[0, :, 4 * d:5 * d]
    a = jnp.dot(ya_ref[0], wpa_ref[...], preferred_element_type=F32)
    b = jnp.dot(yb_ref[0], wpb_ref[...], preferred_element_type=F32)
    merged = (ga_ref[0].astype(F32) * a + gb_ref[0].astype(F32) * b).astype(BF16)
    mix = jnp.dot(merged, wo_ref[...], preferred_element_type=F32)
    x1 = _ln_rows(alpha * x_ref[0] + gate1 * mix) * g1_ref[...] + b1_ref[...]
    x1_ref[0] = x1
    h2 = _ln_rows(x1) * (1.0 + scale2) + shift2
    h2_ref[0, :, 0:d] = h2
    h2_hi, h2_lo = _split_bf16(h2)
    logits = (jnp.dot(h2_hi, wrh_ref[...], preferred_element_type=F32)
              + jnp.dot(h2_lo, wrh_ref[...], preferred_element_type=F32)
              + jnp.dot(h2_hi, wrl_ref[...], preferred_element_type=F32))
    logits_t = logits.T[0:N_EXPERTS, :] + jnp.tile(br_ref[...], (1, tm // br_ref.shape[1]))
    sel, gate_rows = _route_rows(logits_t)

    e_lo = jnp.full((1, tm), N_EXPERTS, jnp.int32)
    e_hi = jnp.full((1, tm), -1, jnp.int32)
    for e in range(N_EXPERTS):
        e_lo = jnp.where(sel[e], jnp.minimum(e_lo, e), e_lo)
        e_hi = jnp.where(sel[e], jnp.maximum(e_hi, e), e_hi)
    g_lo = jnp.zeros((1, tm), F32)
    g_hi = jnp.zeros((1, tm), F32)
    for e in range(N_EXPERTS):
        g_lo = jnp.where(e_lo == e, gate_rows[e], g_lo)
        g_hi = jnp.where(e_hi == e, gate_rows[e], g_hi)
    a = e_lo & (EXPERTS_PER_GROUP - 1)
    b = e_hi & (EXPERTS_PER_GROUP - 1)
    cls = (e_lo >> 2) * PAIRS_PER_GROUP + ((a * (7 - a)) >> 1) + (b - a - 1)
    cls_t = jnp.where(lax.broadcasted_iota(jnp.int32, (CLASS_ROWS, tm), 0) == cls, 1.0, 0.0)
    earlier = (lax.broadcasted_iota(jnp.int32, (tm, tm), 0)
               < lax.broadcasted_iota(jnp.int32, (tm, tm), 1))
    rank_t = jnp.dot(cls_t.astype(BF16), jnp.where(earlier, 1.0, 0.0).astype(BF16),
                     preferred_element_type=F32)
    rank = jnp.sum(cls_t * rank_t, axis=0, keepdims=True)
    route_ref[0] = jnp.concatenate(
        [cls, rank.astype(jnp.int32), jnp.zeros((ROUTE_ROWS - 2, tm), jnp.int32)], axis=0)
    h2_ref[0, :, d:d + GATE_LANES] = jnp.concatenate(
        [g_lo, g_hi, jnp.zeros((GATE_LANES - 2, tm), F32)], axis=0).T
    counts = jnp.sum(cls_t, axis=1, keepdims=True)
    counts_ref[0, 0] = jnp.broadcast_to(counts, (CLASS_ROWS, 128)).astype(jnp.int32)


def _merge(x, ya, yb, ga, gb, mod, w_pa, w_pb, w_o, g1, b1, w_r_hi, w_r_lo, b_r_b, *, alpha):
    bsz, n, d = x.shape
    tm = min(TOKEN_TILE, n)
    nt = n // tm
    tok = lambda b, i: (b, i, 0)
    return pl.pallas_call(
        functools.partial(_merge_kernel, d_model=d, alpha=alpha),
        grid=(bsz, nt),
        in_specs=[
            pl.BlockSpec((1, tm, d), tok),
            pl.BlockSpec((1, tm, A_WIDTH), tok),
            pl.BlockSpec((1, tm, HEAD_COLS), tok),
            pl.BlockSpec((1, tm, d), tok),
            pl.BlockSpec((1, tm, d), tok),
            pl.BlockSpec((1, 1, N_MOD * d), lambda b, i: (b, 0, 0)),
            _resident((A_WIDTH, d)),
            _resident((HEAD_COLS, d)),
            _resident((d, d)),
            _resident((1, d)),
            _resident((1, d)),
            _resident((d, GATE_LANES)),
            _resident((d, GATE_LANES)),
            _resident((N_EXPERTS, 128)),
        ],
        out_specs=[
            pl.BlockSpec((1, tm, d), tok),
            pl.BlockSpec((1, tm, d + GATE_LANES), tok),
            pl.BlockSpec((1, ROUTE_ROWS, tm), lambda b, i: (b, 0, i)),
            pl.BlockSpec((1, 1, CLASS_ROWS, 128), lambda b, i: (b, i, 0, 0)),
        ],
        out_shape=[
            jax.ShapeDtypeStruct((bsz, n, d), F32),
            jax.ShapeDtypeStruct((bsz, n, d + GATE_LANES), F32),
            jax.ShapeDtypeStruct((bsz, ROUTE_ROWS, n), jnp.int32),
            jax.ShapeDtypeStruct((bsz, nt, CLASS_ROWS, 128), jnp.int32),
        ],
        compiler_params=_params(("parallel", "parallel")),
        name="merge_route",
    )(x, ya, yb, ga, gb, mod, w_pa, w_pb, w_o, g1, b1, w_r_hi, w_r_lo, b_r_b)


def _dispatch_plan(routes, counts_list, n_tiles_max):
    tm = MOE_TOKEN_TILE
    n_srcs = [c.shape[0] * c.shape[1] for c in counts_list]
    n_src = sum(n_srcs)
    cnt = jnp.concatenate([c[..., 0].reshape(-1, CLASS_ROWS) for c in counts_list], axis=0)
    cls = jnp.concatenate([r[:, 0].reshape(-1, tm) for r in routes], axis=0)
    rank = jnp.concatenate([r[:, 1].reshape(-1, tm) for r in routes], axis=0)
    total = jnp.sum(cnt, axis=0)
    tiles_per_class = (total + MOE_ROW_TILE - 1) // MOE_ROW_TILE
    classes = jnp.arange(CLASS_ROWS, dtype=jnp.int32)
    tile_end = jnp.sum(jnp.where(classes[None, :] <= classes[:, None], tiles_per_class[None, :], 0), axis=1)
    seg_start = (tile_end - tiles_per_class) * MOE_ROW_TILE
    src = jnp.arange(n_src, dtype=jnp.int32)
    before = jnp.sum(jnp.where((src[None, :] < src[:, None])[:, :, None], cnt[None, :, :], 0), axis=1)
    base = (seg_start[None, :] + before)[:, None, :]
    pos = jnp.sum(jnp.where(cls[..., None] == classes, base, 0), axis=-1) + rank
    tile_ids = jnp.arange(n_tiles_max, dtype=jnp.int32)
    tile_class = jnp.sum((tile_end[None, :] <= tile_ids[:, None]).astype(jnp.int32), axis=1)
    tile_class = jnp.minimum(tile_class, len(PAIR_CLASSES) - 1)
    pairs = jnp.asarray(PAIR_CLASSES, dtype=jnp.int32)
    tile_pair = jnp.sum(jnp.where((tile_class[:, None] == classes[None, :len(PAIR_CLASSES)])[:, :, None],
                                  pairs[None, :, :], 0), axis=1)
    pos_tiles, start = [], 0
    for n_s in n_srcs:
        pos_tiles.append(pos[start:start + n_s].reshape(n_s, 1, tm))
        start += n_s
    return pos_tiles, tile_pair[:, 0], tile_pair[:, 1], tile_end[-1:]


def _dispatch_kernel(pos_ref, h_ref, hs_in_hbm, hs_hbm, sem, *, tile):
    del hs_in_hbm

    def row_copy(t, dst_row):
        return pltpu.make_async_copy(h_ref.at[0, pl.ds(t, 1), :], hs_hbm.at[pl.ds(dst_row, 1), :], sem)

    def issue(i, carry):
        for queue in range(DMA_QUEUES):
            t = DMA_QUEUES * i + queue
            row_copy(t, pos_ref[0, 0, t]).start(priority=queue)
        return carry
    lax.fori_loop(0, tile // DMA_QUEUES, issue, 0, unroll=4)

    def drain(t, carry):
        row_copy(0, 0).wait()
        return carry
    lax.fori_loop(0, tile, drain, 0, unroll=8)


def _dispatch(h2, pos_tiles, hs_init):
    bsz, n, width = h2.shape
    n_rows = hs_init.shape[0]
    tile = pos_tiles.shape[-1]
    nt = n // tile
    return pl.pallas_call(
        functools.partial(_dispatch_kernel, tile=tile),
        grid=(bsz, nt),
        in_specs=[
            pl.BlockSpec((1, 1, tile), lambda b, i: (b * nt + i, 0, 0), memory_space=pltpu.SMEM),
            pl.BlockSpec((1, tile, width), lambda b, i: (b, i, 0)),
            pl.BlockSpec(memory_space=pl.ANY),
        ],
        out_specs=pl.BlockSpec(memory_space=pl.ANY),
        out_shape=jax.ShapeDtypeStruct((n_rows, width), F32),
        scratch_shapes=[pltpu.SemaphoreType.DMA(())],
        input_output_aliases={2: 0},
        compiler_params=_params(("arbitrary", "arbitrary")),
        name="moe_dispatch",
    )(pos_tiles, h2, hs_init)


def _experts_kernel(lo_ref, hi_ref, nv_ref, hs_ref, w1l_ref, w3l_ref, w2l_ref, w1h_ref, w3h_ref, w2h_ref,
                    ys_ref, w1l_b, w3l_b, w2l_b, w1h_b, w3h_b, w2h_b, *, d_model):
    d = d_model
    j = pl.program_id(0)
    prev = jnp.maximum(j - 1, 0)

    @pl.when((j == 0) | (lo_ref[j] != lo_ref[prev]))
    def _():
        w1l_b[...] = w1l_ref[0, 0].astype(BF16)
        w3l_b[...] = w3l_ref[0, 0].astype(BF16)
        w2l_b[...] = w2l_ref[0, 0].astype(BF16)

    @pl.when((j == 0) | (hi_ref[j] != hi_ref[prev]))
    def _():
        w1h_b[...] = w1h_ref[0, 0].astype(BF16)
        w3h_b[...] = w3h_ref[0, 0].astype(BF16)
        w2h_b[...] = w2h_ref[0, 0].astype(BF16)

    def expert(h, w1, w3, w2):
        a = jnp.dot(h, w1[...], preferred_element_type=F32)
        b = jnp.dot(h, w3[...], preferred_element_type=F32)
        hid = (a * jax.nn.sigmoid(a) * b).astype(BF16)
        return jnp.dot(hid, w2[...], preferred_element_type=F32)

    @pl.when(j < nv_ref[0])
    def _():
        h = hs_ref[:, 0:d].astype(BF16)
        gates = hs_ref[:, d:d + GATE_LANES]
        ys_ref[...] = (gates[:, 0:1] * expert(h, w1l_b, w3l_b, w2l_b)
                       + gates[:, 1:2] * expert(h, w1h_b, w3h_b, w2h_b))

    @pl.when(j >= nv_ref[0])
    def _():
        ys_ref[...] = jnp.zeros(ys_ref.shape, F32)


def _experts(hs, tile_lo, tile_hi, n_valid, w1, w3, w2, *, layer):
    n_rows, width = hs.shape
    d = width - GATE_LANES
    de = w1.shape[-1]
    n_tiles = n_rows // MOE_ROW_TILE
    lo_map = lambda j, lo, hi, nv: (layer, lo[j], 0, 0)
    hi_map = lambda j, lo, hi, nv: (layer, hi[j], 0, 0)
    up, down = (1, 1, d, de), (1, 1, de, d)
    return pl.pallas_call(
        functools.partial(_experts_kernel, d_model=d),
        grid_spec=pltpu.PrefetchScalarGridSpec(
            num_scalar_prefetch=3,
            grid=(n_tiles,),
            in_specs=[
                pl.BlockSpec((MOE_ROW_TILE, width), lambda j, lo, hi, nv: (jnp.minimum(j, nv[0] - 1), 0)),
                pl.BlockSpec(up, lo_map), pl.BlockSpec(up, lo_map), pl.BlockSpec(down, lo_map),
                pl.BlockSpec(up, hi_map), pl.BlockSpec(up, hi_map), pl.BlockSpec(down, hi_map),
            ],
            out_specs=pl.BlockSpec((MOE_ROW_TILE, d), lambda j, lo, hi, nv: (j, 0)),
            scratch_shapes=[pltpu.VMEM((d, de), BF16), pltpu.VMEM((d, de), BF16), pltpu.VMEM((de, d), BF16)] * 2,
        ),
        out_shape=jax.ShapeDtypeStruct((n_rows, d), F32),
        compiler_params=_params(("arbitrary",)),
        name="moe_experts",
    )(tile_lo, tile_hi, n_valid, hs, w1, w3, w2, w1, w3, w2)


def _combine_kernel(pos_ref, nxt_ref, x1_ref, mod_ref, g2_ref, b2_ref, ys_hbm,
                    o_ref, ybuf, sem, *, tile, d_model, alpha):
    d = d_model
    step = pl.program_id(0) * pl.num_programs(1) + pl.program_id(1)
    n_steps = pl.num_programs(0) * pl.num_programs(1)
    slot = step % 2

    def row_copy(src_row, slot_, dst_row):
        return pltpu.make_async_copy(ys_hbm.at[pl.ds(src_row, 1), :],
                                     ybuf.at[slot_, pl.ds(dst_row, 1), :], sem.at[slot_])

    def gather(p_ref, slot_):
        def issue(i, carry):
            for queue in range(DMA_QUEUES):
                t = DMA_QUEUES * i + queue
                row_copy(p_ref[0, 0, t], slot_, t).start(priority=queue)
            return carry
        lax.fori_loop(0, tile // DMA_QUEUES, issue, 0, unroll=4)

    @pl.when(step == 0)
    def _():
        gather(pos_ref, 0)

    @pl.when(step + 1 < n_steps)
    def _():
        gather(nxt_ref, 1 - slot)

    def drain(t, carry):
        row_copy(0, slot, 0).wait()
        return carry
    lax.fori_loop(0, tile, drain, 0, unroll=8)

    gate2 = mod_ref[0, :, 5 * d:6 * d]
    o_ref[0] = _ln_rows(alpha * x1_ref[0] + gate2 * ybuf[slot]) * g2_ref[...] + b2_ref[...]


def _combine(ys, pos_tiles, x1, mod, g2, b2, *, alpha):
    bsz, n, d = x1.shape
    tile = pos_tiles.shape[-1]
    nt = n // tile
    n_steps = bsz * nt
    tok = lambda b, i: (b, i, 0)
    smem_tile = lambda index_map: pl.BlockSpec((1, 1, tile), index_map, memory_space=pltpu.SMEM)
    return pl.pallas_call(
        functools.partial(_combine_kernel, tile=tile, d_model=d, alpha=alpha),
        grid=(bsz, nt),
        in_specs=[
            smem_tile(lambda b, i: (b * nt + i, 0, 0)),
            smem_tile(lambda b, i: (jnp.minimum(b * nt + i + 1, n_steps - 1), 0, 0)),
            pl.BlockSpec((1, tile, d), tok),
            pl.BlockSpec((1, 1, N_MOD * d), lambda b, i: (b, 0, 0)),
            pl.BlockSpec((1, d), lambda b, i: (0, 0)),
            pl.BlockSpec((1, d), lambda b, i: (0, 0)),
            pl.BlockSpec(memory_space=pl.ANY),
        ],
        out_specs=pl.BlockSpec((1, tile, d), tok),
        out_shape=jax.ShapeDtypeStruct((bsz, n, d), F32),
        scratch_shapes=[pltpu.VMEM((2, tile, d), F32), pltpu.SemaphoreType.DMA((2,))],
        compiler_params=_params(("arbitrary", "arbitrary")),
        name="moe_combine",
    )(pos_tiles, pos_tiles, x1, mod, g2, b2, ys)


def _moe(streams, w1, w3, w2, g2, b2, *, alpha, layer):
    n_tokens = sum(s[3].shape[0] * s[3].shape[1] for s in streams)
    n_rows = n_tokens + len(PAIR_CLASSES) * MOE_ROW_TILE
    pos_tiles, tile_lo, tile_hi, n_valid = _dispatch_plan(
        [s[1] for s in streams], [s[2] for s in streams], n_rows // MOE_ROW_TILE)
    hs = jnp.zeros((n_rows, streams[0][0].shape[-1]), F32)
    for (h2g, _, _, _, _), pos in zip(streams, pos_tiles):
        hs = _dispatch(h2g, pos, hs)
    ys = _experts(hs, tile_lo, tile_hi, n_valid, w1, w3, w2, layer=layer)
    return [_combine(ys, pos, x1, mod, g2, b2, alpha=alpha)
            for (_, _, _, x1, mod), pos in zip(streams, pos_tiles)]


def _rope_tables(n_tokens):
    rows = n_tokens // GRID_W
    row = jnp.repeat(jnp.arange(rows, dtype=F32), GRID_W)
    col = jnp.tile(jnp.arange(GRID_W, dtype=F32), rows)
    n_freq = QK_DIM // 4
    inv_freq = ROPE_BASE ** (-jnp.arange(n_freq, dtype=F32) / n_freq)
    ang_r = row[:, None] * inv_freq
    ang_c = col[:, None] * inv_freq
    cos64 = jnp.concatenate([jnp.cos(ang_r), jnp.cos(ang_r), jnp.cos(ang_c), jnp.cos(ang_c)], axis=-1)
    sin64 = jnp.concatenate([-jnp.sin(ang_r), jnp.sin(ang_r), -jnp.sin(ang_c), jnp.sin(ang_c)], axis=-1)
    return jnp.tile(cos64, (1, 2)), jnp.tile(sin64, (1, 2))


def kernel(x, c, ctx, c_ctx, w_mod, b_mod, w_in, sgu_g, sgu_b, w_s, b_s, lambda_q, lambda_k, subln_g,
           w_pa, w_pb, w_o, ln1_g, ln1_b, w_router, b_router, w1, w3, w2, ln2_g, ln2_b):
    bsz, n_lat, d = x.shape
    depth = w_mod.shape[0]
    alpha = (2.0 * depth) ** 0.25
    cos_t, sin_t = _rope_tables(n_lat)

    cond = jnp.zeros((8, d), F32).at[0:bsz].set(c).at[bsz].set(c_ctx)
    w_r_pad = jnp.pad(w_router, ((0, 0), (0, GATE_LANES - N_EXPERTS)))
    w_r_hi, w_r_lo = _split_bf16(w_r_pad)
    b_r_b = jnp.broadcast_to(b_router[:, None], (N_EXPERTS, 128))
    row = lambda v: v.reshape(1, -1)

    for l in range(depth):
        last = l == depth - 1
        lam_init = 0.8 - 0.6 * math.exp(-0.3 * l)
        mod = _modulation(cond, w_mod[l], b_mod[l])
        mod_lat = mod[0:bsz, None, :]
        mod_ctx = jnp.broadcast_to(mod[bsz][None, None, :], (bsz, 1, N_MOD * d))
        w_in_l = w_in[l].astype(BF16)
        w_s_l = w_s[l].astype(BF16)
        b_s_b = jnp.broadcast_to(b_s[l][:, :, None], (A_GROUPS, CHUNK, CHUNK))
        proj_args = (w_in_l, row(sgu_g[l]), row(sgu_b[l]), w_s_l, b_s_b, cos_t, sin_t)
        attn_args = (lambda_q[l], lambda_k[l], row(subln_g[l]))
        merge_w = (w_pa[l].astype(BF16), w_pb[l].astype(BF16), w_o[l].astype(BF16),
                   row(ln1_g[l]), row(ln1_b[l]), w_r_hi, w_r_lo, b_r_b)
        moe_w = (w1, w3, w2, row(ln2_g[l]), row(ln2_b[l]))

        ya, qt, k, vt, ga, gb = _inproj(x, mod_lat, *proj_args, use_rope=True)
        cya, cqt, ck, cvt, cga, cgb = _inproj(ctx, mod_ctx, *proj_args, use_rope=False)
        yb = _attention(qt, k, vt, ck, cvt, *attn_args, lam_init=lam_init)
        x1, h2g, route, counts = _merge(x, ya, yb, ga, gb, mod_lat, *merge_w, alpha=alpha)
        streams = [(h2g, route, counts, x1, mod_lat)]
        if not last:
            cyb = _attention(cqt, None, None, ck, cvt, *attn_args, lam_init=lam_init)
            c1, ch2g, croute, ccounts = _merge(ctx, cya, cyb, cga, cgb, mod_ctx, *merge_w, alpha=alpha)
            streams.append((ch2g, croute, ccounts, c1, mod_ctx))
        outs = _moe(streams, *moe_w, alpha=alpha, layer=l)
        x = outs[0]
        if not last:
            ctx = outs[1]
    return x
```

```python
import functools
import math

import jax
import jax.numpy as jnp
from jax import lax
from jax.experimental import pallas as pl
from jax.experimental.pallas import tpu as pltpu

F32 = jnp.float32
BF16 = jnp.bfloat16

GRID_W = 64
CHUNK = 128
A_GROUPS = 8
A_WIDTH = 1024
HEADS = 8
QK_DIM = 64
V_DIM = 2 * QK_DIM
HEAD_COLS = HEADS * V_DIM
ROPE_BASE = 10000.0
N_EXPERTS = 16
N_GROUPS = 4
EXPERTS_PER_GROUP = N_EXPERTS // N_GROUPS
N_MOD = 6
PAIR_CLASSES = tuple((4 * g + a, 4 * g + b) for g in range(N_GROUPS)
                     for a in range(EXPERTS_PER_GROUP) for b in range(a + 1, EXPERTS_PER_GROUP))
PAIRS_PER_GROUP = len(PAIR_CLASSES) // N_GROUPS
CLASS_ROWS = 32
ROUTE_ROWS = 8
GATE_LANES = 128
LN_EPS = 1e-5
RMS_EPS = 1e-5
NEG_BIG = -1e30

VMEM_LIMIT_BYTES = 56 * 1024 * 1024

TOKEN_TILE = 256
ATTN_Q_TILE = 512
ATTN_KV_CHUNK = 512
ATTN_UNROLL = 6
SUM_ROWS = 16
MOE_TOKEN_TILE = 256
MOE_ROW_TILE = 256


def _params(semantics):
    return pltpu.CompilerParams(dimension_semantics=semantics, vmem_limit_bytes=VMEM_LIMIT_BYTES)


def _resident(shape):
    nd = len(shape)
    return pl.BlockSpec(shape, lambda *_: (0,) * nd, pipeline_mode=pl.Buffered(1))


def _split_bf16(x):
    bits = lax.bitcast_convert_type(x, jnp.uint32) & jnp.uint32(0xFFFF0000)
    hi = lax.bitcast_convert_type(bits, F32)
    return hi.astype(BF16), (x - hi).astype(BF16)


def _ln_rows(x):
    mu = jnp.mean(x, axis=-1, keepdims=True)
    xc = x - mu
    var = jnp.mean(xc * xc, axis=-1, keepdims=True)
    return xc * lax.rsqrt(var + LN_EPS)


def _mod_kernel(cond_ref, w_ref, b_ref, o_ref):
    c = cond_ref[...]
    s = c * jax.nn.sigmoid(c)
    o_ref[...] = jnp.dot(s, w_ref[...], preferred_element_type=F32, precision=lax.Precision.HIGHEST) + b_ref[...]


def _modulation(cond, w_mod, b_mod):
    rows, d = cond.shape
    return pl.pallas_call(
        _mod_kernel,
        grid=(N_MOD,),
        in_specs=[
            pl.BlockSpec((rows, d), lambda j: (0, 0)),
            pl.BlockSpec((d, d), lambda j: (0, j)),
            pl.BlockSpec((1, d), lambda j: (0, j)),
        ],
        out_specs=pl.BlockSpec((rows, d), lambda j: (0, j)),
        out_shape=jax.ShapeDtypeStruct((rows, N_MOD * d), F32),
        compiler_params=_params(("arbitrary",)),
        name="modulation",
    )(cond, w_mod, b_mod.reshape(1, -1))


def _rope(x, cos, sin_signed):
    n = x.shape[-1]
    reps = n // cos.shape[-1]
    c = jnp.tile(cos, (1, reps))
    s = jnp.tile(sin_signed, (1, reps))
    lane = lax.broadcasted_iota(jnp.int32, x.shape, 1)
    first = (lane & 31) < 16
    partner = jnp.where(first, pltpu.roll(x, n - 16, 1), pltpu.roll(x, 16, 1))
    return x * c + partner * s


def _inproj_kernel(x_ref, mod_ref, w_ref, sg_ref, sb_ref, ws_ref, bs_ref, cos_ref, sin_ref,
                   ya_ref, qt_ref, k_ref, vt_ref, ga_ref, gb_ref, *, d_model, use_rope, q_scale):
    d = d_model
    tm = x_ref.shape[1]
    shift = mod_ref[0, :, 0:d]
    scale = mod_ref[0, :, d:2 * d]
    h = (_ln_rows(x_ref[0]) * (1.0 + scale) + shift).astype(BF16)

    def proj(c0, width):
        return jnp.dot(h, w_ref[:, c0:c0 + width], preferred_element_type=F32)

    c_u, c_v = 0, A_WIDTH
    c_q = 2 * A_WIDTH
    c_k = c_q + HEAD_COLS
    c_vb = c_k + HEAD_COLS
    c_ga = c_vb + HEAD_COLS
    c_gb = c_ga + d

    u = jax.nn.gelu(proj(c_u, A_WIDTH))
    v = jax.nn.gelu(proj(c_v, A_WIDTH))
    vn = (_ln_rows(v) * sg_ref[...] + sb_ref[...]).astype(BF16)
    gd = A_WIDTH // A_GROUPS
    for pair in range(tm // (2 * CHUNK)):
        r0 = pair * 2 * CHUNK
        r1 = r0 + CHUNK
        for g in range(A_GROUPS):
            cols = slice(g * gd, (g + 1) * gd)
            rhs = jnp.concatenate([vn[r0:r0 + CHUNK, cols], vn[r1:r1 + CHUNK, cols]], axis=1)
            y = jnp.dot(ws_ref[g], rhs, preferred_element_type=F32)
            bias = bs_ref[g]
            ya_ref[0, r0:r0 + CHUNK, cols] = (u[r0:r0 + CHUNK, cols] * (y[:, :gd] + bias)).astype(BF16)
            ya_ref[0, r1:r1 + CHUNK, cols] = (u[r1:r1 + CHUNK, cols] * (y[:, gd:] + bias)).astype(BF16)

    q = proj(c_q, HEAD_COLS)
    k = proj(c_k, HEAD_COLS)
    if use_rope:
        q = _rope(q, cos_ref[...], sin_ref[...])
        k = _rope(k, cos_ref[...], sin_ref[...])
    q = q * q_scale
    k_ref[0] = k.astype(BF16)
    vb = proj(c_vb, HEAD_COLS)
    for hd in range(HEADS):
        cols = slice(hd * V_DIM, (hd + 1) * V_DIM)
        qt_ref[0, hd] = q[:, cols].T.astype(BF16)
        vt_ref[0, hd] = vb[:, cols].T.astype(BF16)

    ga_ref[0] = jax.nn.sigmoid(proj(c_ga, d)).astype(BF16)
    gb_ref[0] = jax.nn.sigmoid(proj(c_gb, d)).astype(BF16)


def _inproj(x, mod, w_in, sgu_g, sgu_b, w_s, b_s_b, cos_t, sin_t, *, use_rope):
    bsz, n, d = x.shape
    tm = min(TOKEN_TILE, n)
    nt = n // tm
    in_cols = w_in.shape[1]
    q_scale = (QK_DIM ** -0.5) * math.log2(math.e)
    tok = lambda b, i: (b, i, 0)
    tr = lambda b, i: (b, 0, 0, i)
    rope_map = (lambda b, i: (i, 0)) if use_rope else (lambda b, i: (0, 0))
    return pl.pallas_call(
        functools.partial(_inproj_kernel, d_model=d, use_rope=use_rope, q_scale=q_scale),
        grid=(bsz, nt),
        in_specs=[
            pl.BlockSpec((1, tm, d), tok),
            pl.BlockSpec((1, 1, N_MOD * d), lambda b, i: (b, 0, 0)),
            _resident((d, in_cols)),
            _resident((1, A_WIDTH)),
            _resident((1, A_WIDTH)),
            _resident((A_GROUPS, CHUNK, CHUNK)),
            _resident((A_GROUPS, CHUNK, CHUNK)),
            pl.BlockSpec((tm, 2 * QK_DIM), rope_map),
            pl.BlockSpec((tm, 2 * QK_DIM), rope_map),
        ],
        out_specs=[
            pl.BlockSpec((1, tm, A_WIDTH), tok),
            pl.BlockSpec((1, HEADS, V_DIM, tm), tr),
            pl.BlockSpec((1, tm, HEAD_COLS), tok),
            pl.BlockSpec((1, HEADS, V_DIM, tm), tr),
            pl.BlockSpec((1, tm, d), tok),
            pl.BlockSpec((1, tm, d), tok),
        ],
        out_shape=[
            jax.ShapeDtypeStruct((bsz, n, A_WIDTH), BF16),
            jax.ShapeDtypeStruct((bsz, HEADS, V_DIM, n), BF16),
            jax.ShapeDtypeStruct((bsz, n, HEAD_COLS), BF16),
            jax.ShapeDtypeStruct((bsz, HEADS, V_DIM, n), BF16),
            jax.ShapeDtypeStruct((bsz, n, d), BF16),
            jax.ShapeDtypeStruct((bsz, n, d), BF16),
        ],
        compiler_params=_params(("parallel", "parallel")),
        name="inproj_rope" if use_rope else "inproj_ctx",
    )(x, mod, w_in, sgu_g, sgu_b, w_s, b_s_b, cos_t, sin_t)


def _attn_kernel(*refs, n_lat_chunks, tk, lam_init):
    if n_lat_chunks:
        (qt_ref, kl_ref, vtl_ref, kc_ref, vtc_ref, lq_ref, lk_ref, sg_ref,
         o_ref, rhs_ref, s_ref, cm_ref, m_ref, acc_ref) = refs
    else:
        (qt_ref, kc_ref, vtc_ref, lq_ref, lk_ref, sg_ref,
         o_ref, rhs_ref, s_ref, cm_ref, m_ref, acc_ref) = refs
    tq = qt_ref.shape[-1]
    nc = kc_ref.shape[1]

    qt = qt_ref[0, 0]
    row = lax.broadcasted_iota(jnp.int32, qt.shape, 0)
    zero = jnp.zeros_like(qt)
    rhs_ref[:, 0:tq] = jnp.where(row < QK_DIM, qt, zero)
    rhs_ref[:, tq:2 * tq] = jnp.where(row >= QK_DIM, qt, zero)
    m_ref[...] = jnp.full(m_ref.shape, NEG_BIG, F32)
    acc_ref[...] = jnp.zeros(acc_ref.shape, F32)

    def stage_a(kc, slot, rows):
        s = jnp.dot(kc, rhs_ref[...], preferred_element_type=F32)
        s_ref[slot, 0:rows, :] = s
        cm_ref[slot] = jnp.max(s, axis=0, keepdims=True)

    def stage_b(vtc, slot, rows):
        m_old = m_ref[...]
        m_new = jnp.maximum(m_old, cm_ref[slot])
        alpha = jnp.exp2(m_old - m_new)
        p = jnp.exp2(s_ref[slot, 0:rows, :] - m_new)
        vt_ones = jnp.concatenate([vtc, jnp.ones((SUM_ROWS, rows), BF16)], axis=0)
        acc_ref[...] = alpha * acc_ref[...] + jnp.dot(vt_ones, p.astype(BF16), preferred_element_type=F32)
        m_ref[...] = m_new

    if n_lat_chunks:
        def lat_k(c):
            return kl_ref[0, pl.ds(pl.multiple_of(c * tk, tk), tk), :]

        def lat_vt(c):
            return vtl_ref[0, 0, :, pl.ds(pl.multiple_of(c * tk, tk), tk)]

        stage_a(lat_k(0), 0, tk)

        def pipelined(c0, n):
            for i in range(n):
                c = c0 + i
                if isinstance(c, int) and c + 1 == n_lat_chunks:
                    stage_a(kc_ref[0], (i + 1) % 2, nc)
                else:
                    stage_a(lat_k(c + 1), (i + 1) % 2, tk)
                stage_b(lat_vt(c), i % 2, tk)

        def body(j, carry):
            pipelined(ATTN_UNROLL * j, ATTN_UNROLL)
            return carry
        n_loop = (n_lat_chunks - 2) // ATTN_UNROLL
        lax.fori_loop(0, n_loop, body, 0)
        pipelined(n_loop * ATTN_UNROLL, n_lat_chunks - n_loop * ATTN_UNROLL)
        stage_b(vtc_ref[0, 0], n_lat_chunks % 2, nc)
    else:
        stage_a(kc_ref[0], 0, nc)
        stage_b(vtc_ref[0, 0], 0, nc)

    lq = lq_ref[...]
    lk = lk_ref[...]
    lam = (jnp.exp(jnp.sum(lq[0:1] * lk[0:1], keepdims=True))
           - jnp.exp(jnp.sum(lq[1:2] * lk[1:2], keepdims=True)) + lam_init)
    acc = acc_ref[0:V_DIM, :]
    l = acc_ref[V_DIM:V_DIM + 1, :]
    o = acc[:, 0:tq] / l[:, 0:tq] - lam * (acc[:, tq:2 * tq] / l[:, tq:2 * tq])
    ms = jnp.mean(o * o, axis=0, keepdims=True)
    on = o * lax.rsqrt(ms + RMS_EPS)
    o_ref[0] = (on.T * sg_ref[...] * (1.0 - lam_init)).astype(BF16)


def _attention(qt, k_lat, vt_lat, k_ctx, vt_ctx, lq, lk, subln_g, *, lam_init):
    bsz, _, _, nq = qt.shape
    nc = k_ctx.shape[1]
    tq = min(ATTN_Q_TILE, nq)
    has_lat = k_lat is not None
    tk = ATTN_KV_CHUNK
    n_lat_chunks = (k_lat.shape[1] // tk) if has_lat else 0
    assert n_lat_chunks % 2 == 0 and ATTN_UNROLL % 2 == 0
    s_rows = tk if has_lat else nc
    in_specs = [pl.BlockSpec((1, 1, V_DIM, tq), lambda b, h, i: (b, h, 0, i))]
    args = [qt]
    if has_lat:
        ns = k_lat.shape[1]
        in_specs += [pl.BlockSpec((1, ns, V_DIM), lambda b, h, i: (b, 0, h)),
                     pl.BlockSpec((1, 1, V_DIM, ns), lambda b, h, i: (b, h, 0, 0))]
        args += [k_lat, vt_lat]
    in_specs += [pl.BlockSpec((1, nc, V_DIM), lambda b, h, i: (b, 0, h)),
                 pl.BlockSpec((1, 1, V_DIM, nc), lambda b, h, i: (b, h, 0, 0)),
                 pl.BlockSpec((2, QK_DIM), lambda b, h, i: (0, 0)),
                 pl.BlockSpec((2, QK_DIM), lambda b, h, i: (0, 0)),
                 pl.BlockSpec((1, V_DIM), lambda b, h, i: (0, 0))]
    args += [k_ctx, vt_ctx, lq, lk, subln_g]
    return pl.pallas_call(
        functools.partial(_attn_kernel, n_lat_chunks=n_lat_chunks, tk=tk, lam_init=lam_init),
        grid=(bsz, HEADS, nq // tq),
        in_specs=in_specs,
        out_specs=pl.BlockSpec((1, tq, V_DIM), lambda b, h, i: (b, i, h)),
        out_shape=jax.ShapeDtypeStruct((bsz, nq, HEAD_COLS), BF16),
        scratch_shapes=[
            pltpu.VMEM((V_DIM, 2 * tq), BF16),
            pltpu.VMEM((2, s_rows, 2 * tq), F32),
            pltpu.VMEM((2, 1, 2 * tq), F32),
            pltpu.VMEM((1, 2 * tq), F32),
            pltpu.VMEM((V_DIM + SUM_ROWS, 2 * tq), F32),
        ],
        compiler_params=_params(("parallel", "parallel", "arbitrary")),
        name="diff_attn_latent" if has_lat else "diff_attn_ctx",
    )(*args)


def _route_rows(logits_t):
    mx = jnp.max(logits_t, axis=0, keepdims=True)
    ex = jnp.exp(logits_t - mx)
    probs = ex / jnp.sum(ex, axis=0, keepdims=True)
    p = [probs[e:e + 1, :] for e in range(N_EXPERTS)]
    scores = []
    for g in range(N_GROUPS):
        a, b, c, d = p[4 * g:4 * g + 4]
        hi1, lo1 = jnp.maximum(a, b), jnp.minimum(a, b)
        hi2, lo2 = jnp.maximum(c, d), jnp.minimum(c, d)
        top1 = jnp.maximum(hi1, hi2)
        top2 = jnp.maximum(jnp.minimum(hi1, hi2), jnp.maximum(lo1, lo2))
        scores.append(top1 + top2)
    best = jnp.zeros_like(scores[0], dtype=jnp.int32)
    best_score = scores[0]
    for g in range(1, N_GROUPS):
        better = scores[g] > best_score
        best = jnp.where(better, g, best)
        best_score = jnp.where(better, scores[g], best_score)
    sel = []
    for e in range(N_EXPERTS):
        g = e // EXPERTS_PER_GROUP
        rank = jnp.zeros_like(best)
        for j in range(g * EXPERTS_PER_GROUP, (g + 1) * EXPERTS_PER_GROUP):
            if j == e:
                continue
            ahead = (p[j] >= p[e]) if j < e else (p[j] > p[e])
            rank = rank + ahead.astype(jnp.int32)
        sel.append((best == g) & (rank < 2))
    kept = [jnp.where(sel[e], p[e], 0.0) for e in range(N_EXPERTS)]
    denom = kept[0]
    for e in range(1, N_EXPERTS):
        denom = denom + kept[e]
    return sel, [kp / denom for kp in kept]


def _merge_kernel(x_ref, ya_ref, yb_ref, ga_ref, gb_ref, mod_ref, wpa_ref, wpb_ref, wo_ref,
                  g1_ref, b1_ref, wrh_ref, wrl_ref, br_ref, x1_ref, h2_ref, route_ref, counts_ref,
                  *, d_model, alpha):
    d = d_model
    tm = x_ref.shape[1]
    gate1 = mod_ref[0, :, 2 * d:3 * d]
    shift2 = mod_ref[0, :, 3 * d:4 * d]
    scale2 = mod_ref[0, :, 4 * d:5 * d]
    a = jnp.dot(ya_ref[0], wpa_ref[...], preferred_element_type=F32)
    b = jnp.dot(yb_ref[0], wpb_ref[...], preferred_element_type=F32)
    merged = (ga_ref[0].astype(F32) * a + gb_ref[0].astype(F32) * b).astype(BF16)
    mix = jnp.dot(merged, wo_ref[...], preferred_element_type=F32)
    x1 = _ln_rows(alpha * x_ref[0] + gate1 * mix) * g1_ref[...] + b1_ref[...]
    x1_ref[0] = x1
    h2 = _ln_rows(x1) * (1.0 + scale2) + shift2
    h2_ref[0, :, 0:d] = h2
    h2_hi, h2_lo = _split_bf16(h2)
    logits = (jnp.dot(h2_hi, wrh_ref[...], preferred_element_type=F32)
              + jnp.dot(h2_lo, wrh_ref[...], preferred_element_type=F32)
              + jnp.dot(h2_hi, wrl_ref[...], preferred_element_type=F32))
    logits_t = logits.T[0:N_EXPERTS, :] + jnp.tile(br_ref[...], (1, tm // br_ref.shape[1]))
    sel, gate_rows = _route_rows(logits_t)

    e_lo = jnp.full((1, tm), N_EXPERTS, jnp.int32)
    e_hi = jnp.full((1, tm), -1, jnp.int32)
    for e in range(N_EXPERTS):
        e_lo = jnp.where(sel[e], jnp.minimum(e_lo, e), e_lo)
        e_hi = jnp.where(sel[e], jnp.maximum(e_hi, e), e_hi)
    g_lo = jnp.zeros((1, tm), F32)
    g_hi = jnp.zeros((1, tm), F32)
    for e in range(N_EXPERTS):
        g_lo = jnp.where(e_lo == e, gate_rows[e], g_lo)
        g_hi = jnp.where(e_hi == e, gate_rows[e], g_hi)
    a = e_lo & (EXPERTS_PER_GROUP - 1)
    b = e_hi & (EXPERTS_PER_GROUP - 1)
    cls = (e_lo >> 2) * PAIRS_PER_GROUP + ((a * (7 - a)) >> 1) + (b - a - 1)
    cls_t = jnp.where(lax.broadcasted_iota(jnp.int32, (CLASS_ROWS, tm), 0) == cls, 1.0, 0.0)
    earlier = (lax.broadcasted_iota(jnp.int32, (tm, tm), 0)
               < lax.broadcasted_iota(jnp.int32, (tm, tm), 1))
    rank_t = jnp.dot(cls_t.astype(BF16), jnp.where(earlier, 1.0, 0.0).astype(BF16),
                     preferred_element_type=F32)
    rank = jnp.sum(cls_t * rank_t, axis=0, keepdims=True)
    route_ref[0] = jnp.concatenate(
        [cls, rank.astype(jnp.int32), jnp.zeros((ROUTE_ROWS - 2, tm), jnp.int32)], axis=0)
    h2_ref[0, :, d:d + GATE_LANES] = jnp.concatenate(
        [g_lo, g_hi, jnp.zeros((GATE_LANES - 2, tm), F32)], axis=0).T
    counts = jnp.sum(cls_t, axis=1, keepdims=True)
    counts_ref[0, 0] = jnp.broadcast_to(counts, (CLASS_ROWS, 128)).astype(jnp.int32)


def _merge(x, ya, yb, ga, gb, mod, w_pa, w_pb, w_o, g1, b1, w_r_hi, w_r_lo, b_r_b, *, alpha):
    bsz, n, d = x.shape
    tm = min(TOKEN_TILE, n)
    nt = n // tm
    tok = lambda b, i: (b, i, 0)
    return pl.pallas_call(
        functools.partial(_merge_kernel, d_model=d, alpha=alpha),
        grid=(bsz, nt),
        in_specs=[
            pl.BlockSpec((1, tm, d), tok),
            pl.BlockSpec((1, tm, A_WIDTH), tok),
            pl.BlockSpec((1, tm, HEAD_COLS), tok),
            pl.BlockSpec((1, tm, d), tok),
            pl.BlockSpec((1, tm, d), tok),
            pl.BlockSpec((1, 1, N_MOD * d), lambda b, i: (b, 0, 0)),
            _resident((A_WIDTH, d)),
            _resident((HEAD_COLS, d)),
            _resident((d, d)),
            _resident((1, d)),
            _resident((1, d)),
            _resident((d, GATE_LANES)),
            _resident((d, GATE_LANES)),
            _resident((N_EXPERTS, 128)),
        ],
        out_specs=[
            pl.BlockSpec((1, tm, d), tok),
            pl.BlockSpec((1, tm, d + GATE_LANES), tok),
            pl.BlockSpec((1, ROUTE_ROWS, tm), lambda b, i: (b, 0, i)),
            pl.BlockSpec((1, 1, CLASS_ROWS, 128), lambda b, i: (b, i, 0, 0)),
        ],
        out_shape=[
            jax.ShapeDtypeStruct((bsz, n, d), F32),
            jax.ShapeDtypeStruct((bsz, n, d + GATE_LANES), F32),
            jax.ShapeDtypeStruct((bsz, ROUTE_ROWS, n), jnp.int32),
            jax.ShapeDtypeStruct((bsz, nt, CLASS_ROWS, 128), jnp.int32),
        ],
        compiler_params=_params(("parallel", "parallel")),
        name="merge_route",
    )(x, ya, yb, ga, gb, mod, w_pa, w_pb, w_o, g1, b1, w_r_hi, w_r_lo, b_r_b)


def _dispatch_plan(routes, counts_list, n_tiles_max):
    tm = MOE_TOKEN_TILE
    n_srcs = [c.shape[0] * c.shape[1] for c in counts_list]
    n_src = sum(n_srcs)
    cnt = jnp.concatenate([c[..., 0].reshape(-1, CLASS_ROWS) for c in counts_list], axis=0)
    cls = jnp.concatenate([r[:, 0].reshape(-1, tm) for r in routes], axis=0)
    rank = jnp.concatenate([r[:, 1].reshape(-1, tm) for r in routes], axis=0)
    total = jnp.sum(cnt, axis=0)
    tiles_per_class = (total + MOE_ROW_TILE - 1) // MOE_ROW_TILE
    classes = jnp.arange(CLASS_ROWS, dtype=jnp.int32)
    tile_end = jnp.sum(jnp.where(classes[None, :] <= classes[:, None], tiles_per_class[None, :], 0), axis=1)
    seg_start = (tile_end - tiles_per_class) * MOE_ROW_TILE
    src = jnp.arange(n_src, dtype=jnp.int32)
    before = jnp.sum(jnp.where((src[None, :] < src[:, None])[:, :, None], cnt[None, :, :], 0), axis=1)
    base = (seg_start[None, :] + before)[:, None, :]
    pos = jnp.sum(jnp.where(cls[..., None] == classes, base, 0), axis=-1) + rank
    tile_ids = jnp.arange(n_tiles_max, dtype=jnp.int32)
    tile_class = jnp.sum((tile_end[None, :] <= tile_ids[:, None]).astype(jnp.int32), axis=1)
    tile_class = jnp.minimum(tile_class, len(PAIR_CLASSES) - 1)
    pairs = jnp.asarray(PAIR_CLASSES, dtype=jnp.int32)
    tile_pair = jnp.sum(jnp.where((tile_class[:, None] == classes[None, :len(PAIR_CLASSES)])[:, :, None],
                                  pairs[None, :, :], 0), axis=1)
    pos_tiles, start = [], 0
    for n_s in n_srcs:
        pos_tiles.append(pos[start:start + n_s].reshape(n_s, 1, tm))
        start += n_s
    return pos_tiles, tile_pair[:, 0], tile_pair[:, 1], tile_end[-1:]


def _dispatch_kernel(pos_ref, h_ref, hs_in_hbm, hs_hbm, sem, *, tile):
    del hs_in_hbm

    def row_copy(t, dst_row):
        return pltpu.make_async_copy(h_ref.at[0, pl.ds(t, 1), :], hs_hbm.at[pl.ds(dst_row, 1), :], sem)

    def issue(t, carry):
        row_copy(t, pos_ref[0, 0, t]).start()
        return carry
    lax.fori_loop(0, tile, issue, 0, unroll=8)

    def drain(t, carry):
        row_copy(0, 0).wait()
        return carry
    lax.fori_loop(0, tile, drain, 0, unroll=8)


def _dispatch(h2, pos_tiles, hs_init):
    bsz, n, width = h2.shape
    n_rows = hs_init.shape[0]
    tile = pos_tiles.shape[-1]
    nt = n // tile
    return pl.pallas_call(
        functools.partial(_dispatch_kernel, tile=tile),
        grid=(bsz, nt),
        in_specs=[
            pl.BlockSpec((1, 1, tile), lambda b, i: (b * nt + i, 0, 0), memory_space=pltpu.SMEM),
            pl.BlockSpec((1, tile, width), lambda b, i: (b, i, 0)),
            pl.BlockSpec(memory_space=pl.ANY),
        ],
        out_specs=pl.BlockSpec(memory_space=pl.ANY),
        out_shape=jax.ShapeDtypeStruct((n_rows, width), F32),
        scratch_shapes=[pltpu.SemaphoreType.DMA(())],
        input_output_aliases={2: 0},
        compiler_params=_params(("arbitrary", "arbitrary")),
        name="moe_dispatch",
    )(pos_tiles, h2, hs_init)


def _experts_kernel(lo_ref, hi_ref, nv_ref, hs_ref, w1l_ref, w3l_ref, w2l_ref, w1h_ref, w3h_ref, w2h_ref,
                    ys_ref, w1l_b, w3l_b, w2l_b, w1h_b, w3h_b, w2h_b, *, d_model):
    d = d_model
    j = pl.program_id(0)
    prev = jnp.maximum(j - 1, 0)

    @pl.when((j == 0) | (lo_ref[j] != lo_ref[prev]))
    def _():
        w1l_b[...] = w1l_ref[0, 0].astype(BF16)
        w3l_b[...] = w3l_ref[0, 0].astype(BF16)
        w2l_b[...] = w2l_ref[0, 0].astype(BF16)

    @pl.when((j == 0) | (hi_ref[j] != hi_ref[prev]))
    def _():
        w1h_b[...] = w1h_ref[0, 0].astype(BF16)
        w3h_b[...] = w3h_ref[0, 0].astype(BF16)
        w2h_b[...] = w2h_ref[0, 0].astype(BF16)

    def expert(h, w1, w3, w2):
        a = jnp.dot(h, w1[...], preferred_element_type=F32)
        b = jnp.dot(h, w3[...], preferred_element_type=F32)
        hid = (a * jax.nn.sigmoid(a) * b).astype(BF16)
        return jnp.dot(hid, w2[...], preferred_element_type=F32)

    @pl.when(j < nv_ref[0])
    def _():
        h = hs_ref[:, 0:d].astype(BF16)
        gates = hs_ref[:, d:d + GATE_LANES]
        ys_ref[...] = (gates[:, 0:1] * expert(h, w1l_b, w3l_b, w2l_b)
                       + gates[:, 1:2] * expert(h, w1h_b, w3h_b, w2h_b))

    @pl.when(j >= nv_ref[0])
    def _():
        ys_ref[...] = jnp.zeros(ys_ref.shape, F32)


def _experts(hs, tile_lo, tile_hi, n_valid, w1, w3, w2, *, layer):
    n_rows, width = hs.shape
    d = width - GATE_LANES
    de = w1.shape[-1]
    n_tiles = n_rows // MOE_ROW_TILE
    lo_map = lambda j, lo, hi, nv: (layer, lo[j], 0, 0)
    hi_map = lambda j, lo, hi, nv: (layer, hi[j], 0, 0)
    up, down = (1, 1, d, de), (1, 1, de, d)
    return pl.pallas_call(
        functools.partial(_experts_kernel, d_model=d),
        grid_spec=pltpu.PrefetchScalarGridSpec(
            num_scalar_prefetch=3,
            grid=(n_tiles,),
            in_specs=[
                pl.BlockSpec((MOE_ROW_TILE, width), lambda j, lo, hi, nv: (jnp.minimum(j, nv[0] - 1), 0)),
                pl.BlockSpec(up, lo_map), pl.BlockSpec(up, lo_map), pl.BlockSpec(down, lo_map),
                pl.BlockSpec(up, hi_map), pl.BlockSpec(up, hi_map), pl.BlockSpec(down, hi_map),
            ],
            out_specs=pl.BlockSpec((MOE_ROW_TILE, d), lambda j, lo, hi, nv: (j, 0)),
            scratch_shapes=[pltpu.VMEM((d, de), BF16), pltpu.VMEM((d, de), BF16), pltpu.VMEM((de, d), BF16)] * 2,
        ),
        out_shape=jax.ShapeDtypeStruct((n_rows, d), F32),
        compiler_params=_params(("arbitrary",)),
        name="moe_experts",
    )(tile_lo, tile_hi, n_valid, hs, w1, w3, w2, w1, w3, w2)


def _combine_kernel(pos_ref, nxt_ref, x1_ref, mod_ref, g2_ref, b2_ref, ys_hbm,
                    o_ref, ybuf, sem, *, tile, d_model, alpha):
    d = d_model
    step = pl.program_id(0) * pl.num_programs(1) + pl.program_id(1)
    n_steps = pl.num_programs(0) * pl.num_programs(1)
    slot = step % 2

    def row_copy(src_row, slot_, dst_row):
        return pltpu.make_async_copy(ys_hbm.at[pl.ds(src_row, 1), :],
                                     ybuf.at[slot_, pl.ds(dst_row, 1), :], sem.at[slot_])

    def gather(p_ref, slot_):
        def issue(t, carry):
            row_copy(p_ref[0, 0, t], slot_, t).start()
            return carry
        lax.fori_loop(0, tile, issue, 0, unroll=8)

    @pl.when(step == 0)
    def _():
        gather(pos_ref, 0)

    @pl.when(step + 1 < n_steps)
    def _():
        gather(nxt_ref, 1 - slot)

    def drain(t, carry):
        row_copy(0, slot, 0).wait()
        return carry
    lax.fori_loop(0, tile, drain, 0, unroll=8)

    gate2 = mod_ref[0, :, 5 * d:6 * d]
    o_ref[0] = _ln_rows(alpha * x1_ref[0] + gate2 * ybuf[slot]) * g2_ref[...] + b2_ref[...]


def _combine(ys, pos_tiles, x1, mod, g2, b2, *, alpha):
    bsz, n, d = x1.shape
    tile = pos_tiles.shape[-1]
    nt = n // tile
    n_steps = bsz * nt
    tok = lambda b, i: (b, i, 0)
    smem_tile = lambda index_map: pl.BlockSpec((1, 1, tile), index_map, memory_space=pltpu.SMEM)
    return pl.pallas_call(
        functools.partial(_combine_kernel, tile=tile, d_model=d, alpha=alpha),
        grid=(bsz, nt),
        in_specs=[
            smem_tile(lambda b, i: (b * nt + i, 0, 0)),
            smem_tile(lambda b, i: (jnp.minimum(b * nt + i + 1, n_steps - 1), 0, 0)),
            pl.BlockSpec((1, tile, d), tok),
            pl.BlockSpec((1, 1, N_MOD * d), lambda b, i: (b, 0, 0)),
            pl.BlockSpec((1, d), lambda b, i: (0, 0)),
            pl.BlockSpec((1, d), lambda b, i: (0, 0)),
            pl.BlockSpec(memory_space=pl.ANY),
        ],
        out_specs=pl.BlockSpec((1, tile, d), tok),
        out_shape=jax.ShapeDtypeStruct((bsz, n, d), F32),
        scratch_shapes=[pltpu.VMEM((2, tile, d), F32), pltpu.SemaphoreType.DMA((2,))],
        compiler_params=_params(("arbitrary", "arbitrary")),
        name="moe_combine",
    )(pos_tiles, pos_tiles, x1, mod, g2, b2, ys)


def _moe(streams, hs, w1, w3, w2, g2, b2, *, alpha, layer):
    pos_tiles, tile_lo, tile_hi, n_valid = _dispatch_plan(
        [s[1] for s in streams], [s[2] for s in streams], hs.shape[0] // MOE_ROW_TILE)
    for (h2g, _, _, _, _), pos in zip(streams, pos_tiles):
        hs = _dispatch(h2g, pos, hs)
    ys = _experts(hs, tile_lo, tile_hi, n_valid, w1, w3, w2, layer=layer)
    outs = [_combine(ys, pos, x1, mod, g2, b2, alpha=alpha)
            for (_, _, _, x1, mod), pos in zip(streams, pos_tiles)]
    return outs, hs


def _rope_tables(n_tokens):
    rows = n_tokens // GRID_W
    row = jnp.repeat(jnp.arange(rows, dtype=F32), GRID_W)
    col = jnp.tile(jnp.arange(GRID_W, dtype=F32), rows)
    n_freq = QK_DIM // 4
    inv_freq = ROPE_BASE ** (-jnp.arange(n_freq, dtype=F32) / n_freq)
    ang_r = row[:, None] * inv_freq
    ang_c = col[:, None] * inv_freq
    cos64 = jnp.concatenate([jnp.cos(ang_r), jnp.cos(ang_r), jnp.cos(ang_c), jnp.cos(ang_c)], axis=-1)
    sin64 = jnp.concatenate([-jnp.sin(ang_r), jnp.sin(ang_r), -jnp.sin(ang_c), jnp.sin(ang_c)], axis=-1)
    return jnp.tile(cos64, (1, 2)), jnp.tile(sin64, (1, 2))


def kernel(x, c, ctx, c_ctx, w_mod, b_mod, w_in, sgu_g, sgu_b, w_s, b_s, lambda_q, lambda_k, subln_g,
           w_pa, w_pb, w_o, ln1_g, ln1_b, w_router, b_router, w1, w3, w2, ln2_g, ln2_b):
    bsz, n_lat, d = x.shape
    depth = w_mod.shape[0]
    alpha = (2.0 * depth) ** 0.25
    cos_t, sin_t = _rope_tables(n_lat)

    cond = jnp.zeros((8, d), F32).at[0:bsz].set(c).at[bsz].set(c_ctx)
    w_r_pad = jnp.pad(w_router, ((0, 0), (0, GATE_LANES - N_EXPERTS)))
    w_r_hi, w_r_lo = _split_bf16(w_r_pad)
    b_r_b = jnp.broadcast_to(b_router[:, None], (N_EXPERTS, 128))
    row = lambda v: v.reshape(1, -1)
    n_moe_rows = bsz * (n_lat + ctx.shape[1]) + len(PAIR_CLASSES) * MOE_ROW_TILE
    moe_rows = jnp.zeros((n_moe_rows, d + GATE_LANES), F32)

    for l in range(depth):
        last = l == depth - 1
        lam_init = 0.8 - 0.6 * math.exp(-0.3 * l)
        mod = _modulation(cond, w_mod[l], b_mod[l])
        mod_lat = mod[0:bsz, None, :]
        mod_ctx = jnp.broadcast_to(mod[bsz][None, None, :], (bsz, 1, N_MOD * d))
        w_in_l = w_in[l].astype(BF16)
        w_s_l = w_s[l].astype(BF16)
        b_s_b = jnp.broadcast_to(b_s[l][:, :, None], (A_GROUPS, CHUNK, CHUNK))
        proj_args = (w_in_l, row(sgu_g[l]), row(sgu_b[l]), w_s_l, b_s_b, cos_t, sin_t)
        attn_args = (lambda_q[l], lambda_k[l], row(subln_g[l]))
        merge_w = (w_pa[l].astype(BF16), w_pb[l].astype(BF16), w_o[l].astype(BF16),
                   row(ln1_g[l]), row(ln1_b[l]), w_r_hi, w_r_lo, b_r_b)
        moe_w = (w1, w3, w2, row(ln2_g[l]), row(ln2_b[l]))

        ya, qt, k, vt, ga, gb = _inproj(x, mod_lat, *proj_args, use_rope=True)
        cya, cqt, ck, cvt, cga, cgb = _inproj(ctx, mod_ctx, *proj_args, use_rope=False)
        yb = _attention(qt, k, vt, ck, cvt, *attn_args, lam_init=lam_init)
        x1, h2g, route, counts = _merge(x, ya, yb, ga, gb, mod_lat, *merge_w, alpha=alpha)
        streams = [(h2g, route, counts, x1, mod_lat)]
        if not last:
            cyb = _attention(cqt, None, None, ck, cvt, *attn_args, lam_init=lam_init)
            c1, ch2g, croute, ccounts = _merge(ctx, cya, cyb, cga, cgb, mod_ctx, *merge_w, alpha=alpha)
            streams.append((ch2g, croute, ccounts, c1, mod_ctx))
        outs, moe_rows = _moe(streams, moe_rows, *moe_w, alpha=alpha, layer=l)
        x = outs[0]
        if not last:
            ctx = outs[1]
    return x
```

```python
import functools
import math

import jax
import jax.numpy as jnp
from jax import lax
from jax.experimental import pallas as pl
from jax.experimental.pallas import tpu as pltpu

F32 = jnp.float32
BF16 = jnp.bfloat16

GRID_W = 64
CHUNK = 128
A_GROUPS = 8
A_WIDTH = 1024
HEADS = 8
QK_DIM = 64
V_DIM = 2 * QK_DIM
HEAD_COLS = HEADS * V_DIM
ROPE_BASE = 10000.0
N_EXPERTS = 16
N_GROUPS = 4
EXPERTS_PER_GROUP = N_EXPERTS // N_GROUPS
N_MOD = 6
PAIR_CLASSES = tuple((4 * g + a, 4 * g + b) for g in range(N_GROUPS)
                     for a in range(EXPERTS_PER_GROUP) for b in range(a + 1, EXPERTS_PER_GROUP))
PAIRS_PER_GROUP = len(PAIR_CLASSES) // N_GROUPS
CLASS_ROWS = 32
ROUTE_ROWS = 8
GATE_LANES = 128
LN_EPS = 1e-5
RMS_EPS = 1e-5
NEG_BIG = -1e30

VMEM_LIMIT_BYTES = 56 * 1024 * 1024

TOKEN_TILE = 256
ATTN_Q_TILE = 512
ATTN_KV_CHUNK = 512
ATTN_UNROLL = 6
SUM_ROWS = 16
MOE_TOKEN_TILE = TOKEN_TILE
MOE_ROW_TILE = 256


def _params(semantics):
    return pltpu.CompilerParams(dimension_semantics=semantics, vmem_limit_bytes=VMEM_LIMIT_BYTES)


def _resident(shape):
    nd = len(shape)
    return pl.BlockSpec(shape, lambda *_: (0,) * nd, pipeline_mode=pl.Buffered(1))


def _split_bf16(x):
    bits = lax.bitcast_convert_type(x, jnp.uint32) & jnp.uint32(0xFFFF0000)
    hi = lax.bitcast_convert_type(bits, F32)
    return hi.astype(BF16), (x - hi).astype(BF16)


def _ln_rows(x):
    mu = jnp.mean(x, axis=-1, keepdims=True)
    xc = x - mu
    var = jnp.mean(xc * xc, axis=-1, keepdims=True)
    return xc * lax.rsqrt(var + LN_EPS)


def _mod_kernel(cond_ref, w_ref, b_ref, o_ref):
    c = cond_ref[...]
    s = c * jax.nn.sigmoid(c)
    o_ref[...] = jnp.dot(s, w_ref[...], preferred_element_type=F32, precision=lax.Precision.HIGHEST) + b_ref[...]


def _modulation(cond, w_mod, b_mod):
    rows, d = cond.shape
    return pl.pallas_call(
        _mod_kernel,
        grid=(N_MOD,),
        in_specs=[
            pl.BlockSpec((rows, d), lambda j: (0, 0)),
            pl.BlockSpec((d, d), lambda j: (0, j)),
            pl.BlockSpec((1, d), lambda j: (0, j)),
        ],
        out_specs=pl.BlockSpec((rows, d), lambda j: (0, j)),
        out_shape=jax.ShapeDtypeStruct((rows, N_MOD * d), F32),
        compiler_params=_params(("arbitrary",)),
        name="modulation",
    )(cond, w_mod, b_mod.reshape(1, -1))


def _rope(x, cos, sin_signed):
    n = x.shape[-1]
    reps = n // cos.shape[-1]
    c = jnp.tile(cos, (1, reps))
    s = jnp.tile(sin_signed, (1, reps))
    lane = lax.broadcasted_iota(jnp.int32, x.shape, 1)
    first = (lane & 31) < 16
    partner = jnp.where(first, pltpu.roll(x, n - 16, 1), pltpu.roll(x, 16, 1))
    return x * c + partner * s


def _inproj_kernel(x_ref, mod_ref, w_ref, sg_ref, sb_ref, ws_ref, bs_ref, cos_ref, sin_ref,
                   ya_ref, qt_ref, k_ref, vt_ref, ga_ref, gb_ref, *, d_model, use_rope, q_scale):
    d = d_model
    tm = x_ref.shape[1]
    shift = mod_ref[0, :, 0:d]
    scale = mod_ref[0, :, d:2 * d]
    h = (_ln_rows(x_ref[0]) * (1.0 + scale) + shift).astype(BF16)

    def proj(c0, width):
        return jnp.dot(h, w_ref[:, c0:c0 + width], preferred_element_type=F32)

    c_u, c_v = 0, A_WIDTH
    c_q = 2 * A_WIDTH
    c_k = c_q + HEAD_COLS
    c_vb = c_k + HEAD_COLS
    c_ga = c_vb + HEAD_COLS
    c_gb = c_ga + d

    u = jax.nn.gelu(proj(c_u, A_WIDTH))
    v = jax.nn.gelu(proj(c_v, A_WIDTH))
    vn = (_ln_rows(v) * sg_ref[...] + sb_ref[...]).astype(BF16)
    gd = A_WIDTH // A_GROUPS
    for pair in range(tm // (2 * CHUNK)):
        r0 = pair * 2 * CHUNK
        r1 = r0 + CHUNK
        for g in range(A_GROUPS):
            cols = slice(g * gd, (g + 1) * gd)
            rhs = jnp.concatenate([vn[r0:r0 + CHUNK, cols], vn[r1:r1 + CHUNK, cols]], axis=1)
            y = jnp.dot(ws_ref[g], rhs, preferred_element_type=F32)
            bias = bs_ref[g]
            ya_ref[0, r0:r0 + CHUNK, cols] = (u[r0:r0 + CHUNK, cols] * (y[:, :gd] + bias)).astype(BF16)
            ya_ref[0, r1:r1 + CHUNK, cols] = (u[r1:r1 + CHUNK, cols] * (y[:, gd:] + bias)).astype(BF16)

    q = proj(c_q, HEAD_COLS)
    k = proj(c_k, HEAD_COLS)
    if use_rope:
        q = _rope(q, cos_ref[...], sin_ref[...])
        k = _rope(k, cos_ref[...], sin_ref[...])
    q = q * q_scale
    k_ref[0] = k.astype(BF16)
    vb = proj(c_vb, HEAD_COLS)
    for hd in range(HEADS):
        cols = slice(hd * V_DIM, (hd + 1) * V_DIM)
        qt_ref[0, hd] = q[:, cols].T.astype(BF16)
        vt_ref[0, hd] = vb[:, cols].T.astype(BF16)

    ga_ref[0] = jax.nn.sigmoid(proj(c_ga, d)).astype(BF16)
    gb_ref[0] = jax.nn.sigmoid(proj(c_gb, d)).astype(BF16)


def _inproj(x, mod, w_in, sgu_g, sgu_b, w_s, b_s_b, cos_t, sin_t, *, use_rope):
    bsz, n, d = x.shape
    tm = min(TOKEN_TILE, n)
    nt = n // tm
    in_cols = w_in.shape[1]
    q_scale = (QK_DIM ** -0.5) * math.log2(math.e)
    tok = lambda b, i: (b, i, 0)
    tr = lambda b, i: (b, 0, 0, i)
    rope_map = (lambda b, i: (i, 0)) if use_rope else (lambda b, i: (0, 0))
    return pl.pallas_call(
        functools.partial(_inproj_kernel, d_model=d, use_rope=use_rope, q_scale=q_scale),
        grid=(bsz, nt),
        in_specs=[
            pl.BlockSpec((1, tm, d), tok),
            pl.BlockSpec((1, 1, N_MOD * d), lambda b, i: (b, 0, 0)),
            _resident((d, in_cols)),
            _resident((1, A_WIDTH)),
            _resident((1, A_WIDTH)),
            _resident((A_GROUPS, CHUNK, CHUNK)),
            _resident((A_GROUPS, CHUNK, CHUNK)),
            pl.BlockSpec((tm, 2 * QK_DIM), rope_map),
            pl.BlockSpec((tm, 2 * QK_DIM), rope_map),
        ],
        out_specs=[
            pl.BlockSpec((1, tm, A_WIDTH), tok),
            pl.BlockSpec((1, HEADS, V_DIM, tm), tr),
            pl.BlockSpec((1, tm, HEAD_COLS), tok),
            pl.BlockSpec((1, HEADS, V_DIM, tm), tr),
            pl.BlockSpec((1, tm, d), tok),
            pl.BlockSpec((1, tm, d), tok),
        ],
        out_shape=[
            jax.ShapeDtypeStruct((bsz, n, A_WIDTH), BF16),
            jax.ShapeDtypeStruct((bsz, HEADS, V_DIM, n), BF16),
            jax.ShapeDtypeStruct((bsz, n, HEAD_COLS), BF16),
            jax.ShapeDtypeStruct((bsz, HEADS, V_DIM, n), BF16),
            jax.ShapeDtypeStruct((bsz, n, d), BF16),
            jax.ShapeDtypeStruct((bsz, n, d), BF16),
        ],
        compiler_params=_params(("parallel", "parallel")),
        name="inproj_rope" if use_rope else "inproj_ctx",
    )(x, mod, w_in, sgu_g, sgu_b, w_s, b_s_b, cos_t, sin_t)


def _attn_kernel(*refs, n_lat_chunks, tk, lam_init):
    if n_lat_chunks:
        (qt_ref, kl_ref, vtl_ref, kc_ref, vtc_ref, lq_ref, lk_ref, sg_ref,
         o_ref, rhs_ref, s_ref, cm_ref, m_ref, acc_ref) = refs
    else:
        (qt_ref, kc_ref, vtc_ref, lq_ref, lk_ref, sg_ref,
         o_ref, rhs_ref, s_ref, cm_ref, m_ref, acc_ref) = refs
    tq = qt_ref.shape[-1]
    nc = kc_ref.shape[1]

    qt = qt_ref[0, 0]
    row = lax.broadcasted_iota(jnp.int32, qt.shape, 0)
    zero = jnp.zeros_like(qt)
    rhs_ref[:, 0:tq] = jnp.where(row < QK_DIM, qt, zero)
    rhs_ref[:, tq:2 * tq] = jnp.where(row >= QK_DIM, qt, zero)
    m_ref[...] = jnp.full(m_ref.shape, NEG_BIG, F32)
    acc_ref[...] = jnp.zeros(acc_ref.shape, F32)

    def stage_a(kc, slot, rows):
        s = jnp.dot(kc, rhs_ref[...], preferred_element_type=F32)
        s_ref[slot, 0:rows, :] = s
        cm_ref[slot] = jnp.max(s, axis=0, keepdims=True)

    def stage_b(vtc, slot, rows):
        m_old = m_ref[...]
        m_new = jnp.maximum(m_old, cm_ref[slot])
        alpha = jnp.exp2(m_old - m_new)
        p = jnp.exp2(s_ref[slot, 0:rows, :] - m_new)
        vt_ones = jnp.concatenate([vtc, jnp.ones((SUM_ROWS, rows), BF16)], axis=0)
        acc_ref[...] = alpha * acc_ref[...] + jnp.dot(vt_ones, p.astype(BF16), preferred_element_type=F32)
        m_ref[...] = m_new

    if n_lat_chunks:
        def lat_k(c):
            return kl_ref[0, pl.ds(pl.multiple_of(c * tk, tk), tk), :]

        def lat_vt(c):
            return vtl_ref[0, 0, :, pl.ds(pl.multiple_of(c * tk, tk), tk)]

        stage_a(lat_k(0), 0, tk)

        def pipelined(c0, n):
            for i in range(n):
                c = c0 + i
                if isinstance(c, int) and c + 1 == n_lat_chunks:
                    stage_a(kc_ref[0], (i + 1) % 2, nc)
                else:
                    stage_a(lat_k(c + 1), (i + 1) % 2, tk)
                stage_b(lat_vt(c), i % 2, tk)

        def body(j, carry):
            pipelined(ATTN_UNROLL * j, ATTN_UNROLL)
            return carry
        n_loop = (n_lat_chunks - 2) // ATTN_UNROLL
        lax.fori_loop(0, n_loop, body, 0)
        pipelined(n_loop * ATTN_UNROLL, n_lat_chunks - n_loop * ATTN_UNROLL)
        stage_b(vtc_ref[0, 0], n_lat_chunks % 2, nc)
    else:
        stage_a(kc_ref[0], 0, nc)
        stage_b(vtc_ref[0, 0], 0, nc)

    lq = lq_ref[...]
    lk = lk_ref[...]
    lam = (jnp.exp(jnp.sum(lq[0:1] * lk[0:1], keepdims=True))
           - jnp.exp(jnp.sum(lq[1:2] * lk[1:2], keepdims=True)) + lam_init)
    acc = acc_ref[0:V_DIM, :]
    l = acc_ref[V_DIM:V_DIM + 1, :]
    o = acc[:, 0:tq] / l[:, 0:tq] - lam * (acc[:, tq:2 * tq] / l[:, tq:2 * tq])
    ms = jnp.mean(o * o, axis=0, keepdims=True)
    on = o * lax.rsqrt(ms + RMS_EPS)
    o_ref[0] = (on.T * sg_ref[...] * (1.0 - lam_init)).astype(BF16)


def _attention(qt, k_lat, vt_lat, k_ctx, vt_ctx, lq, lk, subln_g, *, lam_init):
    bsz, _, _, nq = qt.shape
    nc = k_ctx.shape[1]
    tq = min(ATTN_Q_TILE, nq)
    has_lat = k_lat is not None
    tk = ATTN_KV_CHUNK
    n_lat_chunks = (k_lat.shape[1] // tk) if has_lat else 0
    assert n_lat_chunks % 2 == 0 and ATTN_UNROLL % 2 == 0
    s_rows = tk if has_lat else nc
    in_specs = [pl.BlockSpec((1, 1, V_DIM, tq), lambda b, h, i: (b, h, 0, i))]
    args = [qt]
    if has_lat:
        ns = k_lat.shape[1]
        in_specs += [pl.BlockSpec((1, ns, V_DIM), lambda b, h, i: (b, 0, h)),
                     pl.BlockSpec((1, 1, V_DIM, ns), lambda b, h, i: (b, h, 0, 0))]
        args += [k_lat, vt_lat]
    in_specs += [pl.BlockSpec((1, nc, V_DIM), lambda b, h, i: (b, 0, h)),
                 pl.BlockSpec((1, 1, V_DIM, nc), lambda b, h, i: (b, h, 0, 0)),
                 pl.BlockSpec((2, QK_DIM), lambda b, h, i: (0, 0)),
                 pl.BlockSpec((2, QK_DIM), lambda b, h, i: (0, 0)),
                 pl.BlockSpec((1, V_DIM), lambda b, h, i: (0, 0))]
    args += [k_ctx, vt_ctx, lq, lk, subln_g]
    return pl.pallas_call(
        functools.partial(_attn_kernel, n_lat_chunks=n_lat_chunks, tk=tk, lam_init=lam_init),
        grid=(bsz, HEADS, nq // tq),
        in_specs=in_specs,
        out_specs=pl.BlockSpec((1, tq, V_DIM), lambda b, h, i: (b, i, h)),
        out_shape=jax.ShapeDtypeStruct((bsz, nq, HEAD_COLS), BF16),
        scratch_shapes=[
            pltpu.VMEM((V_DIM, 2 * tq), BF16),
            pltpu.VMEM((2, s_rows, 2 * tq), F32),
            pltpu.VMEM((2, 1, 2 * tq), F32),
            pltpu.VMEM((1, 2 * tq), F32),
            pltpu.VMEM((V_DIM + SUM_ROWS, 2 * tq), F32),
        ],
        compiler_params=_params(("parallel", "parallel", "arbitrary")),
        name="diff_attn_latent" if has_lat else "diff_attn_ctx",
    )(*args)


def _route_rows(logits_t):
    mx = jnp.max(logits_t, axis=0, keepdims=True)
    ex = jnp.exp(logits_t - mx)
    probs = ex / jnp.sum(ex, axis=0, keepdims=True)
    p = [probs[e:e + 1, :] for e in range(N_EXPERTS)]
    scores = []
    for g in range(N_GROUPS):
        a, b, c, d = p[4 * g:4 * g + 4]
        hi1, lo1 = jnp.maximum(a, b), jnp.minimum(a, b)
        hi2, lo2 = jnp.maximum(c, d), jnp.minimum(c, d)
        top1 = jnp.maximum(hi1, hi2)
        top2 = jnp.maximum(jnp.minimum(hi1, hi2), jnp.maximum(lo1, lo2))
        scores.append(top1 + top2)
    best = jnp.zeros_like(scores[0], dtype=jnp.int32)
    best_score = scores[0]
    for g in range(1, N_GROUPS):
        better = scores[g] > best_score
        best = jnp.where(better, g, best)
        best_score = jnp.where(better, scores[g], best_score)
    sel = []
    for e in range(N_EXPERTS):
        g = e // EXPERTS_PER_GROUP
        rank = jnp.zeros_like(best)
        for j in range(g * EXPERTS_PER_GROUP, (g + 1) * EXPERTS_PER_GROUP):
            if j == e:
                continue
            ahead = (p[j] >= p[e]) if j < e else (p[j] > p[e])
            rank = rank + ahead.astype(jnp.int32)
        sel.append((best == g) & (rank < 2))
    kept = [jnp.where(sel[e], p[e], 0.0) for e in range(N_EXPERTS)]
    denom = kept[0]
    for e in range(1, N_EXPERTS):
        denom = denom + kept[e]
    return sel, [kp / denom for kp in kept]


def _merge_kernel(x_ref, ya_ref, yb_ref, ga_ref, gb_ref, mod_ref, wpa_ref, wpb_ref, wo_ref,
                  g1_ref, b1_ref, wrh_ref, wrl_ref, br_ref, x1_ref, h2_ref, route_ref, counts_ref,
                  *, d_model, alpha):
    d = d_model
    tm = x_ref.shape[1]
    gate1 = mod_ref[0, :, 2 * d:3 * d]
    shift2 = mod_ref[0, :, 3 * d:4 * d]
    scale2 = mod_ref[0, :, 4 * d:5 * d]
    a = jnp.dot(ya_ref[0], wpa_ref[...], preferred_element_type=F32)
    b = jnp.dot(yb_ref[0], wpb_ref[...], preferred_element_type=F32)
    merged = (ga_ref[0].astype(F32) * a + gb_ref[0].astype(F32) * b).astype(BF16)
    mix = jnp.dot(merged, wo_ref[...], preferred_element_type=F32)
    x1 = _ln_rows(alpha * x_ref[0] + gate1 * mix) * g1_ref[...] + b1_ref[...]
    x1_ref[0] = x1
    h2 = _ln_rows(x1) * (1.0 + scale2) + shift2
    h2_ref[0, :, 0:d] = h2
    h2_hi, h2_lo = _split_bf16(h2)
    logits = (jnp.dot(h2_hi, wrh_ref[...], preferred_element_type=F32)
              + jnp.dot(h2_lo, wrh_ref[...], preferred_element_type=F32)
              + jnp.dot(h2_hi, wrl_ref[...], preferred_element_type=F32))
    logits_t = logits.T[0:N_EXPERTS, :] + jnp.tile(br_ref[...], (1, tm // br_ref.shape[1]))
    sel, gate_rows = _route_rows(logits_t)

    e_lo = jnp.full((1, tm), N_EXPERTS, jnp.int32)
    e_hi = jnp.full((1, tm), -1, jnp.int32)
    for e in range(N_EXPERTS):
        e_lo = jnp.where(sel[e], jnp.minimum(e_lo, e), e_lo)
        e_hi = jnp.where(sel[e], jnp.maximum(e_hi, e), e_hi)
    g_lo = jnp.zeros((1, tm), F32)
    g_hi = jnp.zeros((1, tm), F32)
    for e in range(N_EXPERTS):
        g_lo = jnp.where(e_lo == e, gate_rows[e], g_lo)
        g_hi = jnp.where(e_hi == e, gate_rows[e], g_hi)
    a = e_lo & (EXPERTS_PER_GROUP - 1)
    b = e_hi & (EXPERTS_PER_GROUP - 1)
    cls = (e_lo >> 2) * PAIRS_PER_GROUP + ((a * (7 - a)) >> 1) + (b - a - 1)
    cls_t = jnp.where(lax.broadcasted_iota(jnp.int32, (CLASS_ROWS, tm), 0) == cls, 1.0, 0.0)
    earlier = (lax.broadcasted_iota(jnp.int32, (tm, tm), 0)
               < lax.broadcasted_iota(jnp.int32, (tm, tm), 1))
    rank_t = jnp.dot(cls_t.astype(BF16), jnp.where(earlier, 1.0, 0.0).astype(BF16),
                     preferred_element_type=F32)
    rank = jnp.sum(cls_t * rank_t, axis=0, keepdims=True)
    route_ref[0] = jnp.concatenate(
        [cls, rank.astype(jnp.int32), jnp.zeros((ROUTE_ROWS - 2, tm), jnp.int32)], axis=0)
    h2_ref[0, :, d:d + GATE_LANES] = jnp.concatenate(
        [g_lo, g_hi, jnp.zeros((GATE_LANES - 2, tm), F32)], axis=0).T
    counts = jnp.sum(cls_t, axis=1, keepdims=True)
    counts_ref[0, 0] = jnp.broadcast_to(counts, (CLASS_ROWS, 128)).astype(jnp.int32)


def _merge(x, ya, yb, ga, gb, mod, w_pa, w_pb, w_o, g1, b1, w_r_hi, w_r_lo, b_r_b, *, alpha):
    bsz, n, d = x.shape
    tm = min(TOKEN_TILE, n)
    nt = n // tm
    tok = lambda b, i: (b, i, 0)
    return pl.pallas_call(
        functools.partial(_merge_kernel, d_model=d, alpha=alpha),
        grid=(bsz, nt),
        in_specs=[
            pl.BlockSpec((1, tm, d), tok),
            pl.BlockSpec((1, tm, A_WIDTH), tok),
            pl.BlockSpec((1, tm, HEAD_COLS), tok),
            pl.BlockSpec((1, tm, d), tok),
            pl.BlockSpec((1, tm, d), tok),
            pl.BlockSpec((1, 1, N_MOD * d), lambda b, i: (b, 0, 0)),
            _resident((A_WIDTH, d)),
            _resident((HEAD_COLS, d)),
            _resident((d, d)),
            _resident((1, d)),
            _resident((1, d)),
            _resident((d, GATE_LANES)),
            _resident((d, GATE_LANES)),
            _resident((N_EXPERTS, 128)),
        ],
        out_specs=[
            pl.BlockSpec((1, tm, d), tok),
            pl.BlockSpec((1, tm, d + GATE_LANES), tok),
            pl.BlockSpec((1, ROUTE_ROWS, tm), lambda b, i: (b, 0, i)),
            pl.BlockSpec((1, 1, CLASS_ROWS, 128), lambda b, i: (b, i, 0, 0)),
        ],
        out_shape=[
            jax.ShapeDtypeStruct((bsz, n, d), F32),
            jax.ShapeDtypeStruct((bsz, n, d + GATE_LANES), F32),
            jax.ShapeDtypeStruct((bsz, ROUTE_ROWS, n), jnp.int32),
            jax.ShapeDtypeStruct((bsz, nt, CLASS_ROWS, 128), jnp.int32),
        ],
        compiler_params=_params(("parallel", "parallel")),
        name="merge_route",
    )(x, ya, yb, ga, gb, mod, w_pa, w_pb, w_o, g1, b1, w_r_hi, w_r_lo, b_r_b)


def _dispatch_plan(routes, counts_list, n_tiles_max):
    tm = MOE_TOKEN_TILE
    n_srcs = [c.shape[0] * c.shape[1] for c in counts_list]
    n_src = sum(n_srcs)
    cnt = jnp.concatenate([c[..., 0].reshape(-1, CLASS_ROWS) for c in counts_list], axis=0)
    cls = jnp.concatenate([r[:, 0].reshape(-1, tm) for r in routes], axis=0)
    rank = jnp.concatenate([r[:, 1].reshape(-1, tm) for r in routes], axis=0)
    total = jnp.sum(cnt, axis=0)
    tiles_per_class = (total + MOE_ROW_TILE - 1) // MOE_ROW_TILE
    classes = jnp.arange(CLASS_ROWS, dtype=jnp.int32)
    tile_end = jnp.sum(jnp.where(classes[None, :] <= classes[:, None], tiles_per_class[None, :], 0), axis=1)
    seg_start = (tile_end - tiles_per_class) * MOE_ROW_TILE
    src = jnp.arange(n_src, dtype=jnp.int32)
    before = jnp.sum(jnp.where((src[None, :] < src[:, None])[:, :, None], cnt[None, :, :], 0), axis=1)
    base = (seg_start[None, :] + before)[:, None, :]
    pos = jnp.sum(jnp.where(cls[..., None] == classes, base, 0), axis=-1) + rank
    tile_ids = jnp.arange(n_tiles_max, dtype=jnp.int32)
    tile_class = jnp.sum((tile_end[None, :] <= tile_ids[:, None]).astype(jnp.int32), axis=1)
    tile_class = jnp.minimum(tile_class, len(PAIR_CLASSES) - 1)
    pairs = jnp.asarray(PAIR_CLASSES, dtype=jnp.int32)
    tile_pair = jnp.sum(jnp.where((tile_class[:, None] == classes[None, :len(PAIR_CLASSES)])[:, :, None],
                                  pairs[None, :, :], 0), axis=1)
    pos_tiles, start = [], 0
    for n_s in n_srcs:
        pos_tiles.append(pos[start:start + n_s].reshape(n_s, 1, tm))
        start += n_s
    return pos_tiles, tile_pair[:, 0], tile_pair[:, 1], tile_end[-1:]


def _dispatch_kernel(pos_ref, h_ref, hs_in_hbm, hs_hbm, sem, *, tile):
    del hs_in_hbm

    def row_copy(t, dst_row):
        return pltpu.make_async_copy(h_ref.at[0, pl.ds(t, 1), :], hs_hbm.at[pl.ds(dst_row, 1), :], sem)

    def issue(t, carry):
        row_copy(t, pos_ref[0, 0, t]).start()
        return carry
    lax.fori_loop(0, tile, issue, 0, unroll=8)

    def drain(t, carry):
        row_copy(0, 0).wait()
        return carry
    lax.fori_loop(0, tile, drain, 0, unroll=8)


def _dispatch(h2, pos_tiles, hs_init):
    bsz, n, width = h2.shape
    n_rows = hs_init.shape[0]
    tile = pos_tiles.shape[-1]
    nt = n // tile
    return pl.pallas_call(
        functools.partial(_dispatch_kernel, tile=tile),
        grid=(bsz, nt),
        in_specs=[
            pl.BlockSpec((1, 1, tile), lambda b, i: (b * nt + i, 0, 0), memory_space=pltpu.SMEM),
            pl.BlockSpec((1, tile, width), lambda b, i: (b, i, 0)),
            pl.BlockSpec(memory_space=pl.ANY),
        ],
        out_specs=pl.BlockSpec(memory_space=pl.ANY),
        out_shape=jax.ShapeDtypeStruct((n_rows, width), F32),
        scratch_shapes=[pltpu.SemaphoreType.DMA(())],
        input_output_aliases={2: 0},
        compiler_params=_params(("arbitrary", "arbitrary")),
        name="moe_dispatch",
    )(pos_tiles, h2, hs_init)


def _experts_kernel(lo_ref, hi_ref, nv_ref, hs_ref, w1l_ref, w3l_ref, w2l_ref, w1h_ref, w3h_ref, w2h_ref,
                    ys_ref, w1l_b, w3l_b, w2l_b, w1h_b, w3h_b, w2h_b, *, d_model):
    d = d_model
    j = pl.program_id(0)
    prev = jnp.maximum(j - 1, 0)

    @pl.when((j == 0) | (lo_ref[j] != lo_ref[prev]))
    def _():
        w1l_b[...] = w1l_ref[0, 0].astype(BF16)
        w3l_b[...] = w3l_ref[0, 0].astype(BF16)
        w2l_b[...] = w2l_ref[0, 0].astype(BF16)

    @pl.when((j == 0) | (hi_ref[j] != hi_ref[prev]))
    def _():
        w1h_b[...] = w1h_ref[0, 0].astype(BF16)
        w3h_b[...] = w3h_ref[0, 0].astype(BF16)
        w2h_b[...] = w2h_ref[0, 0].astype(BF16)

    def expert(h, w1, w3, w2):
        a = jnp.dot(h, w1[...], preferred_element_type=F32)
        b = jnp.dot(h, w3[...], preferred_element_type=F32)
        hid = (a * jax.nn.sigmoid(a) * b).astype(BF16)
        return jnp.dot(hid, w2[...], preferred_element_type=F32)

    @pl.when(j < nv_ref[0])
    def _():
        h = hs_ref[:, 0:d].astype(BF16)
        gates = hs_ref[:, d:d + GATE_LANES]
        ys_ref[...] = (gates[:, 0:1] * expert(h, w1l_b, w3l_b, w2l_b)
                       + gates[:, 1:2] * expert(h, w1h_b, w3h_b, w2h_b))

    @pl.when(j >= nv_ref[0])
    def _():
        ys_ref[...] = jnp.zeros(ys_ref.shape, F32)


def _experts(hs, tile_lo, tile_hi, n_valid, w1, w3, w2, *, layer):
    n_rows, width = hs.shape
    d = width - GATE_LANES
    de = w1.shape[-1]
    n_tiles = n_rows // MOE_ROW_TILE
    lo_map = lambda j, lo, hi, nv: (layer, lo[j], 0, 0)
    hi_map = lambda j, lo, hi, nv: (layer, hi[j], 0, 0)
    up, down = (1, 1, d, de), (1, 1, de, d)
    return pl.pallas_call(
        functools.partial(_experts_kernel, d_model=d),
        grid_spec=pltpu.PrefetchScalarGridSpec(
            num_scalar_prefetch=3,
            grid=(n_tiles,),
            in_specs=[
                pl.BlockSpec((MOE_ROW_TILE, width), lambda j, lo, hi, nv: (jnp.minimum(j, nv[0] - 1), 0)),
                pl.BlockSpec(up, lo_map), pl.BlockSpec(up, lo_map), pl.BlockSpec(down, lo_map),
                pl.BlockSpec(up, hi_map), pl.BlockSpec(up, hi_map), pl.BlockSpec(down, hi_map),
            ],
            out_specs=pl.BlockSpec((MOE_ROW_TILE, d), lambda j, lo, hi, nv: (j, 0)),
            scratch_shapes=[pltpu.VMEM((d, de), BF16), pltpu.VMEM((d, de), BF16), pltpu.VMEM((de, d), BF16)] * 2,
        ),
        out_shape=jax.ShapeDtypeStruct((n_rows, d), F32),
        compiler_params=_params(("arbitrary",)),
        name="moe_experts",
    )(tile_lo, tile_hi, n_valid, hs, w1, w3, w2, w1, w3, w2)


def _combine_kernel(pos_ref, nxt_ref, x1_ref, mod_ref, g2_ref, b2_ref, ys_hbm,
                    o_ref, ybuf, sem, *, tile, d_model, alpha):
    d = d_model
    step = pl.program_id(0) * pl.num_programs(1) + pl.program_id(1)
    n_steps = pl.num_programs(0) * pl.num_programs(1)
    slot = step % 2

    def row_copy(src_row, slot_, dst_row):
        return pltpu.make_async_copy(ys_hbm.at[pl.ds(src_row, 1), :],
                                     ybuf.at[slot_, pl.ds(dst_row, 1), :], sem.at[slot_])

    def gather(p_ref, slot_):
        def issue(t, carry):
            row_copy(p_ref[0, 0, t], slot_, t).start()
            return carry
        lax.fori_loop(0, tile, issue, 0, unroll=8)

    @pl.when(step == 0)
    def _():
        gather(pos_ref, 0)

    @pl.when(step + 1 < n_steps)
    def _():
        gather(nxt_ref, 1 - slot)

    def drain(t, carry):
        row_copy(0, slot, 0).wait()
        return carry
    lax.fori_loop(0, tile, drain, 0, unroll=8)

    gate2 = mod_ref[0, :, 5 * d:6 * d]
    o_ref[0] = _ln_rows(alpha * x1_ref[0] + gate2 * ybuf[slot]) * g2_ref[...] + b2_ref[...]


def _combine(ys, pos_tiles, x1, mod, g2, b2, *, alpha):
    bsz, n, d = x1.shape
    tile = pos_tiles.shape[-1]
    nt = n // tile
    n_steps = bsz * nt
    tok = lambda b, i: (b, i, 0)
    smem_tile = lambda index_map: pl.BlockSpec((1, 1, tile), index_map, memory_space=pltpu.SMEM)
    return pl.pallas_call(
        functools.partial(_combine_kernel, tile=tile, d_model=d, alpha=alpha),
        grid=(bsz, nt),
        in_specs=[
            smem_tile(lambda b, i: (b * nt + i, 0, 0)),
            smem_tile(lambda b, i: (jnp.minimum(b * nt + i + 1, n_steps - 1), 0, 0)),
            pl.BlockSpec((1, tile, d), tok),
            pl.BlockSpec((1, 1, N_MOD * d), lambda b, i: (b, 0, 0)),
            pl.BlockSpec((1, d), lambda b, i: (0, 0)),
            pl.BlockSpec((1, d), lambda b, i: (0, 0)),
            pl.BlockSpec(memory_space=pl.ANY),
        ],
        out_specs=pl.BlockSpec((1, tile, d), tok),
        out_shape=jax.ShapeDtypeStruct((bsz, n, d), F32),
        scratch_shapes=[pltpu.VMEM((2, tile, d), F32), pltpu.SemaphoreType.DMA((2,))],
        compiler_params=_params(("arbitrary", "arbitrary")),
        name="moe_combine",
    )(pos_tiles, pos_tiles, x1, mod, g2, b2, ys)


def _moe(streams, hs, w1, w3, w2, g2, b2, *, alpha, layer):
    pos_tiles, tile_lo, tile_hi, n_valid = _dispatch_plan(
        [s[1] for s in streams], [s[2] for s in streams], hs.shape[0] // MOE_ROW_TILE)
    for (h2g, _, _, _, _), pos in zip(streams, pos_tiles):
        hs = _dispatch(h2g, pos, hs)
    ys = _experts(hs, tile_lo, tile_hi, n_valid, w1, w3, w2, layer=layer)
    outs = [_combine(ys, pos, x1, mod, g2, b2, alpha=alpha)
            for (_, _, _, x1, mod), pos in zip(streams, pos_tiles)]
    return outs, hs


def _rope_tables(n_tokens):
    rows = n_tokens // GRID_W
    row = jnp.repeat(jnp.arange(rows, dtype=F32), GRID_W)
    col = jnp.tile(jnp.arange(GRID_W, dtype=F32), rows)
    n_freq = QK_DIM // 4
    inv_freq = ROPE_BASE ** (-jnp.arange(n_freq, dtype=F32) / n_freq)
    ang_r = row[:, None] * inv_freq
    ang_c = col[:, None] * inv_freq
    cos64 = jnp.concatenate([jnp.cos(ang_r), jnp.cos(ang_r), jnp.cos(ang_c), jnp.cos(ang_c)], axis=-1)
    sin64 = jnp.concatenate([-jnp.sin(ang_r), jnp.sin(ang_r), -jnp.sin(ang_c), jnp.sin(ang_c)], axis=-1)
    return jnp.tile(cos64, (1, 2)), jnp.tile(sin64, (1, 2))


def kernel(x, c, ctx, c_ctx, w_mod, b_mod, w_in, sgu_g, sgu_b, w_s, b_s, lambda_q, lambda_k, subln_g,
           w_pa, w_pb, w_o, ln1_g, ln1_b, w_router, b_router, w1, w3, w2, ln2_g, ln2_b):
    bsz, n_lat, d = x.shape
    depth = w_mod.shape[0]
    alpha = (2.0 * depth) ** 0.25
    cos_t, sin_t = _rope_tables(n_lat)

    cond = jnp.zeros((8, d), F32).at[0:bsz].set(c).at[bsz].set(c_ctx)
    w_r_pad = jnp.pad(w_router, ((0, 0), (0, GATE_LANES - N_EXPERTS)))
    w_r_hi, w_r_lo = _split_bf16(w_r_pad)
    b_r_b = jnp.broadcast_to(b_router[:, None], (N_EXPERTS, 128))
    row = lambda v: v.reshape(1, -1)
    n_moe_rows = bsz * (n_lat + ctx.shape[1]) + len(PAIR_CLASSES) * MOE_ROW_TILE
    moe_rows = jnp.zeros((n_moe_rows, d + GATE_LANES), F32)

    for l in range(depth):
        last = l == depth - 1
        lam_init = 0.8 - 0.6 * math.exp(-0.3 * l)
        mod = _modulation(cond, w_mod[l], b_mod[l])
        mod_lat = mod[0:bsz, None, :]
        mod_ctx = jnp.broadcast_to(mod[bsz][None, None, :], (bsz, 1, N_MOD * d))
        w_in_l = w_in[l].astype(BF16)
        w_s_l = w_s[l].astype(BF16)
        b_s_b = jnp.broadcast_to(b_s[l][:, :, None], (A_GROUPS, CHUNK, CHUNK))
        proj_args = (w_in_l, row(sgu_g[l]), row(sgu_b[l]), w_s_l, b_s_b, cos_t, sin_t)
        attn_args = (lambda_q[l], lambda_k[l], row(subln_g[l]))
        merge_w = (w_pa[l].astype(BF16), w_pb[l].astype(BF16), w_o[l].astype(BF16),
                   row(ln1_g[l]), row(ln1_b[l]), w_r_hi, w_r_lo, b_r_b)
        moe_w = (w1, w3, w2, row(ln2_g[l]), row(ln2_b[l]))

        ya, qt, k, vt, ga, gb = _inproj(x, mod_lat, *proj_args, use_rope=True)
        cya, cqt, ck, cvt, cga, cgb = _inproj(ctx, mod_ctx, *proj_args, use_rope=False)
        yb = _attention(qt, k, vt, ck, cvt, *attn_args, lam_init=lam_init)
        x1, h2g, route, counts = _merge(x, ya, yb, ga, gb, mod_lat, *merge_w, alpha=alpha)
        streams = [(h2g, route, counts, x1, mod_lat)]
        if not last:
            cyb = _attention(cqt, None, None, ck, cvt, *attn_args, lam_init=lam_init)
            c1, ch2g, croute, ccounts = _merge(ctx, cya, cyb, cga, cgb, mod_ctx, *merge_w, alpha=alpha)
            streams.append((ch2g, croute, ccounts, c1, mod_ctx))
        outs, moe_rows = _moe(streams, moe_rows, *moe_w, alpha=alpha, layer=l)
        x = outs[0]
        if not last:
            ctx = outs[1]
    return x
```

```python
import functools
import math

import jax
import jax.numpy as jnp
from jax import lax
from jax.experimental import pallas as pl
from jax.experimental.pallas import tpu as pltpu

F32 = jnp.float32
BF16 = jnp.bfloat16

GRID_W = 64
CHUNK = 128
A_GROUPS = 8
A_WIDTH = 1024
HEADS = 8
QK_DIM = 64
V_DIM = 2 * QK_DIM
HEAD_COLS = HEADS * V_DIM
ROPE_BASE = 10000.0
N_EXPERTS = 16
N_GROUPS = 4
EXPERTS_PER_GROUP = N_EXPERTS // N_GROUPS
N_MOD = 6
PAIR_CLASSES = tuple((4 * g + a, 4 * g + b) for g in range(N_GROUPS)
                     for a in range(EXPERTS_PER_GROUP) for b in range(a + 1, EXPERTS_PER_GROUP))
PAIRS_PER_GROUP = len(PAIR_CLASSES) // N_GROUPS
CLASS_ROWS = 32
ROUTE_ROWS = 8
GATE_LANES = 128
LN_EPS = 1e-5
RMS_EPS = 1e-5
NEG_BIG = -1e30

VMEM_LIMIT_BYTES = 56 * 1024 * 1024

TOKEN_TILE = 256
ATTN_Q_TILE = 512
ATTN_KV_CHUNK = 512
ATTN_UNROLL = 6
SUM_ROWS = 16
MOE_TOKEN_TILE = TOKEN_TILE
DISPATCH_TILE = 1024
COMBINE_TILE = 512
MOE_ROW_TILE = 256


def _params(semantics):
    return pltpu.CompilerParams(dimension_semantics=semantics, vmem_limit_bytes=VMEM_LIMIT_BYTES)


def _resident(shape):
    nd = len(shape)
    return pl.BlockSpec(shape, lambda *_: (0,) * nd, pipeline_mode=pl.Buffered(1))


def _split_bf16(x):
    bits = lax.bitcast_convert_type(x, jnp.uint32) & jnp.uint32(0xFFFF0000)
    hi = lax.bitcast_convert_type(bits, F32)
    return hi.astype(BF16), (x - hi).astype(BF16)


def _ln_rows(x):
    mu = jnp.mean(x, axis=-1, keepdims=True)
    xc = x - mu
    var = jnp.mean(xc * xc, axis=-1, keepdims=True)
    return xc * lax.rsqrt(var + LN_EPS)


def _mod_kernel(cond_ref, w_ref, b_ref, o_ref):
    c = cond_ref[...]
    s = c * jax.nn.sigmoid(c)
    o_ref[...] = jnp.dot(s, w_ref[...], preferred_element_type=F32, precision=lax.Precision.HIGHEST) + b_ref[...]


def _modulation(cond, w_mod, b_mod):
    rows, d = cond.shape
    return pl.pallas_call(
        _mod_kernel,
        grid=(N_MOD,),
        in_specs=[
            pl.BlockSpec((rows, d), lambda j: (0, 0)),
            pl.BlockSpec((d, d), lambda j: (0, j)),
            pl.BlockSpec((1, d), lambda j: (0, j)),
        ],
        out_specs=pl.BlockSpec((rows, d), lambda j: (0, j)),
        out_shape=jax.ShapeDtypeStruct((rows, N_MOD * d), F32),
        compiler_params=_params(("arbitrary",)),
        name="modulation",
    )(cond, w_mod, b_mod.reshape(1, -1))


def _rope(x, cos, sin_signed):
    n = x.shape[-1]
    reps = n // cos.shape[-1]
    c = jnp.tile(cos, (1, reps))
    s = jnp.tile(sin_signed, (1, reps))
    lane = lax.broadcasted_iota(jnp.int32, x.shape, 1)
    first = (lane & 31) < 16
    partner = jnp.where(first, pltpu.roll(x, n - 16, 1), pltpu.roll(x, 16, 1))
    return x * c + partner * s


def _inproj_kernel(x_ref, mod_ref, w_ref, sg_ref, sb_ref, ws_ref, bs_ref, cos_ref, sin_ref,
                   ya_ref, qt_ref, k_ref, vt_ref, ga_ref, gb_ref, *, d_model, use_rope, q_scale):
    d = d_model
    tm = x_ref.shape[1]
    shift = mod_ref[0, :, 0:d]
    scale = mod_ref[0, :, d:2 * d]
    h = (_ln_rows(x_ref[0]) * (1.0 + scale) + shift).astype(BF16)

    def proj(c0, width):
        return jnp.dot(h, w_ref[:, c0:c0 + width], preferred_element_type=F32)

    c_u, c_v = 0, A_WIDTH
    c_q = 2 * A_WIDTH
    c_k = c_q + HEAD_COLS
    c_vb = c_k + HEAD_COLS
    c_ga = c_vb + HEAD_COLS
    c_gb = c_ga + d

    u = jax.nn.gelu(proj(c_u, A_WIDTH))
    v = jax.nn.gelu(proj(c_v, A_WIDTH))
    vn = (_ln_rows(v) * sg_ref[...] + sb_ref[...]).astype(BF16)
    gd = A_WIDTH // A_GROUPS
    for pair in range(tm // (2 * CHUNK)):
        r0 = pair * 2 * CHUNK
        r1 = r0 + CHUNK
        for g in range(A_GROUPS):
            cols = slice(g * gd, (g + 1) * gd)
            rhs = jnp.concatenate([vn[r0:r0 + CHUNK, cols], vn[r1:r1 + CHUNK, cols]], axis=1)
            y = jnp.dot(ws_ref[g], rhs, preferred_element_type=F32)
            bias = bs_ref[g]
            ya_ref[0, r0:r0 + CHUNK, cols] = (u[r0:r0 + CHUNK, cols] * (y[:, :gd] + bias)).astype(BF16)
            ya_ref[0, r1:r1 + CHUNK, cols] = (u[r1:r1 + CHUNK, cols] * (y[:, gd:] + bias)).astype(BF16)

    q = proj(c_q, HEAD_COLS)
    k = proj(c_k, HEAD_COLS)
    if use_rope:
        q = _rope(q, cos_ref[...], sin_ref[...])
        k = _rope(k, cos_ref[...], sin_ref[...])
    q = q * q_scale
    k_ref[0] = k.astype(BF16)
    vb = proj(c_vb, HEAD_COLS)
    for hd in range(HEADS):
        cols = slice(hd * V_DIM, (hd + 1) * V_DIM)
        qt_ref[0, hd] = q[:, cols].T.astype(BF16)
        vt_ref[0, hd] = vb[:, cols].T.astype(BF16)

    ga_ref[0] = jax.nn.sigmoid(proj(c_ga, d)).astype(BF16)
    gb_ref[0] = jax.nn.sigmoid(proj(c_gb, d)).astype(BF16)


def _inproj(x, mod, w_in, sgu_g, sgu_b, w_s, b_s_b, cos_t, sin_t, *, use_rope):
    bsz, n, d = x.shape
    tm = min(TOKEN_TILE, n)
    nt = n // tm
    in_cols = w_in.shape[1]
    q_scale = (QK_DIM ** -0.5) * math.log2(math.e)
    tok = lambda b, i: (b, i, 0)
    tr = lambda b, i: (b, 0, 0, i)
    rope_map = (lambda b, i: (i, 0)) if use_rope else (lambda b, i: (0, 0))
    return pl.pallas_call(
        functools.partial(_inproj_kernel, d_model=d, use_rope=use_rope, q_scale=q_scale),
        grid=(bsz, nt),
        in_specs=[
            pl.BlockSpec((1, tm, d), tok),
            pl.BlockSpec((1, 1, N_MOD * d), lambda b, i: (b, 0, 0)),
            _resident((d, in_cols)),
            _resident((1, A_WIDTH)),
            _resident((1, A_WIDTH)),
            _resident((A_GROUPS, CHUNK, CHUNK)),
            _resident((A_GROUPS, CHUNK, CHUNK)),
            pl.BlockSpec((tm, 2 * QK_DIM), rope_map),
            pl.BlockSpec((tm, 2 * QK_DIM), rope_map),
        ],
        out_specs=[
            pl.BlockSpec((1, tm, A_WIDTH), tok),
            pl.BlockSpec((1, HEADS, V_DIM, tm), tr),
            pl.BlockSpec((1, tm, HEAD_COLS), tok),
            pl.BlockSpec((1, HEADS, V_DIM, tm), tr),
            pl.BlockSpec((1, tm, d), tok),
            pl.BlockSpec((1, tm, d), tok),
        ],
        out_shape=[
            jax.ShapeDtypeStruct((bsz, n, A_WIDTH), BF16),
            jax.ShapeDtypeStruct((bsz, HEADS, V_DIM, n), BF16),
            jax.ShapeDtypeStruct((bsz, n, HEAD_COLS), BF16),
            jax.ShapeDtypeStruct((bsz, HEADS, V_DIM, n), BF16),
            jax.ShapeDtypeStruct((bsz, n, d), BF16),
            jax.ShapeDtypeStruct((bsz, n, d), BF16),
        ],
        compiler_params=_params(("parallel", "parallel")),
        name="inproj_rope" if use_rope else "inproj_ctx",
    )(x, mod, w_in, sgu_g, sgu_b, w_s, b_s_b, cos_t, sin_t)


def _attn_kernel(*refs, n_lat_chunks, tk, lam_init):
    if n_lat_chunks:
        (qt_ref, kl_ref, vtl_ref, kc_ref, vtc_ref, lq_ref, lk_ref, sg_ref,
         o_ref, rhs_ref, s_ref, cm_ref, m_ref, acc_ref) = refs
    else:
        (qt_ref, kc_ref, vtc_ref, lq_ref, lk_ref, sg_ref,
         o_ref, rhs_ref, s_ref, cm_ref, m_ref, acc_ref) = refs
    tq = qt_ref.shape[-1]
    nc = kc_ref.shape[1]

    qt = qt_ref[0, 0]
    row = lax.broadcasted_iota(jnp.int32, qt.shape, 0)
    zero = jnp.zeros_like(qt)
    rhs_ref[:, 0:tq] = jnp.where(row < QK_DIM, qt, zero)
    rhs_ref[:, tq:2 * tq] = jnp.where(row >= QK_DIM, qt, zero)
    m_ref[...] = jnp.full(m_ref.shape, NEG_BIG, F32)
    acc_ref[...] = jnp.zeros(acc_ref.shape, F32)

    def stage_a(kc, slot, rows):
        s = jnp.dot(kc, rhs_ref[...], preferred_element_type=F32)
        s_ref[slot, 0:rows, :] = s
        cm_ref[slot] = jnp.max(s, axis=0, keepdims=True)

    def stage_b(vtc, slot, rows):
        m_old = m_ref[...]
        m_new = jnp.maximum(m_old, cm_ref[slot])
        alpha = jnp.exp2(m_old - m_new)
        p = jnp.exp2(s_ref[slot, 0:rows, :] - m_new)
        vt_ones = jnp.concatenate([vtc, jnp.ones((SUM_ROWS, rows), BF16)], axis=0)
        acc_ref[...] = alpha * acc_ref[...] + jnp.dot(vt_ones, p.astype(BF16), preferred_element_type=F32)
        m_ref[...] = m_new

    if n_lat_chunks:
        def lat_k(c):
            return kl_ref[0, pl.ds(pl.multiple_of(c * tk, tk), tk), :]

        def lat_vt(c):
            return vtl_ref[0, 0, :, pl.ds(pl.multiple_of(c * tk, tk), tk)]

        stage_a(lat_k(0), 0, tk)

        def pipelined(c0, n):
            for i in range(n):
                c = c0 + i
                if isinstance(c, int) and c + 1 == n_lat_chunks:
                    stage_a(kc_ref[0], (i + 1) % 2, nc)
                else:
                    stage_a(lat_k(c + 1), (i + 1) % 2, tk)
                stage_b(lat_vt(c), i % 2, tk)

        def body(j, carry):
            pipelined(ATTN_UNROLL * j, ATTN_UNROLL)
            return carry
        n_loop = (n_lat_chunks - 2) // ATTN_UNROLL
        lax.fori_loop(0, n_loop, body, 0)
        pipelined(n_loop * ATTN_UNROLL, n_lat_chunks - n_loop * ATTN_UNROLL)
        stage_b(vtc_ref[0, 0], n_lat_chunks % 2, nc)
    else:
        stage_a(kc_ref[0], 0, nc)
        stage_b(vtc_ref[0, 0], 0, nc)

    lq = lq_ref[...]
    lk = lk_ref[...]
    lam = (jnp.exp(jnp.sum(lq[0:1] * lk[0:1], keepdims=True))
           - jnp.exp(jnp.sum(lq[1:2] * lk[1:2], keepdims=True)) + lam_init)
    acc = acc_ref[0:V_DIM, :]
    l = acc_ref[V_DIM:V_DIM + 1, :]
    o = acc[:, 0:tq] / l[:, 0:tq] - lam * (acc[:, tq:2 * tq] / l[:, tq:2 * tq])
    ms = jnp.mean(o * o, axis=0, keepdims=True)
    on = o * lax.rsqrt(ms + RMS_EPS)
    o_ref[0] = (on.T * sg_ref[...] * (1.0 - lam_init)).astype(BF16)


def _attention(qt, k_lat, vt_lat, k_ctx, vt_ctx, lq, lk, subln_g, *, lam_init):
    bsz, _, _, nq = qt.shape
    nc = k_ctx.shape[1]
    tq = min(ATTN_Q_TILE, nq)
    has_lat = k_lat is not None
    tk = ATTN_KV_CHUNK
    n_lat_chunks = (k_lat.shape[1] // tk) if has_lat else 0
    assert n_lat_chunks % 2 == 0 and ATTN_UNROLL % 2 == 0
    s_rows = tk if has_lat else nc
    in_specs = [pl.BlockSpec((1, 1, V_DIM, tq), lambda b, h, i: (b, h, 0, i))]
    args = [qt]
    if has_lat:
        ns = k_lat.shape[1]
        in_specs += [pl.BlockSpec((1, ns, V_DIM), lambda b, h, i: (b, 0, h)),
                     pl.BlockSpec((1, 1, V_DIM, ns), lambda b, h, i: (b, h, 0, 0))]
        args += [k_lat, vt_lat]
    in_specs += [pl.BlockSpec((1, nc, V_DIM), lambda b, h, i: (b, 0, h)),
                 pl.BlockSpec((1, 1, V_DIM, nc), lambda b, h, i: (b, h, 0, 0)),
                 pl.BlockSpec((2, QK_DIM), lambda b, h, i: (0, 0)),
                 pl.BlockSpec((2, QK_DIM), lambda b, h, i: (0, 0)),
                 pl.BlockSpec((1, V_DIM), lambda b, h, i: (0, 0))]
    args += [k_ctx, vt_ctx, lq, lk, subln_g]
    return pl.pallas_call(
        functools.partial(_attn_kernel, n_lat_chunks=n_lat_chunks, tk=tk, lam_init=lam_init),
        grid=(bsz, HEADS, nq // tq),
        in_specs=in_specs,
        out_specs=pl.BlockSpec((1, tq, V_DIM), lambda b, h, i: (b, i, h)),
        out_shape=jax.ShapeDtypeStruct((bsz, nq, HEAD_COLS), BF16),
        scratch_shapes=[
            pltpu.VMEM((V_DIM, 2 * tq), BF16),
            pltpu.VMEM((2, s_rows, 2 * tq), F32),
            pltpu.VMEM((2, 1, 2 * tq), F32),
            pltpu.VMEM((1, 2 * tq), F32),
            pltpu.VMEM((V_DIM + SUM_ROWS, 2 * tq), F32),
        ],
        compiler_params=_params(("parallel", "parallel", "arbitrary")),
        name="diff_attn_latent" if has_lat else "diff_attn_ctx",
    )(*args)


def _route_rows(logits_t):
    mx = jnp.max(logits_t, axis=0, keepdims=True)
    ex = jnp.exp(logits_t - mx)
    probs = ex / jnp.sum(ex, axis=0, keepdims=True)
    p = [probs[e:e + 1, :] for e in range(N_EXPERTS)]
    scores = []
    for g in range(N_GROUPS):
        a, b, c, d = p[4 * g:4 * g + 4]
        hi1, lo1 = jnp.maximum(a, b), jnp.minimum(a, b)
        hi2, lo2 = jnp.maximum(c, d), jnp.minimum(c, d)
        top1 = jnp.maximum(hi1, hi2)
        top2 = jnp.maximum(jnp.minimum(hi1, hi2), jnp.maximum(lo1, lo2))
        scores.append(top1 + top2)
    best = jnp.zeros_like(scores[0], dtype=jnp.int32)
    best_score = scores[0]
    for g in range(1, N_GROUPS):
        better = scores[g] > best_score
        best = jnp.where(better, g, best)
        best_score = jnp.where(better, scores[g], best_score)
    sel = []
    for e in range(N_EXPERTS):
        g = e // EXPERTS_PER_GROUP
        rank = jnp.zeros_like(best)
        for j in range(g * EXPERTS_PER_GROUP, (g + 1) * EXPERTS_PER_GROUP):
            if j == e:
                continue
            ahead = (p[j] >= p[e]) if j < e else (p[j] > p[e])
            rank = rank + ahead.astype(jnp.int32)
        sel.append((best == g) & (rank < 2))
    kept = [jnp.where(sel[e], p[e], 0.0) for e in range(N_EXPERTS)]
    denom = kept[0]
    for e in range(1, N_EXPERTS):
        denom = denom + kept[e]
    return sel, [kp / denom for kp in kept]


def _merge_kernel(x_ref, ya_ref, yb_ref, ga_ref, gb_ref, mod_ref, wpa_ref, wpb_ref, wo_ref,
                  g1_ref, b1_ref, wrh_ref, wrl_ref, br_ref, x1_ref, h2_ref, route_ref, counts_ref,
                  *, d_model, alpha):
    d = d_model
    tm = x_ref.shape[1]
    gate1 = mod_ref[0, :, 2 * d:3 * d]
    shift2 = mod_ref[0, :, 3 * d:4 * d]
    scale2 = mod_ref[0, :, 4 * d:5 * d]
    a = jnp.dot(ya_ref[0], wpa_ref[...], preferred_element_type=F32)
    b = jnp.dot(yb_ref[0], wpb_ref[...], preferred_element_type=F32)
    merged = (ga_ref[0].astype(F32) * a + gb_ref[0].astype(F32) * b).astype(BF16)
    mix = jnp.dot(merged, wo_ref[...], preferred_element_type=F32)
    x1 = _ln_rows(alpha * x_ref[0] + gate1 * mix) * g1_ref[...] + b1_ref[...]
    x1_ref[0] = x1
    h2 = _ln_rows(x1) * (1.0 + scale2) + shift2
    h2_ref[0, :, 0:d] = h2
    h2_hi, h2_lo = _split_bf16(h2)
    logits = (jnp.dot(h2_hi, wrh_ref[...], preferred_element_type=F32)
              + jnp.dot(h2_lo, wrh_ref[...], preferred_element_type=F32)
              + jnp.dot(h2_hi, wrl_ref[...], preferred_element_type=F32))
    logits_t = logits.T[0:N_EXPERTS, :] + jnp.tile(br_ref[...], (1, tm // br_ref.shape[1]))
    sel, gate_rows = _route_rows(logits_t)

    e_lo = jnp.full((1, tm), N_EXPERTS, jnp.int32)
    e_hi = jnp.full((1, tm), -1, jnp.int32)
    for e in range(N_EXPERTS):
        e_lo = jnp.where(sel[e], jnp.minimum(e_lo, e), e_lo)
        e_hi = jnp.where(sel[e], jnp.maximum(e_hi, e), e_hi)
    g_lo = jnp.zeros((1, tm), F32)
    g_hi = jnp.zeros((1, tm), F32)
    for e in range(N_EXPERTS):
        g_lo = jnp.where(e_lo == e, gate_rows[e], g_lo)
        g_hi = jnp.where(e_hi == e, gate_rows[e], g_hi)
    a = e_lo & (EXPERTS_PER_GROUP - 1)
    b = e_hi & (EXPERTS_PER_GROUP - 1)
    cls = (e_lo >> 2) * PAIRS_PER_GROUP + ((a * (7 - a)) >> 1) + (b - a - 1)
    cls_t = jnp.where(lax.broadcasted_iota(jnp.int32, (CLASS_ROWS, tm), 0) == cls, 1.0, 0.0)
    earlier = (lax.broadcasted_iota(jnp.int32, (tm, tm), 0)
               < lax.broadcasted_iota(jnp.int32, (tm, tm), 1))
    rank_t = jnp.dot(cls_t.astype(BF16), jnp.where(earlier, 1.0, 0.0).astype(BF16),
                     preferred_element_type=F32)
    rank = jnp.sum(cls_t * rank_t, axis=0, keepdims=True)
    route_ref[0] = jnp.concatenate(
        [cls, rank.astype(jnp.int32), jnp.zeros((ROUTE_ROWS - 2, tm), jnp.int32)], axis=0)
    h2_ref[0, :, d:d + GATE_LANES] = jnp.concatenate(
        [g_lo, g_hi, jnp.zeros((GATE_LANES - 2, tm), F32)], axis=0).T
    counts = jnp.sum(cls_t, axis=1, keepdims=True)
    counts_ref[0, 0] = jnp.broadcast_to(counts, (CLASS_ROWS, 128)).astype(jnp.int32)


def _merge(x, ya, yb, ga, gb, mod, w_pa, w_pb, w_o, g1, b1, w_r_hi, w_r_lo, b_r_b, *, alpha):
    bsz, n, d = x.shape
    tm = min(TOKEN_TILE, n)
    nt = n // tm
    tok = lambda b, i: (b, i, 0)
    return pl.pallas_call(
        functools.partial(_merge_kernel, d_model=d, alpha=alpha),
        grid=(bsz, nt),
        in_specs=[
            pl.BlockSpec((1, tm, d), tok),
            pl.BlockSpec((1, tm, A_WIDTH), tok),
            pl.BlockSpec((1, tm, HEAD_COLS), tok),
            pl.BlockSpec((1, tm, d), tok),
            pl.BlockSpec((1, tm, d), tok),
            pl.BlockSpec((1, 1, N_MOD * d), lambda b, i: (b, 0, 0)),
            _resident((A_WIDTH, d)),
            _resident((HEAD_COLS, d)),
            _resident((d, d)),
            _resident((1, d)),
            _resident((1, d)),
            _resident((d, GATE_LANES)),
            _resident((d, GATE_LANES)),
            _resident((N_EXPERTS, 128)),
        ],
        out_specs=[
            pl.BlockSpec((1, tm, d), tok),
            pl.BlockSpec((1, tm, d + GATE_LANES), tok),
            pl.BlockSpec((1, ROUTE_ROWS, tm), lambda b, i: (b, 0, i)),
            pl.BlockSpec((1, 1, CLASS_ROWS, 128), lambda b, i: (b, i, 0, 0)),
        ],
        out_shape=[
            jax.ShapeDtypeStruct((bsz, n, d), F32),
            jax.ShapeDtypeStruct((bsz, n, d + GATE_LANES), F32),
            jax.ShapeDtypeStruct((bsz, ROUTE_ROWS, n), jnp.int32),
            jax.ShapeDtypeStruct((bsz, nt, CLASS_ROWS, 128), jnp.int32),
        ],
        compiler_params=_params(("parallel", "parallel")),
        name="merge_route",
    )(x, ya, yb, ga, gb, mod, w_pa, w_pb, w_o, g1, b1, w_r_hi, w_r_lo, b_r_b)


def _dispatch_plan(routes, counts_list, n_tiles_max):
    tm = MOE_TOKEN_TILE
    n_srcs = [c.shape[0] * c.shape[1] for c in counts_list]
    n_src = sum(n_srcs)
    cnt = jnp.concatenate([c[..., 0].reshape(-1, CLASS_ROWS) for c in counts_list], axis=0)
    cls = jnp.concatenate([r[:, 0].reshape(-1, tm) for r in routes], axis=0)
    rank = jnp.concatenate([r[:, 1].reshape(-1, tm) for r in routes], axis=0)
    total = jnp.sum(cnt, axis=0)
    tiles_per_class = (total + MOE_ROW_TILE - 1) // MOE_ROW_TILE
    classes = jnp.arange(CLASS_ROWS, dtype=jnp.int32)
    tile_end = jnp.sum(jnp.where(classes[None, :] <= classes[:, None], tiles_per_class[None, :], 0), axis=1)
    seg_start = (tile_end - tiles_per_class) * MOE_ROW_TILE
    src = jnp.arange(n_src, dtype=jnp.int32)
    before = jnp.sum(jnp.where((src[None, :] < src[:, None])[:, :, None], cnt[None, :, :], 0), axis=1)
    base = (seg_start[None, :] + before)[:, None, :]
    pos = jnp.sum(jnp.where(cls[..., None] == classes, base, 0), axis=-1) + rank
    tile_ids = jnp.arange(n_tiles_max, dtype=jnp.int32)
    tile_class = jnp.sum((tile_end[None, :] <= tile_ids[:, None]).astype(jnp.int32), axis=1)
    tile_class = jnp.minimum(tile_class, len(PAIR_CLASSES) - 1)
    pairs = jnp.asarray(PAIR_CLASSES, dtype=jnp.int32)
    tile_pair = jnp.sum(jnp.where((tile_class[:, None] == classes[None, :len(PAIR_CLASSES)])[:, :, None],
                                  pairs[None, :, :], 0), axis=1)
    pos_streams, start = [], 0
    for n_s in n_srcs:
        pos_streams.append(pos[start:start + n_s].reshape(-1))
        start += n_s
    return pos_streams, tile_pair[:, 0], tile_pair[:, 1], tile_end[-1:]


def _dispatch_kernel(pos_ref, h_ref, hs_in_hbm, hs_hbm, sem, *, tile):
    del hs_in_hbm

    def row_copy(t, dst_row):
        return pltpu.make_async_copy(h_ref.at[0, pl.ds(t, 1), :], hs_hbm.at[pl.ds(dst_row, 1), :], sem)

    def issue(t, carry):
        row_copy(t, pos_ref[0, 0, t]).start()
        return carry
    lax.fori_loop(0, tile, issue, 0, unroll=8)

    def drain(t, carry):
        row_copy(0, 0).wait()
        return carry
    lax.fori_loop(0, tile, drain, 0, unroll=8)


def _dispatch(h2, pos, hs_init):
    bsz, n, width = h2.shape
    n_rows = hs_init.shape[0]
    tile = min(DISPATCH_TILE, n)
    nt = n // tile
    pos_tiles = pos.reshape(bsz * nt, 1, tile)
    return pl.pallas_call(
        functools.partial(_dispatch_kernel, tile=tile),
        grid=(bsz, nt),
        in_specs=[
            pl.BlockSpec((1, 1, tile), lambda b, i: (b * nt + i, 0, 0), memory_space=pltpu.SMEM),
            pl.BlockSpec((1, tile, width), lambda b, i: (b, i, 0)),
            pl.BlockSpec(memory_space=pl.ANY),
        ],
        out_specs=pl.BlockSpec(memory_space=pl.ANY),
        out_shape=jax.ShapeDtypeStruct((n_rows, width), F32),
        scratch_shapes=[pltpu.SemaphoreType.DMA(())],
        input_output_aliases={2: 0},
        compiler_params=_params(("arbitrary", "arbitrary")),
        name="moe_dispatch",
    )(pos_tiles, h2, hs_init)


def _experts_kernel(lo_ref, hi_ref, nv_ref, hs_ref, w1l_ref, w3l_ref, w2l_ref, w1h_ref, w3h_ref, w2h_ref,
                    ys_ref, w1l_b, w3l_b, w2l_b, w1h_b, w3h_b, w2h_b, *, d_model):
    d = d_model
    j = pl.program_id(0)
    prev = jnp.maximum(j - 1, 0)

    @pl.when((j == 0) | (lo_ref[j] != lo_ref[prev]))
    def _():
        w1l_b[...] = w1l_ref[0, 0].astype(BF16)
        w3l_b[...] = w3l_ref[0, 0].astype(BF16)
        w2l_b[...] = w2l_ref[0, 0].astype(BF16)

    @pl.when((j == 0) | (hi_ref[j] != hi_ref[prev]))
    def _():
        w1h_b[...] = w1h_ref[0, 0].astype(BF16)
        w3h_b[...] = w3h_ref[0, 0].astype(BF16)
        w2h_b[...] = w2h_ref[0, 0].astype(BF16)

    def expert(h, w1, w3, w2):
        a = jnp.dot(h, w1[...], preferred_element_type=F32)
        b = jnp.dot(h, w3[...], preferred_element_type=F32)
        hid = (a * jax.nn.sigmoid(a) * b).astype(BF16)
        return jnp.dot(hid, w2[...], preferred_element_type=F32)

    @pl.when(j < nv_ref[0])
    def _():
        h = hs_ref[:, 0:d].astype(BF16)
        gates = hs_ref[:, d:d + GATE_LANES]
        ys_ref[...] = (gates[:, 0:1] * expert(h, w1l_b, w3l_b, w2l_b)
                       + gates[:, 1:2] * expert(h, w1h_b, w3h_b, w2h_b))

    @pl.when(j >= nv_ref[0])
    def _():
        ys_ref[...] = jnp.zeros(ys_ref.shape, F32)


def _experts(hs, tile_lo, tile_hi, n_valid, w1, w3, w2, *, layer):
    n_rows, width = hs.shape
    d = width - GATE_LANES
    de = w1.shape[-1]
    n_tiles = n_rows // MOE_ROW_TILE
    lo_map = lambda j, lo, hi, nv: (layer, lo[j], 0, 0)
    hi_map = lambda j, lo, hi, nv: (layer, hi[j], 0, 0)
    up, down = (1, 1, d, de), (1, 1, de, d)
    return pl.pallas_call(
        functools.partial(_experts_kernel, d_model=d),
        grid_spec=pltpu.PrefetchScalarGridSpec(
            num_scalar_prefetch=3,
            grid=(n_tiles,),
            in_specs=[
                pl.BlockSpec((MOE_ROW_TILE, width), lambda j, lo, hi, nv: (jnp.minimum(j, nv[0] - 1), 0)),
                pl.BlockSpec(up, lo_map), pl.BlockSpec(up, lo_map), pl.BlockSpec(down, lo_map),
                pl.BlockSpec(up, hi_map), pl.BlockSpec(up, hi_map), pl.BlockSpec(down, hi_map),
            ],
            out_specs=pl.BlockSpec((MOE_ROW_TILE, d), lambda j, lo, hi, nv: (j, 0)),
            scratch_shapes=[pltpu.VMEM((d, de), BF16), pltpu.VMEM((d, de), BF16), pltpu.VMEM((de, d), BF16)] * 2,
        ),
        out_shape=jax.ShapeDtypeStruct((n_rows, d), F32),
        compiler_params=_params(("arbitrary",)),
        name="moe_experts",
    )(tile_lo, tile_hi, n_valid, hs, w1, w3, w2, w1, w3, w2)


def _combine_kernel(pos_ref, nxt_ref, x1_ref, mod_ref, g2_ref, b2_ref, ys_hbm,
                    o_ref, ybuf, sem, *, tile, d_model, alpha):
    d = d_model
    step = pl.program_id(0) * pl.num_programs(1) + pl.program_id(1)
    n_steps = pl.num_programs(0) * pl.num_programs(1)
    slot = step % 2

    def row_copy(src_row, slot_, dst_row):
        return pltpu.make_async_copy(ys_hbm.at[pl.ds(src_row, 1), :],
                                     ybuf.at[slot_, pl.ds(dst_row, 1), :], sem.at[slot_])

    def gather(p_ref, slot_):
        def issue(t, carry):
            row_copy(p_ref[0, 0, t], slot_, t).start()
            return carry
        lax.fori_loop(0, tile, issue, 0, unroll=8)

    @pl.when(step == 0)
    def _():
        gather(pos_ref, 0)

    @pl.when(step + 1 < n_steps)
    def _():
        gather(nxt_ref, 1 - slot)

    def drain(t, carry):
        row_copy(0, slot, 0).wait()
        return carry
    lax.fori_loop(0, tile, drain, 0, unroll=8)

    gate2 = mod_ref[0, :, 5 * d:6 * d]
    o_ref[0] = _ln_rows(alpha * x1_ref[0] + gate2 * ybuf[slot]) * g2_ref[...] + b2_ref[...]


def _combine(ys, pos, x1, mod, g2, b2, *, alpha):
    bsz, n, d = x1.shape
    tile = min(COMBINE_TILE, n)
    nt = n // tile
    n_steps = bsz * nt
    pos_tiles = pos.reshape(n_steps, 1, tile)
    tok = lambda b, i: (b, i, 0)
    smem_tile = lambda index_map: pl.BlockSpec((1, 1, tile), index_map, memory_space=pltpu.SMEM)
    return pl.pallas_call(
        functools.partial(_combine_kernel, tile=tile, d_model=d, alpha=alpha),
        grid=(bsz, nt),
        in_specs=[
            smem_tile(lambda b, i: (b * nt + i, 0, 0)),
            smem_tile(lambda b, i: (jnp.minimum(b * nt + i + 1, n_steps - 1), 0, 0)),
            pl.BlockSpec((1, tile, d), tok),
            pl.BlockSpec((1, 1, N_MOD * d), lambda b, i: (b, 0, 0)),
            pl.BlockSpec((1, d), lambda b, i: (0, 0)),
            pl.BlockSpec((1, d), lambda b, i: (0, 0)),
            pl.BlockSpec(memory_space=pl.ANY),
        ],
        out_specs=pl.BlockSpec((1, tile, d), tok),
        out_shape=jax.ShapeDtypeStruct((bsz, n, d), F32),
        scratch_shapes=[pltpu.VMEM((2, tile, d), F32), pltpu.SemaphoreType.DMA((2,))],
        compiler_params=_params(("arbitrary", "arbitrary")),
        name="moe_combine",
    )(pos_tiles, pos_tiles, x1, mod, g2, b2, ys)


def _moe(streams, hs, w1, w3, w2, g2, b2, *, alpha, layer):
    pos_streams, tile_lo, tile_hi, n_valid = _dispatch_plan(
        [s[1] for s in streams], [s[2] for s in streams], hs.shape[0] // MOE_ROW_TILE)
    for (h2g, _, _, _, _), pos in zip(streams, pos_streams):
        hs = _dispatch(h2g, pos, hs)
    ys = _experts(hs, tile_lo, tile_hi, n_valid, w1, w3, w2, layer=layer)
    outs = [_combine(ys, pos, x1, mod, g2, b2, alpha=alpha)
            for (_, _, _, x1, mod), pos in zip(streams, pos_streams)]
    return outs, hs


def _rope_tables(n_tokens):
    rows = n_tokens // GRID_W
    row = jnp.repeat(jnp.arange(rows, dtype=F32), GRID_W)
    col = jnp.tile(jnp.arange(GRID_W, dtype=F32), rows)
    n_freq = QK_DIM // 4
    inv_freq = ROPE_BASE ** (-jnp.arange(n_freq, dtype=F32) / n_freq)
    ang_r = row[:, None] * inv_freq
    ang_c = col[:, None] * inv_freq
    cos64 = jnp.concatenate([jnp.cos(ang_r), jnp.cos(ang_r), jnp.cos(ang_c), jnp.cos(ang_c)], axis=-1)
    sin64 = jnp.concatenate([-jnp.sin(ang_r), jnp.sin(ang_r), -jnp.sin(ang_c), jnp.sin(ang_c)], axis=-1)
    return jnp.tile(cos64, (1, 2)), jnp.tile(sin64, (1, 2))


def kernel(x, c, ctx, c_ctx, w_mod, b_mod, w_in, sgu_g, sgu_b, w_s, b_s, lambda_q, lambda_k, subln_g,
           w_pa, w_pb, w_o, ln1_g, ln1_b, w_router, b_router, w1, w3, w2, ln2_g, ln2_b):
    bsz, n_lat, d = x.shape
    depth = w_mod.shape[0]
    alpha = (2.0 * depth) ** 0.25
    cos_t, sin_t = _rope_tables(n_lat)

    cond = jnp.zeros((8, d), F32).at[0:bsz].set(c).at[bsz].set(c_ctx)
    w_r_pad = jnp.pad(w_router, ((0, 0), (0, GATE_LANES - N_EXPERTS)))
    w_r_hi, w_r_lo = _split_bf16(w_r_pad)
    b_r_b = jnp.broadcast_to(b_router[:, None], (N_EXPERTS, 128))
    row = lambda v: v.reshape(1, -1)
    n_moe_rows = bsz * (n_lat + ctx.shape[1]) + len(PAIR_CLASSES) * MOE_ROW_TILE
    moe_rows = jnp.zeros((n_moe_rows, d + GATE_LANES), F32)

    for l in range(depth):
        last = l == depth - 1
        lam_init = 0.8 - 0.6 * math.exp(-0.3 * l)
        mod = _modulation(cond, w_mod[l], b_mod[l])
        mod_lat = mod[0:bsz, None, :]
        mod_ctx = jnp.broadcast_to(mod[bsz][None, None, :], (bsz, 1, N_MOD * d))
        w_in_l = w_in[l].astype(BF16)
        w_s_l = w_s[l].astype(BF16)
        b_s_b = jnp.broadcast_to(b_s[l][:, :, None], (A_GROUPS, CHUNK, CHUNK))
        proj_args = (w_in_l, row(sgu_g[l]), row(sgu_b[l]), w_s_l, b_s_b, cos_t, sin_t)
        attn_args = (lambda_q[l], lambda_k[l], row(subln_g[l]))
        merge_w = (w_pa[l].astype(BF16), w_pb[l].astype(BF16), w_o[l].astype(BF16),
                   row(ln1_g[l]), row(ln1_b[l]), w_r_hi, w_r_lo, b_r_b)
        moe_w = (w1, w3, w2, row(ln2_g[l]), row(ln2_b[l]))

        ya, qt, k, vt, ga, gb = _inproj(x, mod_lat, *proj_args, use_rope=True)
        cya, cqt, ck, cvt, cga, cgb = _inproj(ctx, mod_ctx, *proj_args, use_rope=False)
        yb = _attention(qt, k, vt, ck, cvt, *attn_args, lam_init=lam_init)
        x1, h2g, route, counts = _merge(x, ya, yb, ga, gb, mod_lat, *merge_w, alpha=alpha)
        streams = [(h2g, route, counts, x1, mod_lat)]
        if not last:
            cyb = _attention(cqt, None, None, ck, cvt, *attn_args, lam_init=lam_init)
            c1, ch2g, croute, ccounts = _merge(ctx, cya, cyb, cga, cgb, mod_ctx, *merge_w, alpha=alpha)
            streams.append((ch2g, croute, ccounts, c1, mod_ctx))
        outs, moe_rows = _moe(streams, moe_rows, *moe_w, alpha=alpha, layer=l)
        x = outs[0]
        if not last:
            ctx = outs[1]
    return x
```

```python
import functools
import math

import jax
import jax.numpy as jnp
from jax import lax
from jax.experimental import pallas as pl
from jax.experimental.pallas import tpu as pltpu

F32 = jnp.float32
BF16 = jnp.bfloat16

GRID_W = 64
CHUNK = 128
A_GROUPS = 8
A_WIDTH = 1024
HEADS = 8
QK_DIM = 64
V_DIM = 2 * QK_DIM
HEAD_COLS = HEADS * V_DIM
ROPE_BASE = 10000.0
N_EXPERTS = 16
N_GROUPS = 4
EXPERTS_PER_GROUP = N_EXPERTS // N_GROUPS
N_MOD = 6
PAIR_CLASSES = tuple((4 * g + a, 4 * g + b) for g in range(N_GROUPS)
                     for a in range(EXPERTS_PER_GROUP) for b in range(a + 1, EXPERTS_PER_GROUP))
PAIRS_PER_GROUP = len(PAIR_CLASSES) // N_GROUPS
CLASS_ROWS = 32
ROUTE_ROWS = 8
GATE_LANES = 128
LN_EPS = 1e-5
RMS_EPS = 1e-5
NEG_BIG = -1e30

VMEM_LIMIT_BYTES = 56 * 1024 * 1024

TOKEN_TILE = 256
ATTN_Q_TILE = 512
ATTN_KV_CHUNK = 512
ATTN_UNROLL = 6
SUM_ROWS = 16
MOE_TOKEN_TILE = TOKEN_TILE
DISPATCH_TILE = 4096
COMBINE_TILE = 1024
MOE_ROW_TILE = 256


def _params(semantics):
    return pltpu.CompilerParams(dimension_semantics=semantics, vmem_limit_bytes=VMEM_LIMIT_BYTES)


def _resident(shape):
    nd = len(shape)
    return pl.BlockSpec(shape, lambda *_: (0,) * nd, pipeline_mode=pl.Buffered(1))


def _split_bf16(x):
    bits = lax.bitcast_convert_type(x, jnp.uint32) & jnp.uint32(0xFFFF0000)
    hi = lax.bitcast_convert_type(bits, F32)
    return hi.astype(BF16), (x - hi).astype(BF16)


def _ln_rows(x):
    mu = jnp.mean(x, axis=-1, keepdims=True)
    xc = x - mu
    var = jnp.mean(xc * xc, axis=-1, keepdims=True)
    return xc * lax.rsqrt(var + LN_EPS)


def _mod_kernel(cond_ref, w_ref, b_ref, o_ref):
    c = cond_ref[...]
    s = c * jax.nn.sigmoid(c)
    o_ref[...] = jnp.dot(s, w_ref[...], preferred_element_type=F32, precision=lax.Precision.HIGHEST) + b_ref[...]


def _modulation(cond, w_mod, b_mod):
    rows, d = cond.shape
    return pl.pallas_call(
        _mod_kernel,
        grid=(N_MOD,),
        in_specs=[
            pl.BlockSpec((rows, d), lambda j: (0, 0)),
            pl.BlockSpec((d, d), lambda j: (0, j)),
            pl.BlockSpec((1, d), lambda j: (0, j)),
        ],
        out_specs=pl.BlockSpec((rows, d), lambda j: (0, j)),
        out_shape=jax.ShapeDtypeStruct((rows, N_MOD * d), F32),
        compiler_params=_params(("arbitrary",)),
        name="modulation",
    )(cond, w_mod, b_mod.reshape(1, -1))


def _rope(x, cos, sin_signed):
    n = x.shape[-1]
    reps = n // cos.shape[-1]
    c = jnp.tile(cos, (1, reps))
    s = jnp.tile(sin_signed, (1, reps))
    lane = lax.broadcasted_iota(jnp.int32, x.shape, 1)
    first = (lane & 31) < 16
    partner = jnp.where(first, pltpu.roll(x, n - 16, 1), pltpu.roll(x, 16, 1))
    return x * c + partner * s


def _inproj_kernel(x_ref, mod_ref, w_ref, sg_ref, sb_ref, ws_ref, bs_ref, cos_ref, sin_ref,
                   ya_ref, qt_ref, k_ref, vt_ref, ga_ref, gb_ref, *, d_model, use_rope, q_scale):
    d = d_model
    tm = x_ref.shape[1]
    shift = mod_ref[0, :, 0:d]
    scale = mod_ref[0, :, d:2 * d]
    h = (_ln_rows(x_ref[0]) * (1.0 + scale) + shift).astype(BF16)

    def proj(c0, width):
        return jnp.dot(h, w_ref[:, c0:c0 + width], preferred_element_type=F32)

    c_u, c_v = 0, A_WIDTH
    c_q = 2 * A_WIDTH
    c_k = c_q + HEAD_COLS
    c_vb = c_k + HEAD_COLS
    c_ga = c_vb + HEAD_COLS
    c_gb = c_ga + d

    u = jax.nn.gelu(proj(c_u, A_WIDTH))
    v = jax.nn.gelu(proj(c_v, A_WIDTH))
    vn = (_ln_rows(v) * sg_ref[...] + sb_ref[...]).astype(BF16)
    gd = A_WIDTH // A_GROUPS
    for pair in range(tm // (2 * CHUNK)):
        r0 = pair * 2 * CHUNK
        r1 = r0 + CHUNK
        for g in range(A_GROUPS):
            cols = slice(g * gd, (g + 1) * gd)
            rhs = jnp.concatenate([vn[r0:r0 + CHUNK, cols], vn[r1:r1 + CHUNK, cols]], axis=1)
            y = jnp.dot(ws_ref[g], rhs, preferred_element_type=F32)
            bias = bs_ref[g]
            ya_ref[0, r0:r0 + CHUNK, cols] = (u[r0:r0 + CHUNK, cols] * (y[:, :gd] + bias)).astype(BF16)
            ya_ref[0, r1:r1 + CHUNK, cols] = (u[r1:r1 + CHUNK, cols] * (y[:, gd:] + bias)).astype(BF16)

    q = proj(c_q, HEAD_COLS)
    k = proj(c_k, HEAD_COLS)
    if use_rope:
        q = _rope(q, cos_ref[...], sin_ref[...])
        k = _rope(k, cos_ref[...], sin_ref[...])
    q = q * q_scale
    k_ref[0] = k.astype(BF16)
    vb = proj(c_vb, HEAD_COLS)
    for hd in range(HEADS):
        cols = slice(hd * V_DIM, (hd + 1) * V_DIM)
        qt_ref[0, hd] = q[:, cols].T.astype(BF16)
        vt_ref[0, hd] = vb[:, cols].T.astype(BF16)

    ga_ref[0] = jax.nn.sigmoid(proj(c_ga, d)).astype(BF16)
    gb_ref[0] = jax.nn.sigmoid(proj(c_gb, d)).astype(BF16)


def _inproj(x, mod, w_in, sgu_g, sgu_b, w_s, b_s_b, cos_t, sin_t, *, use_rope):
    bsz, n, d = x.shape
    tm = min(TOKEN_TILE, n)
    nt = n // tm
    in_cols = w_in.shape[1]
    q_scale = (QK_DIM ** -0.5) * math.log2(math.e)
    tok = lambda b, i: (b, i, 0)
    tr = lambda b, i: (b, 0, 0, i)
    rope_map = (lambda b, i: (i, 0)) if use_rope else (lambda b, i: (0, 0))
    return pl.pallas_call(
        functools.partial(_inproj_kernel, d_model=d, use_rope=use_rope, q_scale=q_scale),
        grid=(bsz, nt),
        in_specs=[
            pl.BlockSpec((1, tm, d), tok),
            pl.BlockSpec((1, 1, N_MOD * d), lambda b, i: (b, 0, 0)),
            _resident((d, in_cols)),
            _resident((1, A_WIDTH)),
            _resident((1, A_WIDTH)),
            _resident((A_GROUPS, CHUNK, CHUNK)),
            _resident((A_GROUPS, CHUNK, CHUNK)),
            pl.BlockSpec((tm, 2 * QK_DIM), rope_map),
            pl.BlockSpec((tm, 2 * QK_DIM), rope_map),
        ],
        out_specs=[
            pl.BlockSpec((1, tm, A_WIDTH), tok),
            pl.BlockSpec((1, HEADS, V_DIM, tm), tr),
            pl.BlockSpec((1, tm, HEAD_COLS), tok),
            pl.BlockSpec((1, HEADS, V_DIM, tm), tr),
            pl.BlockSpec((1, tm, d), tok),
            pl.BlockSpec((1, tm, d), tok),
        ],
        out_shape=[
            jax.ShapeDtypeStruct((bsz, n, A_WIDTH), BF16),
            jax.ShapeDtypeStruct((bsz, HEADS, V_DIM, n), BF16),
            jax.ShapeDtypeStruct((bsz, n, HEAD_COLS), BF16),
            jax.ShapeDtypeStruct((bsz, HEADS, V_DIM, n), BF16),
            jax.ShapeDtypeStruct((bsz, n, d), BF16),
            jax.ShapeDtypeStruct((bsz, n, d), BF16),
        ],
        compiler_params=_params(("parallel", "parallel")),
        name="inproj_rope" if use_rope else "inproj_ctx",
    )(x, mod, w_in, sgu_g, sgu_b, w_s, b_s_b, cos_t, sin_t)


def _attn_kernel(*refs, n_lat_chunks, tk, lam_init):
    if n_lat_chunks:
        (qt_ref, kl_ref, vtl_ref, kc_ref, vtc_ref, lq_ref, lk_ref, sg_ref,
         o_ref, rhs_ref, s_ref, cm_ref, m_ref, acc_ref) = refs
    else:
        (qt_ref, kc_ref, vtc_ref, lq_ref, lk_ref, sg_ref,
         o_ref, rhs_ref, s_ref, cm_ref, m_ref, acc_ref) = refs
    tq = qt_ref.shape[-1]
    nc = kc_ref.shape[1]

    qt = qt_ref[0, 0]
    row = lax.broadcasted_iota(jnp.int32, qt.shape, 0)
    zero = jnp.zeros_like(qt)
    rhs_ref[:, 0:tq] = jnp.where(row < QK_DIM, qt, zero)
    rhs_ref[:, tq:2 * tq] = jnp.where(row >= QK_DIM, qt, zero)
    m_ref[...] = jnp.full(m_ref.shape, NEG_BIG, F32)
    acc_ref[...] = jnp.zeros(acc_ref.shape, F32)

    def stage_a(kc, slot, rows):
        s = jnp.dot(kc, rhs_ref[...], preferred_element_type=F32)
        s_ref[slot, 0:rows, :] = s
        cm_ref[slot] = jnp.max(s, axis=0, keepdims=True)

    def stage_b(vtc, slot, rows):
        m_old = m_ref[...]
        m_new = jnp.maximum(m_old, cm_ref[slot])
        alpha = jnp.exp2(m_old - m_new)
        p = jnp.exp2(s_ref[slot, 0:rows, :] - m_new)
        vt_ones = jnp.concatenate([vtc, jnp.ones((SUM_ROWS, rows), BF16)], axis=0)
        acc_ref[...] = alpha * acc_ref[...] + jnp.dot(vt_ones, p.astype(BF16), preferred_element_type=F32)
        m_ref[...] = m_new

    if n_lat_chunks:
        def lat_k(c):
            return kl_ref[0, pl.ds(pl.multiple_of(c * tk, tk), tk), :]

        def lat_vt(c):
            return vtl_ref[0, 0, :, pl.ds(pl.multiple_of(c * tk, tk), tk)]

        stage_a(lat_k(0), 0, tk)

        def pipelined(c0, n):
            for i in range(n):
                c = c0 + i
                if isinstance(c, int) and c + 1 == n_lat_chunks:
                    stage_a(kc_ref[0], (i + 1) % 2, nc)
                else:
                    stage_a(lat_k(c + 1), (i + 1) % 2, tk)
                stage_b(lat_vt(c), i % 2, tk)

        def body(j, carry):
            pipelined(ATTN_UNROLL * j, ATTN_UNROLL)
            return carry
        n_loop = (n_lat_chunks - 2) // ATTN_UNROLL
        lax.fori_loop(0, n_loop, body, 0)
        pipelined(n_loop * ATTN_UNROLL, n_lat_chunks - n_loop * ATTN_UNROLL)
        stage_b(vtc_ref[0, 0], n_lat_chunks % 2, nc)
    else:
        stage_a(kc_ref[0], 0, nc)
        stage_b(vtc_ref[0, 0], 0, nc)

    lq = lq_ref[...]
    lk = lk_ref[...]
    lam = (jnp.exp(jnp.sum(lq[0:1] * lk[0:1], keepdims=True))
           - jnp.exp(jnp.sum(lq[1:2] * lk[1:2], keepdims=True)) + lam_init)
    acc = acc_ref[0:V_DIM, :]
    l = acc_ref[V_DIM:V_DIM + 1, :]
    o = acc[:, 0:tq] / l[:, 0:tq] - lam * (acc[:, tq:2 * tq] / l[:, tq:2 * tq])
    ms = jnp.mean(o * o, axis=0, keepdims=True)
    on = o * lax.rsqrt(ms + RMS_EPS)
    o_ref[0] = (on.T * sg_ref[...] * (1.0 - lam_init)).astype(BF16)


def _attention(qt, k_lat, vt_lat, k_ctx, vt_ctx, lq, lk, subln_g, *, lam_init):
    bsz, _, _, nq = qt.shape
    nc = k_ctx.shape[1]
    tq = min(ATTN_Q_TILE, nq)
    has_lat = k_lat is not None
    tk = ATTN_KV_CHUNK
    n_lat_chunks = (k_lat.shape[1] // tk) if has_lat else 0
    assert n_lat_chunks % 2 == 0 and ATTN_UNROLL % 2 == 0
    s_rows = tk if has_lat else nc
    in_specs = [pl.BlockSpec((1, 1, V_DIM, tq), lambda b, h, i: (b, h, 0, i))]
    args = [qt]
    if has_lat:
        ns = k_lat.shape[1]
        in_specs += [pl.BlockSpec((1, ns, V_DIM), lambda b, h, i: (b, 0, h)),
                     pl.BlockSpec((1, 1, V_DIM, ns), lambda b, h, i: (b, h, 0, 0))]
        args += [k_lat, vt_lat]
    in_specs += [pl.BlockSpec((1, nc, V_DIM), lambda b, h, i: (b, 0, h)),
                 pl.BlockSpec((1, 1, V_DIM, nc), lambda b, h, i: (b, h, 0, 0)),
                 pl.BlockSpec((2, QK_DIM), lambda b, h, i: (0, 0)),
                 pl.BlockSpec((2, QK_DIM), lambda b, h, i: (0, 0)),
                 pl.BlockSpec((1, V_DIM), lambda b, h, i: (0, 0))]
    args += [k_ctx, vt_ctx, lq, lk, subln_g]
    return pl.pallas_call(
        functools.partial(_attn_kernel, n_lat_chunks=n_lat_chunks, tk=tk, lam_init=lam_init),
        grid=(bsz, HEADS, nq // tq),
        in_specs=in_specs,
        out_specs=pl.BlockSpec((1, tq, V_DIM), lambda b, h, i: (b, i, h)),
        out_shape=jax.ShapeDtypeStruct((bsz, nq, HEAD_COLS), BF16),
        scratch_shapes=[
            pltpu.VMEM((V_DIM, 2 * tq), BF16),
            pltpu.VMEM((2, s_rows, 2 * tq), F32),
            pltpu.VMEM((2, 1, 2 * tq), F32),
            pltpu.VMEM((1, 2 * tq), F32),
            pltpu.VMEM((V_DIM + SUM_ROWS, 2 * tq), F32),
        ],
        compiler_params=_params(("parallel", "parallel", "arbitrary")),
        name="diff_attn_latent" if has_lat else "diff_attn_ctx",
    )(*args)


def _route_rows(logits_t):
    mx = jnp.max(logits_t, axis=0, keepdims=True)
    ex = jnp.exp(logits_t - mx)
    probs = ex / jnp.sum(ex, axis=0, keepdims=True)
    p = [probs[e:e + 1, :] for e in range(N_EXPERTS)]
    scores = []
    for g in range(N_GROUPS):
        a, b, c, d = p[4 * g:4 * g + 4]
        hi1, lo1 = jnp.maximum(a, b), jnp.minimum(a, b)
        hi2, lo2 = jnp.maximum(c, d), jnp.minimum(c, d)
        top1 = jnp.maximum(hi1, hi2)
        top2 = jnp.maximum(jnp.minimum(hi1, hi2), jnp.maximum(lo1, lo2))
        scores.append(top1 + top2)
    best = jnp.zeros_like(scores[0], dtype=jnp.int32)
    best_score = scores[0]
    for g in range(1, N_GROUPS):
        better = scores[g] > best_score
        best = jnp.where(better, g, best)
        best_score = jnp.where(better, scores[g], best_score)
    sel = []
    for e in range(N_EXPERTS):
        g = e // EXPERTS_PER_GROUP
        rank = jnp.zeros_like(best)
        for j in range(g * EXPERTS_PER_GROUP, (g + 1) * EXPERTS_PER_GROUP):
            if j == e:
                continue
            ahead = (p[j] >= p[e]) if j < e else (p[j] > p[e])
            rank = rank + ahead.astype(jnp.int32)
        sel.append((best == g) & (rank < 2))
    kept = [jnp.where(sel[e], p[e], 0.0) for e in range(N_EXPERTS)]
    denom = kept[0]
    for e in range(1, N_EXPERTS):
        denom = denom + kept[e]
    return sel, [kp / denom for kp in kept]


def _merge_kernel(x_ref, ya_ref, yb_ref, ga_ref, gb_ref, mod_ref, wpa_ref, wpb_ref, wo_ref,
                  g1_ref, b1_ref, wrh_ref, wrl_ref, br_ref, x1_ref, h2_ref, route_ref, counts_ref,
                  *, d_model, alpha):
    d = d_model
    tm = x_ref.shape[1]
    gate1 = mod_ref[0, :, 2 * d:3 * d]
    shift2 = mod_ref[0, :, 3 * d:4 * d]
    scale2 = mod_ref[0, :, 4 * d:5 * d]
    a = jnp.dot(ya_ref[0], wpa_ref[...], preferred_element_type=F32)
    b = jnp.dot(yb_ref[0], wpb_ref[...], preferred_element_type=F32)
    merged = (ga_ref[0].astype(F32) * a + gb_ref[0].astype(F32) * b).astype(BF16)
    mix = jnp.dot(merged, wo_ref[...], preferred_element_type=F32)
    x1 = _ln_rows(alpha * x_ref[0] + gate1 * mix) * g1_ref[...] + b1_ref[...]
    x1_ref[0] = x1
    h2 = _ln_rows(x1) * (1.0 + scale2) + shift2
    h2_ref[0, :, 0:d] = h2
    h2_hi, h2_lo = _split_bf16(h2)
    logits = (jnp.dot(h2_hi, wrh_ref[...], preferred_element_type=F32)
              + jnp.dot(h2_lo, wrh_ref[...], preferred_element_type=F32)
              + jnp.dot(h2_hi, wrl_ref[...], preferred_element_type=F32))
    logits_t = logits.T[0:N_EXPERTS, :] + jnp.tile(br_ref[...], (1, tm // br_ref.shape[1]))
    sel, gate_rows = _route_rows(logits_t)

    e_lo = jnp.full((1, tm), N_EXPERTS, jnp.int32)
    e_hi = jnp.full((1, tm), -1, jnp.int32)
    for e in range(N_EXPERTS):
        e_lo = jnp.where(sel[e], jnp.minimum(e_lo, e), e_lo)
        e_hi = jnp.where(sel[e], jnp.maximum(e_hi, e), e_hi)
    g_lo = jnp.zeros((1, tm), F32)
    g_hi = jnp.zeros((1, tm), F32)
    for e in range(N_EXPERTS):
        g_lo = jnp.where(e_lo == e, gate_rows[e], g_lo)
        g_hi = jnp.where(e_hi == e, gate_rows[e], g_hi)
    a = e_lo & (EXPERTS_PER_GROUP - 1)
    b = e_hi & (EXPERTS_PER_GROUP - 1)
    cls = (e_lo >> 2) * PAIRS_PER_GROUP + ((a * (7 - a)) >> 1) + (b - a - 1)
    cls_t = jnp.where(lax.broadcasted_iota(jnp.int32, (CLASS_ROWS, tm), 0) == cls, 1.0, 0.0)
    earlier = (lax.broadcasted_iota(jnp.int32, (tm, tm), 0)
               < lax.broadcasted_iota(jnp.int32, (tm, tm), 1))
    rank_t = jnp.dot(cls_t.astype(BF16), jnp.where(earlier, 1.0, 0.0).astype(BF16),
                     preferred_element_type=F32)
    rank = jnp.sum(cls_t * rank_t, axis=0, keepdims=True)
    route_ref[0] = jnp.concatenate(
        [cls, rank.astype(jnp.int32), jnp.zeros((ROUTE_ROWS - 2, tm), jnp.int32)], axis=0)
    h2_ref[0, :, d:d + GATE_LANES] = jnp.concatenate(
        [g_lo, g_hi, jnp.zeros((GATE_LANES - 2, tm), F32)], axis=0).T
    counts = jnp.sum(cls_t, axis=1, keepdims=True)
    counts_ref[0, 0] = jnp.broadcast_to(counts, (CLASS_ROWS, 128)).astype(jnp.int32)


def _merge(x, ya, yb, ga, gb, mod, w_pa, w_pb, w_o, g1, b1, w_r_hi, w_r_lo, b_r_b, *, alpha):
    bsz, n, d = x.shape
    tm = min(TOKEN_TILE, n)
    nt = n // tm
    tok = lambda b, i: (b, i, 0)
    return pl.pallas_call(
        functools.partial(_merge_kernel, d_model=d, alpha=alpha),
        grid=(bsz, nt),
        in_specs=[
            pl.BlockSpec((1, tm, d), tok),
            pl.BlockSpec((1, tm, A_WIDTH), tok),
            pl.BlockSpec((1, tm, HEAD_COLS), tok),
            pl.BlockSpec((1, tm, d), tok),
            pl.BlockSpec((1, tm, d), tok),
            pl.BlockSpec((1, 1, N_MOD * d), lambda b, i: (b, 0, 0)),
            _resident((A_WIDTH, d)),
            _resident((HEAD_COLS, d)),
            _resident((d, d)),
            _resident((1, d)),
            _resident((1, d)),
            _resident((d, GATE_LANES)),
            _resident((d, GATE_LANES)),
            _resident((N_EXPERTS, 128)),
        ],
        out_specs=[
            pl.BlockSpec((1, tm, d), tok),
            pl.BlockSpec((1, tm, d + GATE_LANES), tok),
            pl.BlockSpec((1, ROUTE_ROWS, tm), lambda b, i: (b, 0, i)),
            pl.BlockSpec((1, 1, CLASS_ROWS, 128), lambda b, i: (b, i, 0, 0)),
        ],
        out_shape=[
            jax.ShapeDtypeStruct((bsz, n, d), F32),
            jax.ShapeDtypeStruct((bsz, n, d + GATE_LANES), F32),
            jax.ShapeDtypeStruct((bsz, ROUTE_ROWS, n), jnp.int32),
            jax.ShapeDtypeStruct((bsz, nt, CLASS_ROWS, 128), jnp.int32),
        ],
        compiler_params=_params(("parallel", "parallel")),
        name="merge_route",
    )(x, ya, yb, ga, gb, mod, w_pa, w_pb, w_o, g1, b1, w_r_hi, w_r_lo, b_r_b)


def _dispatch_plan(routes, counts_list, n_tiles_max):
    tm = MOE_TOKEN_TILE
    n_srcs = [c.shape[0] * c.shape[1] for c in counts_list]
    n_src = sum(n_srcs)
    cnt = jnp.concatenate([c[..., 0].reshape(-1, CLASS_ROWS) for c in counts_list], axis=0)
    cls = jnp.concatenate([r[:, 0].reshape(-1, tm) for r in routes], axis=0)
    rank = jnp.concatenate([r[:, 1].reshape(-1, tm) for r in routes], axis=0)
    total = jnp.sum(cnt, axis=0)
    tiles_per_class = (total + MOE_ROW_TILE - 1) // MOE_ROW_TILE
    classes = jnp.arange(CLASS_ROWS, dtype=jnp.int32)
    tile_end = jnp.sum(jnp.where(classes[None, :] <= classes[:, None], tiles_per_class[None, :], 0), axis=1)
    seg_start = (tile_end - tiles_per_class) * MOE_ROW_TILE
    src = jnp.arange(n_src, dtype=jnp.int32)
    before = jnp.sum(jnp.where((src[None, :] < src[:, None])[:, :, None], cnt[None, :, :], 0), axis=1)
    base = (seg_start[None, :] + before)[:, None, :]
    pos = jnp.sum(jnp.where(cls[..., None] == classes, base, 0), axis=-1) + rank
    tile_ids = jnp.arange(n_tiles_max, dtype=jnp.int32)
    tile_class = jnp.sum((tile_end[None, :] <= tile_ids[:, None]).astype(jnp.int32), axis=1)
    tile_class = jnp.minimum(tile_class, len(PAIR_CLASSES) - 1)
    pairs = jnp.asarray(PAIR_CLASSES, dtype=jnp.int32)
    tile_pair = jnp.sum(jnp.where((tile_class[:, None] == classes[None, :len(PAIR_CLASSES)])[:, :, None],
                                  pairs[None, :, :], 0), axis=1)
    pos_streams, start = [], 0
    for n_s in n_srcs:
        pos_streams.append(pos[start:start + n_s].reshape(-1))
        start += n_s
    return pos_streams, tile_pair[:, 0], tile_pair[:, 1], tile_end[-1:]


def _dispatch_kernel(pos_ref, h_ref, hs_in_hbm, hs_hbm, sem, *, tile):
    del hs_in_hbm

    def row_copy(t, dst_row):
        return pltpu.make_async_copy(h_ref.at[0, pl.ds(t, 1), :], hs_hbm.at[pl.ds(dst_row, 1), :], sem)

    def issue(t, carry):
        row_copy(t, pos_ref[0, 0, t]).start()
        return carry
    lax.fori_loop(0, tile, issue, 0, unroll=8)

    def drain(t, carry):
        row_copy(0, 0).wait()
        return carry
    lax.fori_loop(0, tile, drain, 0, unroll=8)


def _dispatch(h2, pos, hs_init):
    bsz, n, width = h2.shape
    n_rows = hs_init.shape[0]
    tile = min(DISPATCH_TILE, n)
    nt = n // tile
    pos_tiles = pos.reshape(bsz * nt, 1, tile)
    return pl.pallas_call(
        functools.partial(_dispatch_kernel, tile=tile),
        grid=(bsz, nt),
        in_specs=[
            pl.BlockSpec((1, 1, tile), lambda b, i: (b * nt + i, 0, 0), memory_space=pltpu.SMEM),
            pl.BlockSpec((1, tile, width), lambda b, i: (b, i, 0)),
            pl.BlockSpec(memory_space=pl.ANY),
        ],
        out_specs=pl.BlockSpec(memory_space=pl.ANY),
        out_shape=jax.ShapeDtypeStruct((n_rows, width), F32),
        scratch_shapes=[pltpu.SemaphoreType.DMA(())],
        input_output_aliases={2: 0},
        compiler_params=_params(("arbitrary", "arbitrary")),
        name="moe_dispatch",
    )(pos_tiles, h2, hs_init)


def _experts_kernel(lo_ref, hi_ref, nv_ref, hs_ref, w1l_ref, w3l_ref, w2l_ref, w1h_ref, w3h_ref, w2h_ref,
                    ys_ref, w1l_b, w3l_b, w2l_b, w1h_b, w3h_b, w2h_b, *, d_model):
    d = d_model
    j = pl.program_id(0)
    prev = jnp.maximum(j - 1, 0)

    @pl.when((j == 0) | (lo_ref[j] != lo_ref[prev]))
    def _():
        w1l_b[...] = w1l_ref[0, 0].astype(BF16)
        w3l_b[...] = w3l_ref[0, 0].astype(BF16)
        w2l_b[...] = w2l_ref[0, 0].astype(BF16)

    @pl.when((j == 0) | (hi_ref[j] != hi_ref[prev]))
    def _():
        w1h_b[...] = w1h_ref[0, 0].astype(BF16)
        w3h_b[...] = w3h_ref[0, 0].astype(BF16)
        w2h_b[...] = w2h_ref[0, 0].astype(BF16)

    def expert(h, w1, w3, w2):
        a = jnp.dot(h, w1[...], preferred_element_type=F32)
        b = jnp.dot(h, w3[...], preferred_element_type=F32)
        hid = (a * jax.nn.sigmoid(a) * b).astype(BF16)
        return jnp.dot(hid, w2[...], preferred_element_type=F32)

    @pl.when(j < nv_ref[0])
    def _():
        h = hs_ref[:, 0:d].astype(BF16)
        gates = hs_ref[:, d:d + GATE_LANES]
        ys_ref[...] = (gates[:, 0:1] * expert(h, w1l_b, w3l_b, w2l_b)
                       + gates[:, 1:2] * expert(h, w1h_b, w3h_b, w2h_b))

    @pl.when(j >= nv_ref[0])
    def _():
        ys_ref[...] = jnp.zeros(ys_ref.shape, F32)


def _experts(hs, tile_lo, tile_hi, n_valid, w1, w3, w2, *, layer):
    n_rows, width = hs.shape
    d = width - GATE_LANES
    de = w1.shape[-1]
    n_tiles = n_rows // MOE_ROW_TILE
    lo_map = lambda j, lo, hi, nv: (layer, lo[j], 0, 0)
    hi_map = lambda j, lo, hi, nv: (layer, hi[j], 0, 0)
    up, down = (1, 1, d, de), (1, 1, de, d)
    return pl.pallas_call(
        functools.partial(_experts_kernel, d_model=d),
        grid_spec=pltpu.PrefetchScalarGridSpec(
            num_scalar_prefetch=3,
            grid=(n_tiles,),
            in_specs=[
                pl.BlockSpec((MOE_ROW_TILE, width), lambda j, lo, hi, nv: (jnp.minimum(j, nv[0] - 1), 0)),
                pl.BlockSpec(up, lo_map), pl.BlockSpec(up, lo_map), pl.BlockSpec(down, lo_map),
                pl.BlockSpec(up, hi_map), pl.BlockSpec(up, hi_map), pl.BlockSpec(down, hi_map),
            ],
            out_specs=pl.BlockSpec((MOE_ROW_TILE, d), lambda j, lo, hi, nv: (j, 0)),
            scratch_shapes=[pltpu.VMEM((d, de), BF16), pltpu.VMEM((d, de), BF16), pltpu.VMEM((de, d), BF16)] * 2,
        ),
        out_shape=jax.ShapeDtypeStruct((n_rows, d), F32),
        compiler_params=_params(("arbitrary",)),
        name="moe_experts",
    )(tile_lo, tile_hi, n_valid, hs, w1, w3, w2, w1, w3, w2)


def _combine_kernel(pos_ref, nxt_ref, x1_ref, mod_ref, g2_ref, b2_ref, ys_hbm,
                    o_ref, ybuf, sem, *, tile, d_model, alpha):
    d = d_model
    step = pl.program_id(0) * pl.num_programs(1) + pl.program_id(1)
    n_steps = pl.num_programs(0) * pl.num_programs(1)
    slot = step % 2

    def row_copy(src_row, slot_, dst_row):
        return pltpu.make_async_copy(ys_hbm.at[pl.ds(src_row, 1), :],
                                     ybuf.at[slot_, pl.ds(dst_row, 1), :], sem.at[slot_])

    def gather(p_ref, slot_):
        def issue(t, carry):
            row_copy(p_ref[0, 0, t], slot_, t).start()
            return carry
        lax.fori_loop(0, tile, issue, 0, unroll=8)

    @pl.when(step == 0)
    def _():
        gather(pos_ref, 0)

    @pl.when(step + 1 < n_steps)
    def _():
        gather(nxt_ref, 1 - slot)

    def drain(t, carry):
        row_copy(0, slot, 0).wait()
        return carry
    lax.fori_loop(0, tile, drain, 0, unroll=8)

    gate2 = mod_ref[0, :, 5 * d:6 * d]
    o_ref[0] = _ln_rows(alpha * x1_ref[0] + gate2 * ybuf[slot]) * g2_ref[...] + b2_ref[...]


def _combine(ys, pos, x1, mod, g2, b2, *, alpha):
    bsz, n, d = x1.shape
    tile = min(COMBINE_TILE, n)
    nt = n // tile
    n_steps = bsz * nt
    pos_tiles = pos.reshape(n_steps, 1, tile)
    tok = lambda b, i: (b, i, 0)
    smem_tile = lambda index_map: pl.BlockSpec((1, 1, tile), index_map, memory_space=pltpu.SMEM)
    return pl.pallas_call(
        functools.partial(_combine_kernel, tile=tile, d_model=d, alpha=alpha),
        grid=(bsz, nt),
        in_specs=[
            smem_tile(lambda b, i: (b * nt + i, 0, 0)),
            smem_tile(lambda b, i: (jnp.minimum(b * nt + i + 1, n_steps - 1), 0, 0)),
            pl.BlockSpec((1, tile, d), tok),
            pl.BlockSpec((1, 1, N_MOD * d), lambda b, i: (b, 0, 0)),
            pl.BlockSpec((1, d), lambda b, i: (0, 0)),
            pl.BlockSpec((1, d), lambda b, i: (0, 0)),
            pl.BlockSpec(memory_space=pl.ANY),
        ],
        out_specs=pl.BlockSpec((1, tile, d), tok),
        out_shape=jax.ShapeDtypeStruct((bsz, n, d), F32),
        scratch_shapes=[pltpu.VMEM((2, tile, d), F32), pltpu.SemaphoreType.DMA((2,))],
        compiler_params=_params(("arbitrary", "arbitrary")),
        name="moe_combine",
    )(pos_tiles, pos_tiles, x1, mod, g2, b2, ys)


def _moe(streams, hs, w1, w3, w2, g2, b2, *, alpha, layer):
    pos_streams, tile_lo, tile_hi, n_valid = _dispatch_plan(
        [s[1] for s in streams], [s[2] for s in streams], hs.shape[0] // MOE_ROW_TILE)
    for (h2g, _, _, _, _), pos in zip(streams, pos_streams):
        hs = _dispatch(h2g, pos, hs)
    ys = _experts(hs, tile_lo, tile_hi, n_valid, w1, w3, w2, layer=layer)
    outs = [_combine(ys, pos, x1, mod, g2, b2, alpha=alpha)
            for (_, _, _, x1, mod), pos in zip(streams, pos_streams)]
    return outs, hs


def _rope_tables(n_tokens):
    rows = n_tokens // GRID_W
    row = jnp.repeat(jnp.arange(rows, dtype=F32), GRID_W)
    col = jnp.tile(jnp.arange(GRID_W, dtype=F32), rows)
    n_freq = QK_DIM // 4
    inv_freq = ROPE_BASE ** (-jnp.arange(n_freq, dtype=F32) / n_freq)
    ang_r = row[:, None] * inv_freq
    ang_c = col[:, None] * inv_freq
    cos64 = jnp.concatenate([jnp.cos(ang_r), jnp.cos(ang_r), jnp.cos(ang_c), jnp.cos(ang_c)], axis=-1)
    sin64 = jnp.concatenate([-jnp.sin(ang_r), jnp.sin(ang_r), -jnp.sin(ang_c), jnp.sin(ang_c)], axis=-1)
    return jnp.tile(cos64, (1, 2)), jnp.tile(sin64, (1, 2))


def kernel(x, c, ctx, c_ctx, w_mod, b_mod, w_in, sgu_g, sgu_b, w_s, b_s, lambda_q, lambda_k, subln_g,
           w_pa, w_pb, w_o, ln1_g, ln1_b, w_router, b_router, w1, w3, w2, ln2_g, ln2_b):
    bsz, n_lat, d = x.shape
    depth = w_mod.shape[0]
    alpha = (2.0 * depth) ** 0.25
    cos_t, sin_t = _rope_tables(n_lat)

    cond = jnp.zeros((8, d), F32).at[0:bsz].set(c).at[bsz].set(c_ctx)
    w_r_pad = jnp.pad(w_router, ((0, 0), (0, GATE_LANES - N_EXPERTS)))
    w_r_hi, w_r_lo = _split_bf16(w_r_pad)
    b_r_b = jnp.broadcast_to(b_router[:, None], (N_EXPERTS, 128))
    row = lambda v: v.reshape(1, -1)
    n_moe_rows = bsz * (n_lat + ctx.shape[1]) + len(PAIR_CLASSES) * MOE_ROW_TILE
    moe_rows = jnp.zeros((n_moe_rows, d + GATE_LANES), F32)

    for l in range(depth):
        last = l == depth - 1
        lam_init = 0.8 - 0.6 * math.exp(-0.3 * l)
        mod = _modulation(cond, w_mod[l], b_mod[l])
        mod_lat = mod[0:bsz, None, :]
        mod_ctx = jnp.broadcast_to(mod[bsz][None, None, :], (bsz, 1, N_MOD * d))
        w_in_l = w_in[l].astype(BF16)
        w_s_l = w_s[l].astype(BF16)
        b_s_b = jnp.broadcast_to(b_s[l][:, :, None], (A_GROUPS, CHUNK, CHUNK))
        proj_args = (w_in_l, row(sgu_g[l]), row(sgu_b[l]), w_s_l, b_s_b, cos_t, sin_t)
        attn_args = (lambda_q[l], lambda_k[l], row(subln_g[l]))
        merge_w = (w_pa[l].astype(BF16), w_pb[l].astype(BF16), w_o[l].astype(BF16),
                   row(ln1_g[l]), row(ln1_b[l]), w_r_hi, w_r_lo, b_r_b)
        moe_w = (w1, w3, w2, row(ln2_g[l]), row(ln2_b[l]))

        ya, qt, k, vt, ga, gb = _inproj(x, mod_lat, *proj_args, use_rope=True)
        cya, cqt, ck, cvt, cga, cgb = _inproj(ctx, mod_ctx, *proj_args, use_rope=False)
        yb = _attention(qt, k, vt, ck, cvt, *attn_args, lam_init=lam_init)
        x1, h2g, route, counts = _merge(x, ya, yb, ga, gb, mod_lat, *merge_w, alpha=alpha)
        streams = [(h2g, route, counts, x1, mod_lat)]
        if not last:
            cyb = _attention(cqt, None, None, ck, cvt, *attn_args, lam_init=lam_init)
            c1, ch2g, croute, ccounts = _merge(ctx, cya, cyb, cga, cgb, mod_ctx, *merge_w, alpha=alpha)
            streams.append((ch2g, croute, ccounts, c1, mod_ctx))
        outs, moe_rows = _moe(streams, moe_rows, *moe_w, alpha=alpha, layer=l)
        x = outs[0]
        if not last:
            ctx = outs[1]
    return x
```

```python
import functools
import math

import jax
import jax.numpy as jnp
from jax import lax
from jax.experimental import pallas as pl
from jax.experimental.pallas import tpu as pltpu

F32 = jnp.float32
BF16 = jnp.bfloat16

GRID_W = 64
CHUNK = 128
A_GROUPS = 8
A_WIDTH = 1024
HEADS = 8
QK_DIM = 64
V_DIM = 2 * QK_DIM
HEAD_COLS = HEADS * V_DIM
ROPE_BASE = 10000.0
N_EXPERTS = 16
N_GROUPS = 4
EXPERTS_PER_GROUP = N_EXPERTS // N_GROUPS
N_MOD = 6
PAIR_CLASSES = tuple((4 * g + a, 4 * g + b) for g in range(N_GROUPS)
                     for a in range(EXPERTS_PER_GROUP) for b in range(a + 1, EXPERTS_PER_GROUP))
PAIRS_PER_GROUP = len(PAIR_CLASSES) // N_GROUPS
CLASS_ROWS = 32
ROUTE_ROWS = 8
GATE_LANES = 128
LN_EPS = 1e-5
RMS_EPS = 1e-5
NEG_BIG = -1e30

VMEM_LIMIT_BYTES = 56 * 1024 * 1024

TOKEN_TILE = 256
ATTN_Q_TILE = 512
ATTN_KV_CHUNK = 512
ATTN_UNROLL = 6
SUM_ROWS = 16
MOE_TOKEN_TILE = TOKEN_TILE
DISPATCH_TILE = 1024
COMBINE_TILE = 512
MOE_ROW_TILE = 256


def _params(semantics):
    return pltpu.CompilerParams(dimension_semantics=semantics, vmem_limit_bytes=VMEM_LIMIT_BYTES)


def _resident(shape):
    nd = len(shape)
    return pl.BlockSpec(shape, lambda *_: (0,) * nd, pipeline_mode=pl.Buffered(1))


def _split_bf16(x):
    bits = lax.bitcast_convert_type(x, jnp.uint32) & jnp.uint32(0xFFFF0000)
    hi = lax.bitcast_convert_type(bits, F32)
    return hi.astype(BF16), (x - hi).astype(BF16)


def _ln_rows(x):
    mu = jnp.mean(x, axis=-1, keepdims=True)
    xc = x - mu
    var = jnp.mean(xc * xc, axis=-1, keepdims=True)
    return xc * lax.rsqrt(var + LN_EPS)


def _mod_kernel(cond_ref, w_ref, b_ref, o_ref):
    c = cond_ref[...]
    s = c * jax.nn.sigmoid(c)
    o_ref[...] = jnp.dot(s, w_ref[...], preferred_element_type=F32, precision=lax.Precision.HIGHEST) + b_ref[...]


def _modulation(cond, w_mod, b_mod):
    rows, d = cond.shape
    return pl.pallas_call(
        _mod_kernel,
        grid=(N_MOD,),
        in_specs=[
            pl.BlockSpec((rows, d), lambda j: (0, 0)),
            pl.BlockSpec((d, d), lambda j: (0, j)),
            pl.BlockSpec((1, d), lambda j: (0, j)),
        ],
        out_specs=pl.BlockSpec((rows, d), lambda j: (0, j)),
        out_shape=jax.ShapeDtypeStruct((rows, N_MOD * d), F32),
        compiler_params=_params(("arbitrary",)),
        name="modulation",
    )(cond, w_mod, b_mod.reshape(1, -1))


def _rope(x, cos, sin_signed):
    n = x.shape[-1]
    reps = n // cos.shape[-1]
    c = jnp.tile(cos, (1, reps))
    s = jnp.tile(sin_signed, (1, reps))
    lane = lax.broadcasted_iota(jnp.int32, x.shape, 1)
    first = (lane & 31) < 16
    partner = jnp.where(first, pltpu.roll(x, n - 16, 1), pltpu.roll(x, 16, 1))
    return x * c + partner * s


def _inproj_kernel(x_ref, mod_ref, w_ref, sg_ref, sb_ref, ws_ref, bs_ref, cos_ref, sin_ref,
                   ya_ref, qt_ref, k_ref, vt_ref, ga_ref, gb_ref, *, d_model, use_rope, q_scale):
    d = d_model
    tm = x_ref.shape[1]
    shift = mod_ref[0, :, 0:d]
    scale = mod_ref[0, :, d:2 * d]
    h = (_ln_rows(x_ref[0]) * (1.0 + scale) + shift).astype(BF16)

    def proj(c0, width):
        return jnp.dot(h, w_ref[:, c0:c0 + width], preferred_element_type=F32)

    c_u, c_v = 0, A_WIDTH
    c_q = 2 * A_WIDTH
    c_k = c_q + HEAD_COLS
    c_vb = c_k + HEAD_COLS
    c_ga = c_vb + HEAD_COLS
    c_gb = c_ga + d

    v = jax.nn.gelu(proj(c_v, A_WIDTH))
    vn = (_ln_rows(v) * sg_ref[...] + sb_ref[...]).astype(BF16)
    gd = A_WIDTH // A_GROUPS
    for g2 in range(A_GROUPS // 2):
        u = jax.nn.gelu(proj(c_u + 2 * g2 * gd, 2 * gd))
        for pair in range(tm // (2 * CHUNK)):
            r0 = pair * 2 * CHUNK
            r1 = r0 + CHUNK
            for sub in range(2):
                g = 2 * g2 + sub
                cols = slice(g * gd, (g + 1) * gd)
                ucols = slice(sub * gd, (sub + 1) * gd)
                rhs = jnp.concatenate([vn[r0:r0 + CHUNK, cols], vn[r1:r1 + CHUNK, cols]], axis=1)
                y = jnp.dot(ws_ref[g], rhs, preferred_element_type=F32)
                bias = bs_ref[g]
                ya_ref[0, r0:r0 + CHUNK, cols] = (u[r0:r0 + CHUNK, ucols] * (y[:, :gd] + bias)).astype(BF16)
                ya_ref[0, r1:r1 + CHUNK, cols] = (u[r1:r1 + CHUNK, ucols] * (y[:, gd:] + bias)).astype(BF16)

    q = proj(c_q, HEAD_COLS)
    k = proj(c_k, HEAD_COLS)
    if use_rope:
        q = _rope(q, cos_ref[...], sin_ref[...])
        k = _rope(k, cos_ref[...], sin_ref[...])
    q = q * q_scale
    k_ref[0] = k.astype(BF16)
    vb = proj(c_vb, HEAD_COLS)
    for hd in range(HEADS):
        cols = slice(hd * V_DIM, (hd + 1) * V_DIM)
        qt_ref[0, hd] = q[:, cols].T.astype(BF16)
        vt_ref[0, hd] = vb[:, cols].T.astype(BF16)

    ga_ref[0] = jax.nn.sigmoid(proj(c_ga, d)).astype(BF16)
    gb_ref[0] = jax.nn.sigmoid(proj(c_gb, d)).astype(BF16)


def _inproj(x, mod, w_in, sgu_g, sgu_b, w_s, b_s_b, cos_t, sin_t, *, use_rope):
    bsz, n, d = x.shape
    tm = min(TOKEN_TILE, n)
    nt = n // tm
    in_cols = w_in.shape[1]
    q_scale = (QK_DIM ** -0.5) * math.log2(math.e)
    tok = lambda b, i: (b, i, 0)
    tr = lambda b, i: (b, 0, 0, i)
    rope_map = (lambda b, i: (i, 0)) if use_rope else (lambda b, i: (0, 0))
    return pl.pallas_call(
        functools.partial(_inproj_kernel, d_model=d, use_rope=use_rope, q_scale=q_scale),
        grid=(bsz, nt),
        in_specs=[
            pl.BlockSpec((1, tm, d), tok),
            pl.BlockSpec((1, 1, N_MOD * d), lambda b, i: (b, 0, 0)),
            _resident((d, in_cols)),
            _resident((1, A_WIDTH)),
            _resident((1, A_WIDTH)),
            _resident((A_GROUPS, CHUNK, CHUNK)),
            _resident((A_GROUPS, CHUNK, CHUNK)),
            pl.BlockSpec((tm, 2 * QK_DIM), rope_map),
            pl.BlockSpec((tm, 2 * QK_DIM), rope_map),
        ],
        out_specs=[
            pl.BlockSpec((1, tm, A_WIDTH), tok),
            pl.BlockSpec((1, HEADS, V_DIM, tm), tr),
            pl.BlockSpec((1, tm, HEAD_COLS), tok),
            pl.BlockSpec((1, HEADS, V_DIM, tm), tr),
            pl.BlockSpec((1, tm, d), tok),
            pl.BlockSpec((1, tm, d), tok),
        ],
        out_shape=[
            jax.ShapeDtypeStruct((bsz, n, A_WIDTH), BF16),
            jax.ShapeDtypeStruct((bsz, HEADS, V_DIM, n), BF16),
            jax.ShapeDtypeStruct((bsz, n, HEAD_COLS), BF16),
            jax.ShapeDtypeStruct((bsz, HEADS, V_DIM, n), BF16),
            jax.ShapeDtypeStruct((bsz, n, d), BF16),
            jax.ShapeDtypeStruct((bsz, n, d), BF16),
        ],
        compiler_params=_params(("parallel", "parallel")),
        name="inproj_rope" if use_rope else "inproj_ctx",
    )(x, mod, w_in, sgu_g, sgu_b, w_s, b_s_b, cos_t, sin_t)


def _attn_kernel(*refs, n_lat_chunks, tk, lam_init):
    if n_lat_chunks:
        (qt_ref, kl_ref, vtl_ref, kc_ref, vtc_ref, lq_ref, lk_ref, sg_ref,
         o_ref, rhs_ref, s_ref, cm_ref, m_ref, acc_ref) = refs
    else:
        (qt_ref, kc_ref, vtc_ref, lq_ref, lk_ref, sg_ref,
         o_ref, rhs_ref, s_ref, cm_ref, m_ref, acc_ref) = refs
    tq = qt_ref.shape[-1]
    nc = kc_ref.shape[1]

    qt = qt_ref[0, 0]
    row = lax.broadcasted_iota(jnp.int32, qt.shape, 0)
    zero = jnp.zeros_like(qt)
    rhs_ref[:, 0:tq] = jnp.where(row < QK_DIM, qt, zero)
    rhs_ref[:, tq:2 * tq] = jnp.where(row >= QK_DIM, qt, zero)
    m_ref[...] = jnp.full(m_ref.shape, NEG_BIG, F32)
    acc_ref[...] = jnp.zeros(acc_ref.shape, F32)

    def stage_a(kc, slot, rows):
        s = jnp.dot(kc, rhs_ref[...], preferred_element_type=F32)
        s_ref[slot, 0:rows, :] = s
        cm_ref[slot] = jnp.max(s, axis=0, keepdims=True)

    def stage_b(vtc, slot, rows):
        m_old = m_ref[...]
        m_new = jnp.maximum(m_old, cm_ref[slot])
        alpha = jnp.exp2(m_old - m_new)
        p = jnp.exp2(s_ref[slot, 0:rows, :] - m_new)
        vt_ones = jnp.concatenate([vtc, jnp.ones((SUM_ROWS, rows), BF16)], axis=0)
        acc_ref[...] = alpha * acc_ref[...] + jnp.dot(vt_ones, p.astype(BF16), preferred_element_type=F32)
        m_ref[...] = m_new

    if n_lat_chunks:
        def lat_k(c):
            return kl_ref[0, pl.ds(pl.multiple_of(c * tk, tk), tk), :]

        def lat_vt(c):
            return vtl_ref[0, 0, :, pl.ds(pl.multiple_of(c * tk, tk), tk)]

        stage_a(lat_k(0), 0, tk)

        def pipelined(c0, n):
            for i in range(n):
                c = c0 + i
                if isinstance(c, int) and c + 1 == n_lat_chunks:
                    stage_a(kc_ref[0], (i + 1) % 2, nc)
                else:
                    stage_a(lat_k(c + 1), (i + 1) % 2, tk)
                stage_b(lat_vt(c), i % 2, tk)

        def body(j, carry):
            pipelined(ATTN_UNROLL * j, ATTN_UNROLL)
            return carry
        n_loop = (n_lat_chunks - 2) // ATTN_UNROLL
        lax.fori_loop(0, n_loop, body, 0)
        pipelined(n_loop * ATTN_UNROLL, n_lat_chunks - n_loop * ATTN_UNROLL)
        stage_b(vtc_ref[0, 0], n_lat_chunks % 2, nc)
    else:
        stage_a(kc_ref[0], 0, nc)
        stage_b(vtc_ref[0, 0], 0, nc)

    lq = lq_ref[...]
    lk = lk_ref[...]
    lam = (jnp.exp(jnp.sum(lq[0:1] * lk[0:1], keepdims=True))
           - jnp.exp(jnp.sum(lq[1:2] * lk[1:2], keepdims=True)) + lam_init)
    acc = acc_ref[0:V_DIM, :]
    l = acc_ref[V_DIM:V_DIM + 1, :]
    o = acc[:, 0:tq] / l[:, 0:tq] - lam * (acc[:, tq:2 * tq] / l[:, tq:2 * tq])
    ms = jnp.mean(o * o, axis=0, keepdims=True)
    on = o * lax.rsqrt(ms + RMS_EPS)
    o_ref[0] = (on.T * sg_ref[...] * (1.0 - lam_init)).astype(BF16)


def _attention(qt, k_lat, vt_lat, k_ctx, vt_ctx, lq, lk, subln_g, *, lam_init):
    bsz, _, _, nq = qt.shape
    nc = k_ctx.shape[1]
    tq = min(ATTN_Q_TILE, nq)
    has_lat = k_lat is not None
    tk = ATTN_KV_CHUNK
    n_lat_chunks = (k_lat.shape[1] // tk) if has_lat else 0
    assert n_lat_chunks % 2 == 0 and ATTN_UNROLL % 2 == 0
    s_rows = tk if has_lat else nc
    in_specs = [pl.BlockSpec((1, 1, V_DIM, tq), lambda b, h, i: (b, h, 0, i))]
    args = [qt]
    if has_lat:
        ns = k_lat.shape[1]
        in_specs += [pl.BlockSpec((1, ns, V_DIM), lambda b, h, i: (b, 0, h)),
                     pl.BlockSpec((1, 1, V_DIM, ns), lambda b, h, i: (b, h, 0, 0))]
        args += [k_lat, vt_lat]
    in_specs += [pl.BlockSpec((1, nc, V_DIM), lambda b, h, i: (b, 0, h)),
                 pl.BlockSpec((1, 1, V_DIM, nc), lambda b, h, i: (b, h, 0, 0)),
                 pl.BlockSpec((2, QK_DIM), lambda b, h, i: (0, 0)),
                 pl.BlockSpec((2, QK_DIM), lambda b, h, i: (0, 0)),
                 pl.BlockSpec((1, V_DIM), lambda b, h, i: (0, 0))]
    args += [k_ctx, vt_ctx, lq, lk, subln_g]
    return pl.pallas_call(
        functools.partial(_attn_kernel, n_lat_chunks=n_lat_chunks, tk=tk, lam_init=lam_init),
        grid=(bsz, HEADS, nq // tq),
        in_specs=in_specs,
        out_specs=pl.BlockSpec((1, tq, V_DIM), lambda b, h, i: (b, i, h)),
        out_shape=jax.ShapeDtypeStruct((bsz, nq, HEAD_COLS), BF16),
        scratch_shapes=[
            pltpu.VMEM((V_DIM, 2 * tq), BF16),
            pltpu.VMEM((2, s_rows, 2 * tq), F32),
            pltpu.VMEM((2, 1, 2 * tq), F32),
            pltpu.VMEM((1, 2 * tq), F32),
            pltpu.VMEM((V_DIM + SUM_ROWS, 2 * tq), F32),
        ],
        compiler_params=_params(("parallel", "parallel", "arbitrary")),
        name="diff_attn_latent" if has_lat else "diff_attn_ctx",
    )(*args)


def _route_rows(logits_t):
    mx = jnp.max(logits_t, axis=0, keepdims=True)
    ex = jnp.exp(logits_t - mx)
    probs = ex / jnp.sum(ex, axis=0, keepdims=True)
    p = [probs[e:e + 1, :] for e in range(N_EXPERTS)]
    scores = []
    for g in range(N_GROUPS):
        a, b, c, d = p[4 * g:4 * g + 4]
        hi1, lo1 = jnp.maximum(a, b), jnp.minimum(a, b)
        hi2, lo2 = jnp.maximum(c, d), jnp.minimum(c, d)
        top1 = jnp.maximum(hi1, hi2)
        top2 = jnp.maximum(jnp.minimum(hi1, hi2), jnp.maximum(lo1, lo2))
        scores.append(top1 + top2)
    best = jnp.zeros_like(scores[0], dtype=jnp.int32)
    best_score = scores[0]
    for g in range(1, N_GROUPS):
        better = scores[g] > best_score
        best = jnp.where(better, g, best)
        best_score = jnp.where(better, scores[g], best_score)
    sel = []
    for e in range(N_EXPERTS):
        g = e // EXPERTS_PER_GROUP
        rank = jnp.zeros_like(best)
        for j in range(g * EXPERTS_PER_GROUP, (g + 1) * EXPERTS_PER_GROUP):
            if j == e:
                continue
            ahead = (p[j] >= p[e]) if j < e else (p[j] > p[e])
            rank = rank + ahead.astype(jnp.int32)
        sel.append((best == g) & (rank < 2))
    kept = [jnp.where(sel[e], p[e], 0.0) for e in range(N_EXPERTS)]
    denom = kept[0]
    for e in range(1, N_EXPERTS):
        denom = denom + kept[e]
    return sel, [kp / denom for kp in kept]


def _merge_kernel(x_ref, ya_ref, yb_ref, ga_ref, gb_ref, mod_ref, wpa_ref, wpb_ref, wo_ref,
                  g1_ref, b1_ref, wrh_ref, wrl_ref, br_ref, x1_ref, h2_ref, route_ref, counts_ref,
                  *, d_model, alpha):
    d = d_model
    tm = x_ref.shape[1]
    gate1 = mod_ref[0, :, 2 * d:3 * d]
    shift2 = mod_ref[0, :, 3 * d:4 * d]
    scale2 = mod_ref[0, :, 4 * d:5 * d]
    a = jnp.dot(ya_ref[0], wpa_ref[...], preferred_element_type=F32)
    b = jnp.dot(yb_ref[0], wpb_ref[...], preferred_element_type=F32)
    merged = (ga_ref[0].astype(F32) * a + gb_ref[0].astype(F32) * b).astype(BF16)
    mix = jnp.dot(merged, wo_ref[...], preferred_element_type=F32)
    x1 = _ln_rows(alpha * x_ref[0] + gate1 * mix) * g1_ref[...] + b1_ref[...]
    x1_ref[0] = x1
    h2 = _ln_rows(x1) * (1.0 + scale2) + shift2
    h2_ref[0, :, 0:d] = h2
    h2_hi, h2_lo = _split_bf16(h2)
    logits = (jnp.dot(h2_hi, wrh_ref[...], preferred_element_type=F32)
              + jnp.dot(h2_lo, wrh_ref[...], preferred_element_type=F32)
              + jnp.dot(h2_hi, wrl_ref[...], preferred_element_type=F32))
    logits_t = logits.T[0:N_EXPERTS, :] + jnp.tile(br_ref[...], (1, tm // br_ref.shape[1]))
    sel, gate_rows = _route_rows(logits_t)

    e_lo = jnp.full((1, tm), N_EXPERTS, jnp.int32)
    e_hi = jnp.full((1, tm), -1, jnp.int32)
    for e in range(N_EXPERTS):
        e_lo = jnp.where(sel[e], jnp.minimum(e_lo, e), e_lo)
        e_hi = jnp.where(sel[e], jnp.maximum(e_hi, e), e_hi)
    g_lo = jnp.zeros((1, tm), F32)
    g_hi = jnp.zeros((1, tm), F32)
    for e in range(N_EXPERTS):
        g_lo = jnp.where(e_lo == e, gate_rows[e], g_lo)
        g_hi = jnp.where(e_hi == e, gate_rows[e], g_hi)
    a = e_lo & (EXPERTS_PER_GROUP - 1)
    b = e_hi & (EXPERTS_PER_GROUP - 1)
    cls = (e_lo >> 2) * PAIRS_PER_GROUP + ((a * (7 - a)) >> 1) + (b - a - 1)
    cls_t = jnp.where(lax.broadcasted_iota(jnp.int32, (CLASS_ROWS, tm), 0) == cls, 1.0, 0.0)
    earlier = (lax.broadcasted_iota(jnp.int32, (tm, tm), 0)
               < lax.broadcasted_iota(jnp.int32, (tm, tm), 1))
    rank_t = jnp.dot(cls_t.astype(BF16), jnp.where(earlier, 1.0, 0.0).astype(BF16),
                     preferred_element_type=F32)
    rank = jnp.sum(cls_t * rank_t, axis=0, keepdims=True)
    route_ref[0] = jnp.concatenate(
        [cls, rank.astype(jnp.int32), jnp.zeros((ROUTE_ROWS - 2, tm), jnp.int32)], axis=0)
    h2_ref[0, :, d:d + GATE_LANES] = jnp.concatenate(
        [g_lo, g_hi, jnp.zeros((GATE_LANES - 2, tm), F32)], axis=0).T
    counts = jnp.sum(cls_t, axis=1, keepdims=True)
    counts_ref[0, 0] = jnp.broadcast_to(counts, (CLASS_ROWS, 128)).astype(jnp.int32)


def _merge(x, ya, yb, ga, gb, mod, w_pa, w_pb, w_o, g1, b1, w_r_hi, w_r_lo, b_r_b, *, alpha):
    bsz, n, d = x.shape
    tm = min(TOKEN_TILE, n)
    nt = n // tm
    tok = lambda b, i: (b, i, 0)
    return pl.pallas_call(
        functools.partial(_merge_kernel, d_model=d, alpha=alpha),
        grid=(bsz, nt),
        in_specs=[
            pl.BlockSpec((1, tm, d), tok),
            pl.BlockSpec((1, tm, A_WIDTH), tok),
            pl.BlockSpec((1, tm, HEAD_COLS), tok),
            pl.BlockSpec((1, tm, d), tok),
            pl.BlockSpec((1, tm, d), tok),
            pl.BlockSpec((1, 1, N_MOD * d), lambda b, i: (b, 0, 0)),
            _resident((A_WIDTH, d)),
            _resident((HEAD_COLS, d)),
            _resident((d, d)),
            _resident((1, d)),
            _resident((1, d)),
            _resident((d, GATE_LANES)),
            _resident((d, GATE_LANES)),
            _resident((N_EXPERTS, 128)),
        ],
        out_specs=[
            pl.BlockSpec((1, tm, d), tok),
            pl.BlockSpec((1, tm, d + GATE_LANES), tok),
            pl.BlockSpec((1, ROUTE_ROWS, tm), lambda b, i: (b, 0, i)),
            pl.BlockSpec((1, 1, CLASS_ROWS, 128), lambda b, i: (b, i, 0, 0)),
        ],
        out_shape=[
            jax.ShapeDtypeStruct((bsz, n, d), F32),
            jax.ShapeDtypeStruct((bsz, n, d + GATE_LANES), F32),
            jax.ShapeDtypeStruct((bsz, ROUTE_ROWS, n), jnp.int32),
            jax.ShapeDtypeStruct((bsz, nt, CLASS_ROWS, 128), jnp.int32),
        ],
        compiler_params=_params(("parallel", "parallel")),
        name="merge_route",
    )(x, ya, yb, ga, gb, mod, w_pa, w_pb, w_o, g1, b1, w_r_hi, w_r_lo, b_r_b)


def _dispatch_plan(routes, counts_list, n_tiles_max):
    tm = MOE_TOKEN_TILE
    n_srcs = [c.shape[0] * c.shape[1] for c in counts_list]
    n_src = sum(n_srcs)
    cnt = jnp.concatenate([c[..., 0].reshape(-1, CLASS_ROWS) for c in counts_list], axis=0)
    cls = jnp.concatenate([r[:, 0].reshape(-1, tm) for r in routes], axis=0)
    rank = jnp.concatenate([r[:, 1].reshape(-1, tm) for r in routes], axis=0)
    total = jnp.sum(cnt, axis=0)
    tiles_per_class = (total + MOE_ROW_TILE - 1) // MOE_ROW_TILE
    classes = jnp.arange(CLASS_ROWS, dtype=jnp.int32)
    tile_end = jnp.sum(jnp.where(classes[None, :] <= classes[:, None], tiles_per_class[None, :], 0), axis=1)
    seg_start = (tile_end - tiles_per_class) * MOE_ROW_TILE
    src = jnp.arange(n_src, dtype=jnp.int32)
    before = jnp.sum(jnp.where((src[None, :] < src[:, None])[:, :, None], cnt[None, :, :], 0), axis=1)
    base = (seg_start[None, :] + before)[:, None, :]
    pos = jnp.sum(jnp.where(cls[..., None] == classes, base, 0), axis=-1) + rank
    tile_ids = jnp.arange(n_tiles_max, dtype=jnp.int32)
    tile_class = jnp.sum((tile_end[None, :] <= tile_ids[:, None]).astype(jnp.int32), axis=1)
    tile_class = jnp.minimum(tile_class, len(PAIR_CLASSES) - 1)
    pairs = jnp.asarray(PAIR_CLASSES, dtype=jnp.int32)
    tile_pair = jnp.sum(jnp.where((tile_class[:, None] == classes[None, :len(PAIR_CLASSES)])[:, :, None],
                                  pairs[None, :, :], 0), axis=1)
    pos_streams, start = [], 0
    for n_s in n_srcs:
        pos_streams.append(pos[start:start + n_s].reshape(-1))
        start += n_s
    return pos_streams, tile_pair[:, 0], tile_pair[:, 1], tile_end[-1:]


def _dispatch_kernel(pos_ref, h_ref, hs_in_hbm, hs_hbm, sem, *, tile):
    del hs_in_hbm

    def row_copy(t, dst_row):
        return pltpu.make_async_copy(h_ref.at[0, pl.ds(t, 1), :], hs_hbm.at[pl.ds(dst_row, 1), :], sem)

    def issue(t, carry):
        row_copy(t, pos_ref[0, 0, t]).start()
        return carry
    lax.fori_loop(0, tile, issue, 0, unroll=8)

    def drain(t, carry):
        row_copy(0, 0).wait()
        return carry
    lax.fori_loop(0, tile, drain, 0, unroll=8)


def _dispatch(h2, pos, hs_init):
    bsz, n, width = h2.shape
    n_rows = hs_init.shape[0]
    tile = min(DISPATCH_TILE, n)
    nt = n // tile
    pos_tiles = pos.reshape(bsz * nt, 1, tile)
    return pl.pallas_call(
        functools.partial(_dispatch_kernel, tile=tile),
        grid=(bsz, nt),
        in_specs=[
            pl.BlockSpec((1, 1, tile), lambda b, i: (b * nt + i, 0, 0), memory_space=pltpu.SMEM),
            pl.BlockSpec((1, tile, width), lambda b, i: (b, i, 0)),
            pl.BlockSpec(memory_space=pl.ANY),
        ],
        out_specs=pl.BlockSpec(memory_space=pl.ANY),
        out_shape=jax.ShapeDtypeStruct((n_rows, width), F32),
        scratch_shapes=[pltpu.SemaphoreType.DMA(())],
        input_output_aliases={2: 0},
        compiler_params=_params(("arbitrary", "arbitrary")),
        name="moe_dispatch",
    )(pos_tiles, h2, hs_init)


def _experts_kernel(lo_ref, hi_ref, nv_ref, hs_ref, w1l_ref, w3l_ref, w2l_ref, w1h_ref, w3h_ref, w2h_ref,
                    ys_ref, w1l_b, w3l_b, w2l_b, w1h_b, w3h_b, w2h_b, *, d_model):
    d = d_model
    j = pl.program_id(0)
    prev = jnp.maximum(j - 1, 0)

    @pl.when((j == 0) | (lo_ref[j] != lo_ref[prev]))
    def _():
        w1l_b[...] = w1l_ref[0, 0].astype(BF16)
        w3l_b[...] = w3l_ref[0, 0].astype(BF16)
        w2l_b[...] = w2l_ref[0, 0].astype(BF16)

    @pl.when((j == 0) | (hi_ref[j] != hi_ref[prev]))
    def _():
        w1h_b[...] = w1h_ref[0, 0].astype(BF16)
        w3h_b[...] = w3h_ref[0, 0].astype(BF16)
        w2h_b[...] = w2h_ref[0, 0].astype(BF16)

    def expert(h, w1, w3, w2):
        a = jnp.dot(h, w1[...], preferred_element_type=F32)
        b = jnp.dot(h, w3[...], preferred_element_type=F32)
        hid = (a * jax.nn.sigmoid(a) * b).astype(BF16)
        return jnp.dot(hid, w2[...], preferred_element_type=F32)

    @pl.when(j < nv_ref[0])
    def _():
        h = hs_ref[:, 0:d].astype(BF16)
        gates = hs_ref[:, d:d + GATE_LANES]
        ys_ref[...] = (gates[:, 0:1] * expert(h, w1l_b, w3l_b, w2l_b)
                       + gates[:, 1:2] * expert(h, w1h_b, w3h_b, w2h_b))

    @pl.when(j >= nv_ref[0])
    def _():
        ys_ref[...] = jnp.zeros(ys_ref.shape, F32)


def _experts(hs, tile_lo, tile_hi, n_valid, w1, w3, w2, *, layer):
    n_rows, width = hs.shape
    d = width - GATE_LANES
    de = w1.shape[-1]
    n_tiles = n_rows // MOE_ROW_TILE
    lo_map = lambda j, lo, hi, nv: (layer, lo[j], 0, 0)
    hi_map = lambda j, lo, hi, nv: (layer, hi[j], 0, 0)
    up, down = (1, 1, d, de), (1, 1, de, d)
    return pl.pallas_call(
        functools.partial(_experts_kernel, d_model=d),
        grid_spec=pltpu.PrefetchScalarGridSpec(
            num_scalar_prefetch=3,
            grid=(n_tiles,),
            in_specs=[
                pl.BlockSpec((MOE_ROW_TILE, width), lambda j, lo, hi, nv: (jnp.minimum(j, nv[0] - 1), 0)),
                pl.BlockSpec(up, lo_map), pl.BlockSpec(up, lo_map), pl.BlockSpec(down, lo_map),
                pl.BlockSpec(up, hi_map), pl.BlockSpec(up, hi_map), pl.BlockSpec(down, hi_map),
            ],
            out_specs=pl.BlockSpec((MOE_ROW_TILE, d), lambda j, lo, hi, nv: (j, 0)),
            scratch_shapes=[pltpu.VMEM((d, de), BF16), pltpu.VMEM((d, de), BF16), pltpu.VMEM((de, d), BF16)] * 2,
        ),
        out_shape=jax.ShapeDtypeStruct((n_rows, d), F32),
        compiler_params=_params(("arbitrary",)),
        name="moe_experts",
    )(tile_lo, tile_hi, n_valid, hs, w1, w3, w2, w1, w3, w2)


def _combine_kernel(pos_ref, nxt_ref, x1_ref, mod_ref, g2_ref, b2_ref, ys_hbm,
                    o_ref, ybuf, sem, *, tile, d_model, alpha):
    d = d_model
    step = pl.program_id(0) * pl.num_programs(1) + pl.program_id(1)
    n_steps = pl.num_programs(0) * pl.num_programs(1)
    slot = step % 2

    def row_copy(src_row, slot_, dst_row):
        return pltpu.make_async_copy(ys_hbm.at[pl.ds(src_row, 1), :],
                                     ybuf.at[slot_, pl.ds(dst_row, 1), :], sem.at[slot_])

    def gather(p_ref, slot_):
        def issue(t, carry):
            row_copy(p_ref[0, 0, t], slot_, t).start()
            return carry
        lax.fori_loop(0, tile, issue, 0, unroll=8)

    @pl.when(step == 0)
    def _():
        gather(pos_ref, 0)

    @pl.when(step + 1 < n_steps)
    def _():
        gather(nxt_ref, 1 - slot)

    def drain(t, carry):
        row_copy(0, slot, 0).wait()
        return carry
    lax.fori_loop(0, tile, drain, 0, unroll=8)

    gate2 = mod_ref[0, :, 5 * d:6 * d]
    o_ref[0] = _ln_rows(alpha * x1_ref[0] + gate2 * ybuf[slot]) * g2_ref[...] + b2_ref[...]


def _combine(ys, pos, x1, mod, g2, b2, *, alpha):
    bsz, n, d = x1.shape
    tile = min(COMBINE_TILE, n)
    nt = n // tile
    n_steps = bsz * nt
    pos_tiles = pos.reshape(n_steps, 1, tile)
    tok = lambda b, i: (b, i, 0)
    smem_tile = lambda index_map: pl.BlockSpec((1, 1, tile), index_map, memory_space=pltpu.SMEM)
    return pl.pallas_call(
        functools.partial(_combine_kernel, tile=tile, d_model=d, alpha=alpha),
        grid=(bsz, nt),
        in_specs=[
            smem_tile(lambda b, i: (b * nt + i, 0, 0)),
            smem_tile(lambda b, i: (jnp.minimum(b * nt + i + 1, n_steps - 1), 0, 0)),
            pl.BlockSpec((1, tile, d), tok),
            pl.BlockSpec((1, 1, N_MOD * d), lambda b, i: (b, 0, 0)),
            pl.BlockSpec((1, d), lambda b, i: (0, 0)),
            pl.BlockSpec((1, d), lambda b, i: (0, 0)),
            pl.BlockSpec(memory_space=pl.ANY),
        ],
        out_specs=pl.BlockSpec((1, tile, d), tok),
        out_shape=jax.ShapeDtypeStruct((bsz, n, d), F32),
        scratch_shapes=[pltpu.VMEM((2, tile, d), F32), pltpu.SemaphoreType.DMA((2,))],
        compiler_params=_params(("arbitrary", "arbitrary")),
        name="moe_combine",
    )(pos_tiles, pos_tiles, x1, mod, g2, b2, ys)


def _moe(streams, hs, w1, w3, w2, g2, b2, *, alpha, layer):
    pos_streams, tile_lo, tile_hi, n_valid = _dispatch_plan(
        [s[1] for s in streams], [s[2] for s in streams], hs.shape[0] // MOE_ROW_TILE)
    for (h2g, _, _, _, _), pos in zip(streams, pos_streams):
        hs = _dispatch(h2g, pos, hs)
    ys = _experts(hs, tile_lo, tile_hi, n_valid, w1, w3, w2, layer=layer)
    outs = [_combine(ys, pos, x1, mod, g2, b2, alpha=alpha)
            for (_, _, _, x1, mod), pos in zip(streams, pos_streams)]
    return outs, hs


def _rope_tables(n_tokens):
    rows = n_tokens // GRID_W
    row = jnp.repeat(jnp.arange(rows, dtype=F32), GRID_W)
    col = jnp.tile(jnp.arange(GRID_W, dtype=F32), rows)
    n_freq = QK_DIM // 4
    inv_freq = ROPE_BASE ** (-jnp.arange(n_freq, dtype=F32) / n_freq)
    ang_r = row[:, None] * inv_freq
    ang_c = col[:, None] * inv_freq
    cos64 = jnp.concatenate([jnp.cos(ang_r), jnp.cos(ang_r), jnp.cos(ang_c), jnp.cos(ang_c)], axis=-1)
    sin64 = jnp.concatenate([-jnp.sin(ang_r), jnp.sin(ang_r), -jnp.sin(ang_c), jnp.sin(ang_c)], axis=-1)
    return jnp.tile(cos64, (1, 2)), jnp.tile(sin64, (1, 2))


def kernel(x, c, ctx, c_ctx, w_mod, b_mod, w_in, sgu_g, sgu_b, w_s, b_s, lambda_q, lambda_k, subln_g,
           w_pa, w_pb, w_o, ln1_g, ln1_b, w_router, b_router, w1, w3, w2, ln2_g, ln2_b):
    bsz, n_lat, d = x.shape
    depth = w_mod.shape[0]
    alpha = (2.0 * depth) ** 0.25
    cos_t, sin_t = _rope_tables(n_lat)

    cond = jnp.zeros((8, d), F32).at[0:bsz].set(c).at[bsz].set(c_ctx)
    w_r_pad = jnp.pad(w_router, ((0, 0), (0, GATE_LANES - N_EXPERTS)))
    w_r_hi, w_r_lo = _split_bf16(w_r_pad)
    b_r_b = jnp.broadcast_to(b_router[:, None], (N_EXPERTS, 128))
    row = lambda v: v.reshape(1, -1)
    n_moe_rows = bsz * (n_lat + ctx.shape[1]) + len(PAIR_CLASSES) * MOE_ROW_TILE
    moe_rows = jnp.zeros((n_moe_rows, d + GATE_LANES), F32)

    for l in range(depth):
        last = l == depth - 1
        lam_init = 0.8 - 0.6 * math.exp(-0.3 * l)
        mod = _modulation(cond, w_mod[l], b_mod[l])
        mod_lat = mod[0:bsz, None, :]
        mod_ctx = jnp.broadcast_to(mod[bsz][None, None, :], (bsz, 1, N_MOD * d))
        w_in_l = w_in[l].astype(BF16)
        w_s_l = w_s[l].astype(BF16)
        b_s_b = jnp.broadcast_to(b_s[l][:, :, None], (A_GROUPS, CHUNK, CHUNK))
        proj_args = (w_in_l, row(sgu_g[l]), row(sgu_b[l]), w_s_l, b_s_b, cos_t, sin_t)
        attn_args = (lambda_q[l], lambda_k[l], row(subln_g[l]))
        merge_w = (w_pa[l].astype(BF16), w_pb[l].astype(BF16), w_o[l].astype(BF16),
                   row(ln1_g[l]), row(ln1_b[l]), w_r_hi, w_r_lo, b_r_b)
        moe_w = (w1, w3, w2, row(ln2_g[l]), row(ln2_b[l]))

        ya, qt, k, vt, ga, gb = _inproj(x, mod_lat, *proj_args, use_rope=True)
        cya, cqt, ck, cvt, cga, cgb = _inproj(ctx, mod_ctx, *proj_args, use_rope=False)
        yb = _attention(qt, k, vt, ck, cvt, *attn_args, lam_init=lam_init)
        x1, h2g, route, counts = _merge(x, ya, yb, ga, gb, mod_lat, *merge_w, alpha=alpha)
        streams = [(h2g, route, counts, x1, mod_lat)]
        if not last:
            cyb = _attention(cqt, None, None, ck, cvt, *attn_args, lam_init=lam_init)
            c1, ch2g, croute, ccounts = _merge(ctx, cya, cyb, cga, cgb, mod_ctx, *merge_w, alpha=alpha)
            streams.append((ch2g, croute, ccounts, c1, mod_ctx))
        outs, moe_rows = _moe(streams, moe_rows, *moe_w, alpha=alpha, layer=l)
        x = outs[0]
        if not last:
            ctx = outs[1]
    return x
```
